```python
import jax
import jax.numpy as jnp
from jax import lax
import numpy as np

D_MODEL = 4096
BATCH = 4
SEQ = 2048
DEPTH = 2
DEC_BATCH = 8
DEC_SEQ = 1
PAST_LEN = 16384
PAGE_SIZE = 128

D_MIX = D_MODEL
A_WIDTH = D_MIX // 4
A_HEADS = 8
A_HEAD_DIM = A_WIDTH // A_HEADS
A_PATTERNS = ((128, 1), (512, 4), (2048, 16))
A_WIN_MAX = 2048
B_WIDTH = D_MIX // 4
B_POOLS = (2, 4, 8, 16)
B_GROUPS = 4
B_GROUP_W = B_WIDTH // B_GROUPS
B_BUF = 15
C_WIDTH = D_MIX // 2
C_HEADS = 16
C_HEAD_DIM = C_WIDTH // C_HEADS
C_CONV = 4
C_CHUNK = 64
IN_SPLITS = (A_WIDTH, A_WIDTH, A_WIDTH, A_WIDTH, B_WIDTH, B_WIDTH,
             C_WIDTH, C_WIDTH, C_WIDTH, C_WIDTH, C_HEADS, C_HEADS)
D_IN = 4 * A_WIDTH + 2 * B_WIDTH + 4 * C_WIDTH + 2 * C_HEADS
RMS_EPS = 1e-6

kernel_name = 'hybrid_dilated_pool_gdn_step'


def rms_norm(x, gain):
    xf = x.astype(jnp.float32)
    y = xf * lax.rsqrt(jnp.mean(xf * xf, axis=-1, keepdims=True) + RMS_EPS)
    return (y * gain.astype(jnp.float32)).astype(x.dtype)


def l2_norm(x):
    return x * lax.rsqrt(jnp.sum(x * x, axis=-1, keepdims=True) + RMS_EPS)


def split_cols(z):
    outs, off = [], 0
    for n in IN_SPLITS:
        outs.append(z[..., off:off + n])
        off += n
    return outs


def dilated_pattern_prompt(q, k, v, window, dilation):
    b, s, h, dh = q.shape
    n = window // dilation
    s_pad = -(-s // window) * window
    nb = s_pad // window

    def strided(t):
        t = jnp.pad(t, ((0, 0), (0, s_pad - s), (0, 0), (0, 0)))
        return t.reshape(b, nb, n, dilation, h, dh)

    def with_prev(t):
        prev = jnp.pad(t, ((0, 0), (1, 0), (0, 0), (0, 0), (0, 0), (0, 0)))[:, :-1]
        return jnp.concatenate([prev, t], axis=2)

    qs = strided(q)
    kk, vv = with_prev(strided(k)), with_prev(strided(v))
    sc = jnp.einsum('bnqrhd,bnkrhd->bnrhqk', qs, kk, preferred_element_type=jnp.float32) * (dh ** -0.5)
    qi = jnp.arange(n)[:, None]
    kj = jnp.arange(2 * n)[None, :]
    dist = qi + n - kj
    band = (dist >= 0) & (dist <= n)
    mask = band[None] & ((jnp.arange(nb)[:, None, None] > 0) | (kj[None] >= n))
    sc = jnp.where(mask[None, :, None, None], sc, -jnp.inf)
    m = jnp.max(sc, axis=-1)
    p = jnp.exp(sc - m[..., None])
    l = jnp.sum(p, axis=-1)
    o = jnp.einsum('bnrhqk,bnkrhd->bnqrhd', p.astype(v.dtype), vv, preferred_element_type=jnp.float32)
    o = o.reshape(b, s_pad, h, dh)[:, :s]

    def to_pos(t):
        return t.transpose(0, 1, 4, 2, 3).reshape(b, s_pad, h)[:, :s]

    return o, to_pos(m), to_pos(l)


def dilated_pattern_sample(q, k_all, v_all, n_prev, window, dilation):
    b, t, h, dh = q.shape
    n = window // dilation
    idx = n_prev + jnp.arange(t)[:, None] - dilation * jnp.arange(n + 1)[None, :]
    valid = idx >= 0
    idx = jnp.maximum(idx, 0)
    kg = k_all[:, idx]
    vg = v_all[:, idx]
    sc = jnp.einsum('bthd,btkhd->bthk', q, kg, preferred_element_type=jnp.float32) * (dh ** -0.5)
    sc = jnp.where(valid[None, :, None, :], sc, -jnp.inf)
    m = jnp.max(sc, axis=-1)
    p = jnp.exp(sc - m[..., None])
    l = jnp.sum(p, axis=-1)
    o = jnp.einsum('bthk,btkhd->bthd', p.astype(vg.dtype), vg, preferred_element_type=jnp.float32)
    return o, m, l


def combine_patterns(parts):
    m_all = parts[0][1]
    for _, m, _ in parts[1:]:
        m_all = jnp.maximum(m_all, m)
    num, den = 0.0, 0.0
    for o, m, l in parts:
        wgt = jnp.exp(m - m_all)
        num = num + o * wgt[..., None]
        den = den + l * wgt
    return num / den[..., None]


def pool_mix(u_ext, n_prev, start_pos, pool_w, pool_scale):
    b, total, _ = u_ext.shape
    t = total - n_prev
    uf = u_ext.astype(jnp.float32)
    csum = jnp.concatenate([jnp.zeros_like(uf[:, :1]), jnp.cumsum(uf, axis=1)], axis=1)
    rows = n_prev + jnp.arange(t)
    first = max(0, -start_pos)
    outs = []
    for gi, w in enumerate(B_POOLS):
        sl = slice(gi * B_GROUP_W, (gi + 1) * B_GROUP_W)
        c = csum[..., sl]
        lo = jnp.maximum(rows + 1 - w, first)
        mean = (c[:, rows + 1] - c[:, lo]) / (rows + 1 - lo).astype(jnp.float32)[None, :, None]
        outs.append(mean - uf[:, n_prev:, sl])
    y = jnp.stack(outs, axis=2)
    y = jnp.einsum('btgc,gcd->btgd', y, pool_w.astype(jnp.float32)).reshape(b, t, B_WIDTH)
    return y * pool_scale.astype(jnp.float32)


def short_conv(x_ext, conv_w):
    t = x_ext.shape[1] - (C_CONV - 1)
    y = x_ext[:, 0:t] * conv_w[0]
    for i in range(1, C_CONV):
        y = y + x_ext[:, i:i + t] * conv_w[i]
    return jax.nn.silu(y)


def delta_inputs(qkv, a, bg, a_log, dt_bias):
    b, t, _ = qkv.shape
    qkv = qkv.astype(jnp.float32).reshape(b, t, 3, C_HEADS, C_HEAD_DIM)
    q = l2_norm(qkv[:, :, 0]) * (C_HEAD_DIM ** -0.5)
    k = l2_norm(qkv[:, :, 1])
    v = qkv[:, :, 2]
    beta = jax.nn.sigmoid(bg.astype(jnp.float32))
    g = -jnp.exp(a_log.astype(jnp.float32)) * jax.nn.softplus(a.astype(jnp.float32) + dt_bias.astype(jnp.float32))
    return q, k, v, g, beta


def gated_delta_chunked(q, k, v, g, beta, s0):
    b, t, h, dk = q.shape
    dv = v.shape[-1]
    c = C_CHUNK
    n = t // c

    def chunks(x):
        x = x.reshape((b, n, c, h) + x.shape[3:])
        return jnp.moveaxis(x, (1, 3), (0, 2))

    qc, kc, vc = chunks(q), chunks(k), chunks(v)
    gc = jnp.cumsum(chunks(g), axis=-1)
    bc = chunks(beta)
    causal = jnp.arange(c)[:, None] >= jnp.arange(c)[None, :]
    strict = jnp.arange(c)[:, None] > jnp.arange(c)[None, :]
    decay = jnp.exp(jnp.where(causal, gc[..., :, None] - gc[..., None, :], -jnp.inf))
    kb = kc * bc[..., None]
    a_mat = jnp.where(strict, jnp.einsum('nbhid,nbhjd->nbhij', kb, kc) * decay, 0.0)
    lhs = a_mat + jnp.eye(c, dtype=a_mat.dtype)
    rhs = jnp.concatenate([vc * bc[..., None], kb * jnp.exp(gc)[..., None]], axis=-1)
    sol = lax.linalg.triangular_solve(lhs, rhs, left_side=True, lower=True, unit_diagonal=True)
    u, w = sol[..., :dv], sol[..., dv:]
    qk = jnp.einsum('nbhid,nbhjd->nbhij', qc, kc) * decay

    def step(state, xs):
        q_i, k_i, u_i, w_i, qk_i, g_i = xs
        v_new = u_i - jnp.einsum('bhck,bhkv->bhcv', w_i, state)
        o = (jnp.einsum('bhck,bhkv->bhcv', q_i * jnp.exp(g_i)[..., None], state)
             + jnp.einsum('bhij,bhjv->bhiv', qk_i, v_new))
        g_last = g_i[..., -1:]
        state = (state * jnp.exp(g_last)[..., None]
                 + jnp.einsum('bhck,bhcv->bhkv', k_i * jnp.exp(g_last - g_i)[..., None], v_new))
        return state, o

    state, o = lax.scan(step, s0, (qc, kc, u, w, qk, gc))
    o = jnp.moveaxis(o, (0, 2), (1, 3)).reshape(b, t, h, dv)
    return o, state


def gated_delta_recurrent(q, k, v, g, beta, s0):
    def step(state, xs):
        q_t, k_t, v_t, g_t, b_t = xs
        state = state * jnp.exp(g_t)[..., None, None]
        delta = (v_t - jnp.einsum('bhk,bhkv->bhv', k_t, state)) * b_t[..., None]
        state = state + jnp.einsum('bhk,bhv->bhkv', k_t, delta)
        return state, jnp.einsum('bhk,bhkv->bhv', q_t, state)

    xs = tuple(jnp.moveaxis(a, 1, 0) for a in (q, k, v, g, beta))
    state, o = lax.scan(step, s0, xs)
    return jnp.moveaxis(o, 0, 1), state


def trunk_layer(x, past, is_prompt, norm_w, w_in, conv_w, a_log, dt_bias, delta_norm_w, pool_w, pool_scale, w_out):
    b, t, _ = x.shape
    h = rms_norm(x, norm_w)
    z = h @ w_in
    qa, ka, va, ga, ub, gb, qc, kc, vc, gc, ac, bc = split_cols(z)
    qa = qa.reshape(b, t, A_HEADS, A_HEAD_DIM)
    ka = ka.reshape(b, t, A_HEADS, A_HEAD_DIM)
    va = va.reshape(b, t, A_HEADS, A_HEAD_DIM)
    qkv_c = jnp.concatenate([qc, kc, vc], axis=-1)
    if is_prompt:
        parts = [dilated_pattern_prompt(qa, ka, va, w, d) for w, d in A_PATTERNS]
        keep = min(A_WIN_MAX, t)
        new_k, new_v = ka[:, t - keep:], va[:, t - keep:]
        u_ext, n_prev, start = ub, 0, 0
        conv_ext = jnp.pad(qkv_c, ((0, 0), (C_CONV - 1, 0), (0, 0)))
        s0 = jnp.zeros((b, C_HEADS, C_HEAD_DIM, C_HEAD_DIM), jnp.float32)
    else:
        win_k, win_v, pool_buf, conv_buf, s0 = past
        n_buf = win_k.shape[1]
        k_all = jnp.concatenate([win_k.astype(ka.dtype), ka], axis=1)
        v_all = jnp.concatenate([win_v.astype(va.dtype), va], axis=1)
        parts = [dilated_pattern_sample(qa, k_all, v_all, n_buf, w, d) for w, d in A_PATTERNS]
        keep = min(A_WIN_MAX, n_buf + t)
        new_k, new_v = k_all[:, n_buf + t - keep:], v_all[:, n_buf + t - keep:]
        n_prev = pool_buf.shape[1]
        u_ext, start = jnp.concatenate([pool_buf.astype(ub.dtype), ub], axis=1), PAST_LEN - n_prev
        conv_ext = jnp.concatenate([conv_buf.astype(qkv_c.dtype), qkv_c], axis=1)
        s0 = s0.astype(jnp.float32)
    oa = combine_patterns(parts).reshape(b, t, A_WIDTH)
    ob = pool_mix(u_ext, n_prev, start, pool_w, pool_scale)
    new_pool = u_ext[:, u_ext.shape[1] - B_BUF:]
    new_conv = conv_ext[:, conv_ext.shape[1] - (C_CONV - 1):]
    q, k, v, g, beta = delta_inputs(short_conv(conv_ext, conv_w), ac, bc, a_log, dt_bias)
    if is_prompt:
        oc, s_new = gated_delta_chunked(q, k, v, g, beta, s0)
    else:
        oc, s_new = gated_delta_recurrent(q, k, v, g, beta, s0)
    oc = rms_norm(oc, delta_norm_w).reshape(b, t, C_WIDTH)
    mix = jnp.concatenate([oa * jax.nn.silu(ga), ob * jax.nn.silu(gb), oc * jax.nn.silu(gc)], axis=-1)
    y = x + mix.astype(x.dtype) @ w_out
    return y, (new_k, new_v, new_pool, new_conv, s_new)


def setup_inputs(seed: int = 0) -> dict:
    key = jax.random.key(seed)
    ks = jax.random.split(key, 18)
    f32 = jnp.float32
    a_buf = min(A_WIN_MAX, PAST_LEN)

    def nrm(k, shape, scale=1.0):
        return jax.random.normal(k, shape, f32) * scale

    return {
        'x_prompt': nrm(ks[0], (BATCH, SEQ, D_MODEL)),
        'x_sample': nrm(ks[1], (DEC_BATCH, DEC_SEQ, D_MODEL)),
        'cache_win_k': nrm(ks[2], (DEPTH, DEC_BATCH, a_buf, A_HEADS, A_HEAD_DIM)),
        'cache_win_v': nrm(ks[3], (DEPTH, DEC_BATCH, a_buf, A_HEADS, A_HEAD_DIM)),
        'state_pool': nrm(ks[4], (DEPTH, DEC_BATCH, B_BUF, B_WIDTH)),
        'state_conv': nrm(ks[5], (DEPTH, DEC_BATCH, C_CONV - 1, 3 * C_WIDTH)),
        'state_delta': nrm(ks[6], (DEPTH, DEC_BATCH, C_HEADS, C_HEAD_DIM, C_HEAD_DIM), 0.1),
        'norm_w': 1.0 + nrm(ks[7], (DEPTH, D_MODEL), 0.02),
        'w_in': nrm(ks[8], (DEPTH, D_MODEL, D_IN), D_MODEL ** -0.5),
        'conv_w': nrm(ks[9], (DEPTH, C_CONV, 3 * C_WIDTH), C_CONV ** -0.5),
        'a_log': jnp.log(jax.random.uniform(ks[10], (DEPTH, C_HEADS), f32, 1.0, 16.0)),
        'dt_bias': nrm(ks[11], (DEPTH, C_HEADS), 0.1),
        'delta_norm_w': 1.0 + nrm(ks[12], (DEPTH, C_HEAD_DIM), 0.02),
        'pool_w': nrm(ks[13], (DEPTH, B_GROUPS, B_GROUP_W, B_GROUP_W), B_GROUP_W ** -0.5),
        'pool_scale': 1.0 + nrm(ks[14], (DEPTH, B_WIDTH), 0.02),
        'w_out': nrm(ks[15], (DEPTH, D_MIX, D_MODEL), D_MIX ** -0.5),
        'final_norm_w': 1.0 + nrm(ks[16], (D_MODEL,), 0.02),
    }


def reference(x_prompt, x_sample, cache_win_k, cache_win_v, state_pool, state_conv, state_delta,
              norm_w, w_in, conv_w, a_log, dt_bias, delta_norm_w, pool_w, pool_scale, w_out, final_norm_w):
    hp, hs = x_prompt, x_sample
    sp_all, ss_all = [], []
    for l in range(DEPTH):
        lw = (norm_w[l], w_in[l], conv_w[l], a_log[l], dt_bias[l], delta_norm_w[l], pool_w[l], pool_scale[l], w_out[l])
        hp, sp = trunk_layer(hp, None, True, *lw)
        past = (cache_win_k[l], cache_win_v[l], state_pool[l], state_conv[l], state_delta[l])
        hs, ss = trunk_layer(hs, past, False, *lw)
        sp_all.append(sp)
        ss_all.append(ss)
    y_prompt = rms_norm(hp, final_norm_w)
    y_sample = rms_norm(hs, final_norm_w)

    def stack(states, i):
        return jnp.stack([s[i] for s in states], axis=0)

    return (y_prompt, y_sample,
            stack(sp_all, 0), stack(sp_all, 1), stack(sp_all, 2), stack(sp_all, 3), stack(sp_all, 4),
            stack(ss_all, 0), stack(ss_all, 1), stack(ss_all, 2), stack(ss_all, 3), stack(ss_all, 4))
```

```python
import functools

import jax
import jax.numpy as jnp
from jax import lax
from jax.experimental import pallas as pl
from jax.experimental.pallas import tpu as pltpu

F32 = jnp.float32
BF16 = jnp.bfloat16

LANE = 128
MXU_ROWS = 16
VMEM_LIMIT = 56 * 2**20
RMS_EPS = 1e-6

A_HEADS = 8
A_PATTERNS = ((128, 1), (512, 4), (2048, 16))
A_BLOCK = 128
B_POOLS = (2, 4, 8, 16)
B_BUF = 15
C_HEADS = 16
C_CONV = 4
C_CHUNK = 64

CB_QA, CB_KA, CB_VA, CB_GA = 0, 8, 16, 24
CB_UB, CB_GB = 32, 40
CB_QC, CB_KC, CB_VC, CB_GC = 48, 64, 80, 96
CB_TOTAL = 112
N_MAIN = CB_TOTAL * LANE

HIGHEST = lax.Precision.HIGHEST


def _params(*sem):
    return pltpu.CompilerParams(dimension_semantics=sem, vmem_limit_bytes=VMEM_LIMIT)


def _sigmoid(x):
    return 1.0 / (1.0 + jnp.exp(-x))


def _silu(x):
    return x * _sigmoid(x)


def _softplus(x):
    return jnp.maximum(x, 0.0) + jnp.log(1.0 + jnp.exp(-jnp.abs(x)))


def _dot(a, b, precision=None):
    return jnp.dot(a, b, preferred_element_type=F32, precision=precision)


def _dot_nt(a, b, precision=None):
    return lax.dot_general(a, b, (((1,), (1,)), ((), ())), preferred_element_type=F32, precision=precision)


def _dot_tn(a, b, precision=None):
    return lax.dot_general(a, b, (((0,), (0,)), ((), ())), preferred_element_type=F32, precision=precision)


def _rms_body(x_ref, g_ref, o_ref):
    x = x_ref[...]
    ms = jnp.mean(x * x, axis=-1, keepdims=True)
    o_ref[...] = (x * lax.rsqrt(ms + RMS_EPS) * g_ref[...]).astype(o_ref.dtype)


def _rmsnorm(x, gain, out_dtype, tm):
    m, d = x.shape
    return pl.pallas_call(
        _rms_body,
        out_shape=jax.ShapeDtypeStruct((m, d), out_dtype),
        grid=(m // tm,),
        in_specs=[pl.BlockSpec((tm, d), lambda i: (i, 0)), pl.BlockSpec((1, d), lambda i: (0, 0))],
        out_specs=pl.BlockSpec((tm, d), lambda i: (i, 0)),
        compiler_params=_params("parallel"),
        name="rmsnorm",
    )(x, gain.reshape(1, d))


def _inproj_body(h_ref, w_ref, wab_ref, wabt_ref, z_ref, ab_ref, abt_ref):
    h = h_ref[...]
    acc = _dot(h, w_ref[...])
    for c in range(acc.shape[1] // LANE):
        z_ref[c] = acc[:, c * LANE:(c + 1) * LANE]

    @pl.when(pl.program_id(1) == 0)
    def _():
        ab_ref[...] = _dot(h, wab_ref[...])
        abt_ref[...] = _dot_nt(wabt_ref[...], h)


def _inproj(h, w_main, w_ab, w_abt, tm, tn):
    m, d = h.shape
    nab = w_ab.shape[1]
    return pl.pallas_call(
        _inproj_body,
        out_shape=(jax.ShapeDtypeStruct((CB_TOTAL, m, LANE), F32),
                   jax.ShapeDtypeStruct((m, nab), F32),
                   jax.ShapeDtypeStruct((nab, m), F32)),
        grid=(m // tm, N_MAIN // tn),
        in_specs=[pl.BlockSpec((tm, d), lambda i, j: (i, 0)),
                  pl.BlockSpec((d, tn), lambda i, j: (0, j)),
                  pl.BlockSpec((d, nab), lambda i, j: (0, 0)),
                  pl.BlockSpec((nab, d), lambda i, j: (0, 0))],
        out_specs=(pl.BlockSpec((tn // LANE, tm, LANE), lambda i, j: (j, i, 0)),
                   pl.BlockSpec((tm, nab), lambda i, j: (i, 0)),
                   pl.BlockSpec((nab, tm), lambda i, j: (0, i))),
        compiler_params=_params("parallel", "arbitrary"),
        name="inproj",
    )(h, w_main, w_ab, w_abt)


def _outproj_body(ma_ref, mb_ref, mc_ref, w_ref, x_ref, o_ref, lhs_ref):
    @pl.when(pl.program_id(1) == 0)
    def _():
        c0 = 0
        for ref in (ma_ref, mb_ref, mc_ref):
            for c in range(ref.shape[0]):
                lhs_ref[:, (c0 + c) * LANE:(c0 + c + 1) * LANE] = ref[c].astype(BF16)
            c0 += ref.shape[0]

    o_ref[...] = x_ref[...] + _dot(lhs_ref[...], w_ref[...])


def _outproj(mix_a, mix_b, mix_c, w_out, x, tm, tn):
    m, d = x.shape
    k = w_out.shape[0]

    def mix_spec(a):
        return pl.BlockSpec((a.shape[0], tm, LANE), lambda i, j: (0, i, 0))

    return pl.pallas_call(
        _outproj_body,
        out_shape=jax.ShapeDtypeStruct((m, d), F32),
        grid=(m // tm, d // tn),
        in_specs=[mix_spec(mix_a), mix_spec(mix_b), mix_spec(mix_c),
                  pl.BlockSpec((k, tn), lambda i, j: (0, j)),
                  pl.BlockSpec((tm, tn), lambda i, j: (i, j))],
        out_specs=pl.BlockSpec((tm, tn), lambda i, j: (i, j)),
        scratch_shapes=[pltpu.VMEM((tm, k), BF16)],
        compiler_params=_params("parallel", "arbitrary"),
        name="outproj",
    )(mix_a, mix_b, mix_c, w_out, x)


def _attn_prompt_body(q_ref, k_ref, v_ref, g_ref, o_ref, acc_ref, m_ref, l_ref, *, seq):
    n = A_BLOCK
    scale = LANE ** -0.5
    qi = lax.broadcasted_iota(jnp.int32, (n, n), 0)
    kj = lax.broadcasted_iota(jnp.int32, (n, n), 1)
    cur_mask = kj <= qi
    prev_mask = kj >= qi

    for pi, (window, d) in enumerate(A_PATTERNS):
        nblk = seq // window
        shift = nblk.bit_length() - 1

        def rows(t0, d=d):
            if d == 1:
                return pl.ds(pl.multiple_of(t0, n), n)
            return pl.ds(t0, n, stride=d)

        def block(idx, carry, pi=pi, window=window, nblk=nblk, shift=shift, rows=rows):
            r = idx >> shift
            blk = idx & (nblk - 1)
            t0 = blk * window + r
            q = q_ref[0, rows(t0), :].astype(BF16)
            kc = k_ref[0, rows(t0), :].astype(BF16)
            vc = v_ref[0, rows(t0), :].astype(BF16)
            s_c = jnp.where(cur_mask, _dot_nt(q, kc) * scale, -jnp.inf)
            m = jnp.max(s_c, axis=1, keepdims=True)
            if nblk > 1:
                tp = jnp.maximum(blk - 1, 0) * window + r
                kp = k_ref[0, rows(tp), :].astype(BF16)
                vp = v_ref[0, rows(tp), :].astype(BF16)
                s_p = jnp.where(prev_mask & (blk > 0), _dot_nt(q, kp) * scale, -jnp.inf)
                m = jnp.maximum(m, jnp.max(s_p, axis=1, keepdims=True))
            p_c = jnp.exp(s_c - m)
            l = jnp.sum(p_c, axis=1, keepdims=True)
            o = _dot(p_c.astype(BF16), vc)
            if nblk > 1:
                p_p = jnp.exp(s_p - m)
                l = l + jnp.sum(p_p, axis=1, keepdims=True)
                o = o + _dot(p_p.astype(BF16), vp)
            mb = jnp.broadcast_to(m, (n, LANE))
            lb = jnp.broadcast_to(l, (n, LANE))
            if pi == 0:
                acc_ref[rows(t0), :] = o
                m_ref[rows(t0), :] = mb
                l_ref[rows(t0), :] = lb
            else:
                m_old = m_ref[rows(t0), :]
                m_new = jnp.maximum(m_old, mb)
                w_old = jnp.exp(m_old - m_new)
                w_cur = jnp.exp(mb - m_new)
                acc_ref[rows(t0), :] = acc_ref[rows(t0), :] * w_old + o * w_cur
                l_ref[rows(t0), :] = l_ref[rows(t0), :] * w_old + lb * w_cur
                m_ref[rows(t0), :] = m_new
            return carry

        lax.fori_loop(0, d * nblk, block, 0)

    rows_out = 256

    def finish(i, carry):
        sl = pl.ds(pl.multiple_of(i * rows_out, rows_out), rows_out)
        out = acc_ref[sl, :] / l_ref[sl, :] * _silu(g_ref[0, sl, :])
        o_ref[0, sl, :] = out.astype(o_ref.dtype)
        return carry

    lax.fori_loop(0, seq // rows_out, finish, 0)


def _attn_prompt(z3, batch, seq):
    m = z3.shape[1]

    def spec(cb0):
        return pl.BlockSpec((1, seq, LANE), lambda b, h: (cb0 + h, b, 0))

    return pl.pallas_call(
        functools.partial(_attn_prompt_body, seq=seq),
        out_shape=jax.ShapeDtypeStruct((A_HEADS, m, LANE), BF16),
        grid=(batch, A_HEADS),
        in_specs=[spec(CB_QA), spec(CB_KA), spec(CB_VA), spec(CB_GA)],
        out_specs=pl.BlockSpec((1, seq, LANE), lambda b, h: (h, b, 0)),
        scratch_shapes=[pltpu.VMEM((seq, LANE), F32)] * 3,
        compiler_params=_params("parallel", "parallel"),
        name="attn_prompt",
    )(z3, z3, z3, z3)


def _pool_prompt_body(u_ref, g_ref, w_ref, sc_ref, o_ref, *, seq):
    grp = pl.program_id(1)
    row = lax.broadcasted_iota(jnp.int32, (seq, 1), 0)
    for gi, width in enumerate(B_POOLS):
        @pl.when(grp == gi)
        def _(width=width):
            u = jnp.concatenate([u_ref[0], u_ref[1]], axis=1)
            s = u
            step = 1
            while step < width:
                s = s + jnp.where(row >= step, pltpu.roll(s, step, axis=0), 0.0)
                step *= 2
            cnt = jnp.minimum(row + 1, width).astype(F32)
            y = (s / cnt - u).astype(BF16)
            out = _dot(y, w_ref[0]) * sc_ref[0]
            gate = jnp.concatenate([g_ref[0], g_ref[1]], axis=1)
            out = (out * _silu(gate)).astype(o_ref.dtype)
            o_ref[0] = out[:, :LANE]
            o_ref[1] = out[:, LANE:]


def _pool_prompt(z3, pool_w, pool_scale, batch, seq):
    m = z3.shape[1]
    ngrp = len(B_POOLS)
    gw = pool_w.shape[1]
    cpg = gw // LANE

    def spec(cb0):
        return pl.BlockSpec((cpg, seq, LANE), lambda b, g: (cb0 // cpg + g, b, 0))

    return pl.pallas_call(
        functools.partial(_pool_prompt_body, seq=seq),
        out_shape=jax.ShapeDtypeStruct((ngrp * cpg, m, LANE), BF16),
        grid=(batch, ngrp),
        in_specs=[spec(CB_UB), spec(CB_GB),
                  pl.BlockSpec((1, gw, gw), lambda b, g: (g, 0, 0)),
                  pl.BlockSpec((1, 1, gw), lambda b, g: (g, 0, 0))],
        out_specs=pl.BlockSpec((cpg, seq, LANE), lambda b, g: (g, b, 0)),
        compiler_params=_params("parallel", "parallel"),
        name="pool_prompt",
    )(z3, z3, pool_w.astype(BF16), pool_scale.reshape(ngrp, 1, gw))


def _unit_lower_inverse(a):
    c = a.shape[0]
    eye = (lax.broadcasted_iota(jnp.int32, (c, c), 0) == lax.broadcasted_iota(jnp.int32, (c, c), 1)).astype(F32)
    p = -a
    x = eye + p
    k = 2
    while k < c:
        p = _dot(p, p, HIGHEST)
        x = x + _dot(x, p, HIGHEST)
        k *= 2
    return x


def _delta_prompt_body(q_ref, k_ref, v_ref, gate_ref, ab_ref, abt_ref, cw_ref, alog_ref, dtb_ref,
                       alogc_ref, dtbc_ref, nw_ref, o_ref, s_ref, tail_ref, gct_ref, *, group):
    heads = C_HEADS
    c = C_CHUNK
    tile = q_ref.shape[1]
    nchunk = tile // c
    hd = LANE

    @pl.when(pl.program_id(1) == 0)
    def _():
        s_ref[...] = jnp.zeros_like(s_ref)
        tail_ref[...] = jnp.zeros_like(tail_ref)

    g = -jnp.exp(alog_ref[...]) * _softplus(ab_ref[:, 0:heads] + dtb_ref[...])
    beta = _sigmoid(ab_ref[:, heads:2 * heads])
    g_t = -jnp.exp(alogc_ref[...]) * _softplus(abt_ref[0:heads, :] + dtbc_ref[...])

    ri = lax.broadcasted_iota(jnp.int32, (c, c), 0)
    ci = lax.broadcasted_iota(jnp.int32, (c, c), 1)
    causal = ri >= ci
    strict = ri > ci
    tri = causal.astype(F32)
    gcs, betas = [], []
    for ch in range(nchunk):
        sl = slice(ch * c, (ch + 1) * c)
        gcs.append(_dot(tri, g[sl], HIGHEST))
        betas.append(beta[sl])
        gct_ref[ch] = _dot_nt(g_t[:, sl], tri, HIGHEST)
    lane_h = lax.broadcasted_iota(jnp.int32, (1, heads), 1)

    def conv(x, tl, w):
        xe = jnp.concatenate([tl, x], axis=0)
        y = xe[5:5 + tile] * w[0:1]
        y = y + xe[6:6 + tile] * w[1:2]
        y = y + xe[7:7 + tile] * w[2:3]
        y = y + x * w[3:4]
        return _silu(y)

    def l2n(x):
        return x * lax.rsqrt(jnp.sum(x * x, axis=-1, keepdims=True) + RMS_EPS)

    def group_body(hg, carry):
        for i in range(group):
            h = hg * group + i
            onehot = lane_h == h
            xq, xk, xv = q_ref[h], k_ref[h], v_ref[h]
            q = l2n(conv(xq, tail_ref[h], cw_ref[h])) * (hd ** -0.5)
            k = l2n(conv(xk, tail_ref[heads + h], cw_ref[heads + h]))
            v = conv(xv, tail_ref[2 * heads + h], cw_ref[2 * heads + h])
            tail_ref[h] = xq[tile - 8:tile]
            tail_ref[heads + h] = xk[tile - 8:tile]
            tail_ref[2 * heads + h] = xv[tile - 8:tile]
            state = s_ref[0, h]
            for ch in range(nchunk):
                sl = slice(ch * c, (ch + 1) * c)
                qh, kh, vh = q[sl], k[sl], v[sl]
                gcol = jnp.sum(jnp.where(onehot, gcs[ch], 0.0), axis=1, keepdims=True)
                bcol = jnp.sum(jnp.where(onehot, betas[ch], 0.0), axis=1, keepdims=True)
                grow = gct_ref[ch, pl.ds(h, 1), :]
                dec = jnp.exp(jnp.where(causal, gcol - grow, -jnp.inf))
                kb = kh * bcol
                a_mat = jnp.where(strict, _dot_nt(kb, kh, HIGHEST) * dec, 0.0)
                qk = _dot_nt(qh, kh, HIGHEST) * dec
                egc = jnp.exp(gcol)
                rhs = jnp.concatenate([vh * bcol, kb * egc], axis=1)
                sol = _dot(_unit_lower_inverse(a_mat), rhs, HIGHEST)
                u, w = sol[:, :hd], sol[:, hd:]
                v_new = u - _dot(w, state, HIGHEST)
                o = _dot(qh * egc, state, HIGHEST) + _dot(qk, v_new, HIGHEST)
                glast = gcol[c - 1:c, :]
                kd = kh * jnp.exp(glast - gcol)
                state = state * jnp.exp(glast) + _dot_tn(kd, v_new, HIGHEST)
                on = o * lax.rsqrt(jnp.mean(o * o, axis=-1, keepdims=True) + RMS_EPS) * nw_ref[...]
                o_ref[h, sl, :] = (on * _silu(gate_ref[h, sl, :])).astype(o_ref.dtype)
            s_ref[0, h] = state
        return carry

    lax.fori_loop(0, heads // group, group_body, 0)


def _delta_prompt(z3, ab, abt, conv_w3, a_log, dt_bias, norm_w, batch, seq, tile=128, group=4):
    m = z3.shape[1]
    heads = C_HEADS
    nt = seq // tile

    def spec(cb0):
        return pl.BlockSpec((heads, tile, LANE), lambda b, t: (cb0 // heads, b * nt + t, 0))

    def full(a):
        return pl.BlockSpec(a.shape, lambda b, t: (0,) * a.ndim)

    small = (conv_w3, a_log.reshape(1, heads), dt_bias.reshape(1, heads),
             a_log.reshape(heads, 1), dt_bias.reshape(heads, 1), norm_w.reshape(1, LANE))
    return pl.pallas_call(
        functools.partial(_delta_prompt_body, group=group),
        out_shape=(jax.ShapeDtypeStruct((heads, m, LANE), BF16),
                   jax.ShapeDtypeStruct((batch, heads, LANE, LANE), F32)),
        grid=(batch, nt),
        in_specs=[spec(CB_QC), spec(CB_KC), spec(CB_VC), spec(CB_GC),
                  pl.BlockSpec((tile, 2 * heads), lambda b, t: (b * nt + t, 0)),
                  pl.BlockSpec((2 * heads, tile), lambda b, t: (0, b * nt + t))] + [full(a) for a in small],
        out_specs=(pl.BlockSpec((heads, tile, LANE), lambda b, t: (0, b * nt + t, 0)),
                   pl.BlockSpec((1, heads, LANE, LANE), lambda b, t: (b, 0, 0, 0))),
        scratch_shapes=[pltpu.VMEM((3 * heads, 8, LANE), F32),
                        pltpu.VMEM((tile // C_CHUNK, heads, C_CHUNK), F32)],
        compiler_params=_params("parallel", "arbitrary"),
        name="delta_prompt",
    )(z3, z3, z3, z3, ab, abt, *small)


def _attn_sample_body(z_ref, kc_ref, vc_ref, o_ref, *, nbuf):
    h = pl.program_id(0)
    b = pl.program_id(1)

    @pl.when(b == 0)
    def _():
        o_ref[...] = jnp.zeros_like(o_ref)

    row = pl.ds(b, 1)
    q = z_ref[CB_QA + h, row, :].astype(BF16)
    kn = z_ref[CB_KA + h, row, :].astype(BF16)
    vn = z_ref[CB_VA + h, row, :].astype(BF16)
    gate = z_ref[CB_GA + h, row, :]
    scale = LANE ** -0.5
    kmat = kc_ref[0].astype(BF16)
    vmat = vc_ref[0].astype(BF16)
    s = _dot_nt(jnp.broadcast_to(q, (MXU_ROWS, LANE)), kmat)[0:1] * scale
    s_new = jnp.sum(q.astype(F32) * kn.astype(F32), axis=1, keepdims=True) * scale
    dist = nbuf - lax.broadcasted_iota(jnp.int32, (1, nbuf), 1)
    ps, pns, ms, ls = [], [], [], []
    for window, d in A_PATTERNS:
        mask = ((dist & (d - 1)) == 0) & (dist <= window)
        sm = jnp.where(mask, s, -jnp.inf)
        m = jnp.maximum(jnp.max(sm, axis=1, keepdims=True), s_new)
        p = jnp.exp(sm - m)
        pn = jnp.exp(s_new - m)
        ps.append(p)
        pns.append(pn)
        ms.append(m)
        ls.append(jnp.sum(p, axis=1, keepdims=True) + pn)
    npat = len(A_PATTERNS)
    pmat = jnp.concatenate(ps + [jnp.zeros((MXU_ROWS - npat, nbuf), F32)], axis=0).astype(BF16)
    omat = _dot(pmat, vmat)
    m_all = functools.reduce(jnp.maximum, ms)
    num = jnp.zeros((1, LANE), F32)
    den = jnp.zeros((1, 1), F32)
    for i in range(npat):
        o_i = omat[i:i + 1] + pns[i].astype(BF16).astype(F32) * vn.astype(F32)
        wgt = jnp.exp(ms[i] - m_all)
        num = num + o_i * wgt
        den = den + ls[i] * wgt
    o_ref[0, row, :] = num / den * _silu(gate)


def _attn_sample(z3s, cache_k, cache_v):
    nb, nbuf = cache_k.shape[0], cache_k.shape[1]
    rows = z3s.shape[1]
    kc = cache_k.reshape(nb, nbuf, A_HEADS * LANE)
    vc = cache_v.reshape(nb, nbuf, A_HEADS * LANE)
    cspec = pl.BlockSpec((1, nbuf, LANE), lambda h, b: (b, 0, h))
    return pl.pallas_call(
        functools.partial(_attn_sample_body, nbuf=nbuf),
        out_shape=jax.ShapeDtypeStruct((A_HEADS, rows, LANE), F32),
        grid=(A_HEADS, nb),
        in_specs=[pl.BlockSpec(z3s.shape, lambda h, b: (0, 0, 0)), cspec, cspec],
        out_specs=pl.BlockSpec((1, rows, LANE), lambda h, b: (h, 0, 0)),
        compiler_params=_params("parallel", "arbitrary"),
        name="attn_sample",
    )(z3s, kc, vc)


def _state_sample_body(z_ref, ab_ref, pool_ref, conv_ref, st_ref, pw_ref, psc_ref, cw_ref, alog_ref, dtb_ref,
                       nw_ref, mb_ref, mc_ref, so_ref):
    b = pl.program_id(0)
    heads = C_HEADS
    hd = LANE

    @pl.when(b == 0)
    def _():
        mb_ref[...] = jnp.zeros_like(mb_ref)
        mc_ref[...] = jnp.zeros_like(mc_ref)

    row = pl.ds(b, 1)

    gw = pw_ref.shape[1]
    cpg = gw // LANE
    for gi, width in enumerate(B_POOLS):
        ys = []
        for j in range(cpg):
            cb = gi * cpg + j
            un = z_ref[CB_UB + cb, row, :]
            prev = pool_ref[0, B_BUF - (width - 1):B_BUF, cb * LANE:(cb + 1) * LANE]
            mean = (jnp.sum(prev, axis=0, keepdims=True) + un) / float(width)
            ys.append(mean - un)
        y = jnp.broadcast_to(jnp.concatenate(ys, axis=1), (MXU_ROWS, gw)).astype(BF16)
        out = _dot(y, pw_ref[gi])[0:1] * psc_ref[:, gi * gw:(gi + 1) * gw]
        for j in range(cpg):
            cb = gi * cpg + j
            mb_ref[cb, row, :] = out[:, j * LANE:(j + 1) * LANE] * _silu(z_ref[CB_GB + cb, row, :])

    arow = ab_ref[row, :]
    lane_ab = lax.broadcasted_iota(jnp.int32, (1, 2 * heads), 1)
    lane_h = lax.broadcasted_iota(jnp.int32, (1, heads), 1)
    eye = lax.broadcasted_iota(jnp.int32, (hd, hd), 0) == lax.broadcasted_iota(jnp.int32, (hd, hd), 1)

    def pick(vec, lanes, idx):
        return jnp.sum(jnp.where(lanes == idx, vec, 0.0), axis=1, keepdims=True)

    def column(x):
        return jnp.sum(jnp.where(eye, jnp.broadcast_to(x, (hd, hd)), 0.0), axis=1, keepdims=True)

    def conv(j):
        st = conv_ref[0, j]
        w = cw_ref[j]
        y = st[0:1] * w[0:1]
        y = y + st[1:2] * w[1:2]
        y = y + st[2:3] * w[2:3]
        y = y + z_ref[CB_QC + j, row, :] * w[3:4]
        return _silu(y)

    def l2n(x):
        return x * lax.rsqrt(jnp.sum(x * x, axis=-1, keepdims=True) + RMS_EPS)

    def head_body(h, carry):
        q = l2n(conv(h)) * (hd ** -0.5)
        k = l2n(conv(heads + h))
        v = conv(2 * heads + h)
        g = -jnp.exp(pick(alog_ref[...], lane_h, h)) * _softplus(pick(arow, lane_ab, h) + pick(dtb_ref[...], lane_h, h))
        beta = _sigmoid(pick(arow, lane_ab, heads + h))
        state = st_ref[0, h] * jnp.exp(g)
        kcol = column(k)
        delta = (v - jnp.sum(kcol * state, axis=0, keepdims=True)) * beta
        state = state + kcol * delta
        o = jnp.sum(column(q) * state, axis=0, keepdims=True)
        on = o * lax.rsqrt(jnp.mean(o * o, axis=-1, keepdims=True) + RMS_EPS) * nw_ref[...]
        mc_ref[h, row, :] = on * _silu(z_ref[CB_GC + h, row, :])
        so_ref[0, h] = state
        return carry

    lax.fori_loop(0, heads, head_body, 0)


def _state_sample(z3s, ab_s, state_pool, state_conv3, state_delta, pool_w, pool_scale, conv_w3, a_log, dt_bias,
                  norm_w):
    nb = state_pool.shape[0]
    rows = z3s.shape[1]
    heads = C_HEADS
    bw = state_pool.shape[2]

    def full(a):
        return pl.BlockSpec(a.shape, lambda b: (0,) * a.ndim)

    def per_b(a):
        return pl.BlockSpec((1,) + a.shape[1:], lambda b: (b,) + (0,) * (a.ndim - 1))

    small = (pool_w.astype(BF16), pool_scale.reshape(1, bw), conv_w3, a_log.reshape(1, heads),
             dt_bias.reshape(1, heads), norm_w.reshape(1, LANE))
    return pl.pallas_call(
        _state_sample_body,
        out_shape=(jax.ShapeDtypeStruct((bw // LANE, rows, LANE), F32),
                   jax.ShapeDtypeStruct((heads, rows, LANE), F32),
                   jax.ShapeDtypeStruct(state_delta.shape, F32)),
        grid=(nb,),
        in_specs=[full(z3s), full(ab_s), per_b(state_pool), per_b(state_conv3), per_b(state_delta)]
        + [full(a) for a in small],
        out_specs=(pl.BlockSpec((bw // LANE, rows, LANE), lambda b: (0, 0, 0)),
                   pl.BlockSpec((heads, rows, LANE), lambda b: (0, 0, 0)),
                   per_b(state_delta)),
        compiler_params=_params("arbitrary"),
        name="state_sample",
    )(z3s, ab_s, state_pool, state_conv3, state_delta, *small)


def _cols(z4, cb0, ncb, r0, r1):
    blk = z4[cb0:cb0 + ncb, :, r0:r1]
    return jnp.transpose(blk, (1, 2, 0, 3)).reshape(blk.shape[1], r1 - r0, ncb * LANE)


def kernel(x_prompt, x_sample, cache_win_k, cache_win_v, state_pool, state_conv, state_delta, norm_w, w_in,
           conv_w, a_log, dt_bias, delta_norm_w, pool_w, pool_scale, w_out, final_norm_w):
    batch, seq, d_model = x_prompt.shape
    nb = x_sample.shape[0]
    depth = w_in.shape[0]
    heads = C_HEADS
    m = batch * seq
    rows_s = 16
    nbuf = cache_win_k.shape[2]
    pool_cb = state_pool.shape[-1] // LANE

    hp = x_prompt.reshape(m, d_model)
    hs = jnp.zeros((rows_s, d_model), F32).at[:nb].set(x_sample.reshape(nb, d_model))

    outs_p = [[] for _ in range(5)]
    outs_s = [[] for _ in range(5)]
    for l in range(depth):
        w_main = w_in[l, :, :N_MAIN].astype(BF16)
        w_ab = w_in[l, :, N_MAIN:].astype(BF16)
        w_abt = w_ab.T
        w_o = w_out[l].astype(BF16)
        conv_w3 = jnp.transpose(conv_w[l].reshape(C_CONV, 3 * heads, LANE), (1, 0, 2))

        h = _rmsnorm(hp, norm_w[l], BF16, 256)
        z3, ab, abt = _inproj(h, w_main, w_ab, w_abt, 1024, 512)
        mix_a = _attn_prompt(z3, batch, seq)
        mix_b = _pool_prompt(z3, pool_w[l], pool_scale[l], batch, seq)
        mix_c, s_new = _delta_prompt(z3, ab, abt, conv_w3, a_log[l], dt_bias[l], delta_norm_w[l], batch, seq)
        hp = _outproj(mix_a, mix_b, mix_c, w_o, hp, 1024, 512)

        z4 = z3.reshape(CB_TOTAL, batch, seq, LANE)
        keep_p = min(seq, A_PATTERNS[-1][0])
        outs_p[0].append(_cols(z4, CB_KA, A_HEADS, seq - keep_p, seq).reshape(batch, keep_p, A_HEADS, LANE))
        outs_p[1].append(_cols(z4, CB_VA, A_HEADS, seq - keep_p, seq).reshape(batch, keep_p, A_HEADS, LANE))
        outs_p[2].append(_cols(z4, CB_UB, pool_cb, seq - B_BUF, seq))
        outs_p[3].append(_cols(z4, CB_QC, 3 * heads, seq - (C_CONV - 1), seq))
        outs_p[4].append(s_new)

        h_s = _rmsnorm(hs, norm_w[l], BF16, rows_s)
        z3s, ab_s, _ = _inproj(h_s, w_main, w_ab, w_abt, rows_s, 512)
        mix_as = _attn_sample(z3s, cache_win_k[l], cache_win_v[l])
        conv3 = jnp.transpose(state_conv[l].reshape(nb, C_CONV - 1, 3 * heads, LANE), (0, 2, 1, 3))
        mix_bs, mix_cs, st_new = _state_sample(z3s, ab_s, state_pool[l], conv3, state_delta[l], pool_w[l],
                                               pool_scale[l], conv_w3, a_log[l], dt_bias[l], delta_norm_w[l])
        hs = _outproj(mix_as, mix_bs, mix_cs, w_o, hs, rows_s, 512)

        z4s = z3s[:, :nb].reshape(CB_TOTAL, nb, 1, LANE)
        k_new = _cols(z4s, CB_KA, A_HEADS, 0, 1).reshape(nb, 1, A_HEADS, LANE)
        v_new = _cols(z4s, CB_VA, A_HEADS, 0, 1).reshape(nb, 1, A_HEADS, LANE)
        keep = min(nbuf + 1, A_PATTERNS[-1][0])
        outs_s[0].append(jnp.concatenate([cache_win_k[l], k_new], axis=1)[:, nbuf + 1 - keep:])
        outs_s[1].append(jnp.concatenate([cache_win_v[l], v_new], axis=1)[:, nbuf + 1 - keep:])
        outs_s[2].append(jnp.concatenate([state_pool[l], _cols(z4s, CB_UB, pool_cb, 0, 1)], axis=1)[:, 1:])
        outs_s[3].append(jnp.concatenate([state_conv[l], _cols(z4s, CB_QC, 3 * heads, 0, 1)], axis=1)[:, 1:])
        outs_s[4].append(st_new)

    y_prompt = _rmsnorm(hp, final_norm_w, F32, 256).reshape(batch, seq, d_model)
    y_sample = _rmsnorm(hs, final_norm_w, F32, rows_s)[:nb].reshape(nb, 1, d_model)
    stack = lambda xs: jnp.stack(xs, axis=0)
    return (y_prompt, y_sample) + tuple(stack(o) for o in outs_p) + tuple(stack(o) for o in outs_s)
```

```python
import functools

import jax
import jax.numpy as jnp
from jax import lax
from jax.experimental import pallas as pl
from jax.experimental.pallas import tpu as pltpu

F32 = jnp.float32
BF16 = jnp.bfloat16

LANE = 128
MXU_ROWS = 16
VMEM_LIMIT = 56 * 2**20
RMS_EPS = 1e-6

A_HEADS = 8
A_PATTERNS = ((128, 1), (512, 4), (2048, 16))
A_BLOCK = 128
A_UNROLL = 4
B_POOLS = (2, 4, 8, 16)
B_BUF = 15
C_HEADS = 16
C_CONV = 4
CONV_PAD = 8

CB_QA, CB_KA, CB_VA, CB_GA = 0, 8, 16, 24
CB_UB, CB_GB = 32, 40
CB_QC, CB_KC, CB_VC, CB_GC = 48, 64, 80, 96
CB_TOTAL = 112
N_MAIN = CB_TOTAL * LANE

HIGHEST = lax.Precision.HIGHEST


def _params(*sem):
    return pltpu.CompilerParams(dimension_semantics=sem, vmem_limit_bytes=VMEM_LIMIT)


def _sigmoid(x):
    return 1.0 / (1.0 + jnp.exp(-x))


def _silu(x):
    return x * _sigmoid(x)


def _softplus(x):
    return jnp.maximum(x, 0.0) + jnp.log(1.0 + jnp.exp(-jnp.abs(x)))


def _dot(a, b, precision=None):
    return jnp.dot(a, b, preferred_element_type=F32, precision=precision)


def _dot_nt(a, b, precision=None):
    return lax.dot_general(a, b, (((1,), (1,)), ((), ())), preferred_element_type=F32, precision=precision)


def _dot_tn(a, b, precision=None):
    return lax.dot_general(a, b, (((0,), (0,)), ((), ())), preferred_element_type=F32, precision=precision)


def _rms_body(x_ref, g_ref, o_ref):
    x = x_ref[...]
    ms = jnp.mean(x * x, axis=-1, keepdims=True)
    o_ref[...] = (x * lax.rsqrt(ms + RMS_EPS) * g_ref[...]).astype(o_ref.dtype)


def _rmsnorm(x, gain, out_dtype, tm):
    m, d = x.shape
    return pl.pallas_call(
        _rms_body,
        out_shape=jax.ShapeDtypeStruct((m, d), out_dtype),
        grid=(m // tm,),
        in_specs=[pl.BlockSpec((tm, d), lambda i: (i, 0)), pl.BlockSpec((1, d), lambda i: (0, 0))],
        out_specs=pl.BlockSpec((tm, d), lambda i: (i, 0)),
        compiler_params=_params("parallel"),
        name="rmsnorm",
    )(x, gain.reshape(1, d))


def _inproj_body(h_ref, w_ref, wab_ref, wabt_ref, z_ref, ab_ref, abt_ref, k_ref, v_ref, *, k_tiles, v_tiles):
    j = pl.program_id(1)
    h = h_ref[...]
    acc = _dot(h, w_ref[...])
    for c in range(acc.shape[1] // LANE):
        z_ref[c] = acc[:, c * LANE:(c + 1) * LANE]

    @pl.when(j == 0)
    def _():
        ab_ref[...] = _dot(h, wab_ref[...])
        abt_ref[...] = _dot_nt(wabt_ref[...], h)

    @pl.when((j >= k_tiles[0]) & (j < k_tiles[1]))
    def _():
        k_ref[...] = acc

    @pl.when((j >= v_tiles[0]) & (j < v_tiles[1]))
    def _():
        v_ref[...] = acc


def _inproj(h, w_main, w_ab, w_abt, tm, tn):
    m, d = h.shape
    nab = w_ab.shape[1]
    a_width = A_HEADS * LANE
    k_tiles = (CB_KA * LANE // tn, CB_VA * LANE // tn)
    v_tiles = (CB_VA * LANE // tn, CB_GA * LANE // tn)

    def natural(tiles):
        return pl.BlockSpec((tm, tn), lambda i, j: (i, jnp.clip(j - tiles[0], 0, tiles[1] - tiles[0] - 1)))

    return pl.pallas_call(
        functools.partial(_inproj_body, k_tiles=k_tiles, v_tiles=v_tiles),
        out_shape=(jax.ShapeDtypeStruct((CB_TOTAL, m, LANE), F32),
                   jax.ShapeDtypeStruct((m, nab), F32),
                   jax.ShapeDtypeStruct((nab, m), F32),
                   jax.ShapeDtypeStruct((m, a_width), F32),
                   jax.ShapeDtypeStruct((m, a_width), F32)),
        grid=(m // tm, N_MAIN // tn),
        in_specs=[pl.BlockSpec((tm, d), lambda i, j: (i, 0)),
                  pl.BlockSpec((d, tn), lambda i, j: (0, j)),
                  pl.BlockSpec((d, nab), lambda i, j: (0, 0)),
                  pl.BlockSpec((nab, d), lambda i, j: (0, 0))],
        out_specs=(pl.BlockSpec((tn // LANE, tm, LANE), lambda i, j: (j, i, 0)),
                   pl.BlockSpec((tm, nab), lambda i, j: (i, 0)),
                   pl.BlockSpec((nab, tm), lambda i, j: (0, i)),
                   natural(k_tiles), natural(v_tiles)),
        compiler_params=_params("parallel", "arbitrary"),
        name="inproj",
    )(h, w_main, w_ab, w_abt)


def _outproj_body(ma_ref, mb_ref, mc_ref, w_ref, x_ref, o_ref, lhs_ref):
    @pl.when(pl.program_id(1) == 0)
    def _():
        c0 = 0
        for ref in (ma_ref, mb_ref, mc_ref):
            for c in range(ref.shape[0]):
                lhs_ref[:, (c0 + c) * LANE:(c0 + c + 1) * LANE] = ref[c].astype(BF16)
            c0 += ref.shape[0]

    o_ref[...] = x_ref[...] + _dot(lhs_ref[...], w_ref[...])


def _outproj(mix_a, mix_b, mix_c, w_out, x, tm, tn):
    m, d = x.shape
    k = w_out.shape[0]

    def mix_spec(a):
        return pl.BlockSpec((a.shape[0], tm, LANE), lambda i, j: (0, i, 0))

    return pl.pallas_call(
        _outproj_body,
        out_shape=jax.ShapeDtypeStruct((m, d), F32),
        grid=(m // tm, d // tn),
        in_specs=[mix_spec(mix_a), mix_spec(mix_b), mix_spec(mix_c),
                  pl.BlockSpec((k, tn), lambda i, j: (0, j)),
                  pl.BlockSpec((tm, tn), lambda i, j: (i, j))],
        out_specs=pl.BlockSpec((tm, tn), lambda i, j: (i, j)),
        scratch_shapes=[pltpu.VMEM((tm, k), BF16)],
        compiler_params=_params("parallel", "arbitrary"),
        name="outproj",
    )(mix_a, mix_b, mix_c, w_out, x)


def _attn_prompt_body(q_ref, k_ref, v_ref, g_ref, o_ref, acc_ref, m_ref, l_ref, *, seq):
    n = A_BLOCK
    scale = LANE ** -0.5
    qi = lax.broadcasted_iota(jnp.int32, (n, n), 0)
    kj = lax.broadcasted_iota(jnp.int32, (n, n), 1)
    cur_mask = kj <= qi
    prev_mask = kj >= qi

    for pi, (window, d) in enumerate(A_PATTERNS):
        nblk = seq // window
        shift = nblk.bit_length() - 1

        def rows(t0, d=d):
            if d == 1:
                return pl.ds(pl.multiple_of(t0, n), n)
            return pl.ds(t0, n, stride=d)

        def blocks(it, carry, pi=pi, window=window, nblk=nblk, shift=shift, rows=rows):
            has_prev = nblk > 1
            t0s, scores = [], []
            for u in range(A_UNROLL):
                idx = it * A_UNROLL + u
                r = idx >> shift
                blk = idx & (nblk - 1)
                t0 = blk * window + r
                t0s.append(t0)
                q = q_ref[0, rows(t0), :].astype(BF16)
                s_c = _dot_nt(q, k_ref[rows(t0), :].astype(BF16))
                vc = v_ref[rows(t0), :].astype(BF16)
                if has_prev:
                    tp = jnp.maximum(blk - 1, 0) * window + r
                    s_p = _dot_nt(q, k_ref[rows(tp), :].astype(BF16))
                    vp = v_ref[rows(tp), :].astype(BF16)
                    scores.append((s_c, vc, s_p, vp, blk > 0))
                else:
                    scores.append((s_c, vc))
            probs = []
            for sc in scores:
                s_c = jnp.where(cur_mask, sc[0] * scale, -jnp.inf)
                m = jnp.max(s_c, axis=1, keepdims=True)
                if has_prev:
                    s_p = jnp.where(prev_mask & sc[4], sc[2] * scale, -jnp.inf)
                    m = jnp.maximum(m, jnp.max(s_p, axis=1, keepdims=True))
                p_c = jnp.exp(s_c - m)
                l = jnp.sum(p_c, axis=1, keepdims=True)
                if has_prev:
                    p_p = jnp.exp(s_p - m)
                    l = l + jnp.sum(p_p, axis=1, keepdims=True)
                    probs.append((m, l, p_c.astype(BF16), p_p.astype(BF16)))
                else:
                    probs.append((m, l, p_c.astype(BF16)))
            outs = []
            for sc, pr in zip(scores, probs):
                o = _dot(pr[2], sc[1])
                if has_prev:
                    o = o + _dot(pr[3], sc[3])
                outs.append(o)
            for t0, pr, o in zip(t0s, probs, outs):
                mb = jnp.broadcast_to(pr[0], (n, LANE))
                lb = jnp.broadcast_to(pr[1], (n, LANE))
                if pi == 0:
                    acc_ref[rows(t0), :] = o
                    m_ref[rows(t0), :] = mb
                    l_ref[rows(t0), :] = lb
                else:
                    m_old = m_ref[rows(t0), :]
                    m_new = jnp.maximum(m_old, mb)
                    w_old = jnp.exp(m_old - m_new)
                    w_cur = jnp.exp(mb - m_new)
                    acc_ref[rows(t0), :] = acc_ref[rows(t0), :] * w_old + o * w_cur
                    l_ref[rows(t0), :] = l_ref[rows(t0), :] * w_old + lb * w_cur
                    m_ref[rows(t0), :] = m_new
            return carry

        lax.fori_loop(0, d * nblk // A_UNROLL, blocks, 0)

    rows_out = 256

    def finish(i, carry):
        sl = pl.ds(pl.multiple_of(i * rows_out, rows_out), rows_out)
        out = acc_ref[sl, :] / l_ref[sl, :] * _silu(g_ref[0, sl, :])
        o_ref[0, sl, :] = out.astype(o_ref.dtype)
        return carry

    lax.fori_loop(0, seq // rows_out, finish, 0)


def _attn_prompt(z3, k_nat, v_nat, batch, seq):
    m = z3.shape[1]

    def spec(cb0):
        return pl.BlockSpec((1, seq, LANE), lambda b, h: (cb0 + h, b, 0))

    nat = pl.BlockSpec((seq, LANE), lambda b, h: (b, h))
    return pl.pallas_call(
        functools.partial(_attn_prompt_body, seq=seq),
        out_shape=jax.ShapeDtypeStruct((A_HEADS, m, LANE), BF16),
        grid=(batch, A_HEADS),
        in_specs=[spec(CB_QA), nat, nat, spec(CB_GA)],
        out_specs=pl.BlockSpec((1, seq, LANE), lambda b, h: (h, b, 0)),
        scratch_shapes=[pltpu.VMEM((seq, LANE), F32)] * 3,
        compiler_params=_params("parallel", "parallel"),
        name="attn_prompt",
    )(z3, k_nat, v_nat, z3)


def _pool_prompt_body(u_ref, g_ref, w_ref, sc_ref, o_ref, *, seq):
    grp = pl.program_id(1)
    row = lax.broadcasted_iota(jnp.int32, (seq, 1), 0)
    for gi, width in enumerate(B_POOLS):
        @pl.when(grp == gi)
        def _(width=width):
            u = jnp.concatenate([u_ref[0], u_ref[1]], axis=1)
            s = u
            step = 1
            while step < width:
                s = s + jnp.where(row >= step, pltpu.roll(s, step, axis=0), 0.0)
                step *= 2
            cnt = jnp.minimum(row + 1, width).astype(F32)
            y = (s / cnt - u).astype(BF16)
            out = _dot(y, w_ref[0]) * sc_ref[0]
            gate = jnp.concatenate([g_ref[0], g_ref[1]], axis=1)
            out = (out * _silu(gate)).astype(o_ref.dtype)
            o_ref[0] = out[:, :LANE]
            o_ref[1] = out[:, LANE:]


def _pool_prompt(z3, pool_w, pool_scale, batch, seq):
    m = z3.shape[1]
    ngrp = len(B_POOLS)
    gw = pool_w.shape[1]
    cpg = gw // LANE

    def spec(cb0):
        return pl.BlockSpec((cpg, seq, LANE), lambda b, g: (cb0 // cpg + g, b, 0))

    return pl.pallas_call(
        functools.partial(_pool_prompt_body, seq=seq),
        out_shape=jax.ShapeDtypeStruct((ngrp * cpg, m, LANE), BF16),
        grid=(batch, ngrp),
        in_specs=[spec(CB_UB), spec(CB_GB),
                  pl.BlockSpec((1, gw, gw), lambda b, g: (g, 0, 0)),
                  pl.BlockSpec((1, 1, gw), lambda b, g: (g, 0, 0))],
        out_specs=pl.BlockSpec((cpg, seq, LANE), lambda b, g: (g, b, 0)),
        compiler_params=_params("parallel", "parallel"),
        name="pool_prompt",
    )(z3, z3, pool_w.astype(BF16), pool_scale.reshape(ngrp, 1, gw))


def _unit_lower_inverses(mats):
    c = mats[0].shape[0]
    eye = (lax.broadcasted_iota(jnp.int32, (c, c), 0) == lax.broadcasted_iota(jnp.int32, (c, c), 1)).astype(F32)
    xs = [eye - a for a in mats]
    pbs = [(-a).astype(BF16) for a in mats]
    k = 2
    while k < c:
        pbs = [_dot(pb, pb).astype(BF16) for pb in pbs]
        xs = [x + _dot(x.astype(BF16), pb) for x, pb in zip(xs, pbs)]
        k *= 2
    return xs


def _delta_prompt_body(q_ref, k_ref, v_ref, gate_ref, ab_ref, abt_ref, cw_ref, alog_ref, dtb_ref,
                       alogc_ref, dtbc_ref, nw_ref, o_ref, s_ref, xbuf_ref, gct_ref, *, group):
    heads = C_HEADS
    c = q_ref.shape[1]
    tile = c
    hd = LANE

    @pl.when(pl.program_id(1) == 0)
    def _():
        s_ref[...] = jnp.zeros_like(s_ref)
        xbuf_ref[:, 0:CONV_PAD, :] = jnp.zeros((3 * heads, CONV_PAD, hd), F32)

    for j in range(heads):
        xbuf_ref[j, CONV_PAD:CONV_PAD + tile, :] = q_ref[j]
        xbuf_ref[heads + j, CONV_PAD:CONV_PAD + tile, :] = k_ref[j]
        xbuf_ref[2 * heads + j, CONV_PAD:CONV_PAD + tile, :] = v_ref[j]

    g = -jnp.exp(alog_ref[...]) * _softplus(ab_ref[:, 0:heads] + dtb_ref[...])
    beta = _sigmoid(ab_ref[:, heads:2 * heads])
    g_t = -jnp.exp(alogc_ref[...]) * _softplus(abt_ref[0:heads, :] + dtbc_ref[...])

    ri = lax.broadcasted_iota(jnp.int32, (c, c), 0)
    ci = lax.broadcasted_iota(jnp.int32, (c, c), 1)
    causal = ri >= ci
    strict = ri > ci
    tri = causal.astype(F32)
    gcum = _dot(tri, g, HIGHEST)
    gct_ref[...] = _dot_nt(g_t, tri, HIGHEST)
    lane_h = lax.broadcasted_iota(jnp.int32, (1, heads), 1)

    def conv(j):
        w = cw_ref[j]
        first = CONV_PAD - (C_CONV - 1)
        y = xbuf_ref[j, pl.ds(first, tile), :] * w[0:1]
        for i in range(1, C_CONV):
            y = y + xbuf_ref[j, pl.ds(first + i, tile), :] * w[i:i + 1]
        return _silu(y)

    def l2n(x):
        return x * lax.rsqrt(jnp.sum(x * x, axis=-1, keepdims=True) + RMS_EPS)

    def group_body(hg, carry):
        hs = [hg * group + i for i in range(group)]
        states = [s_ref[0, h] for h in hs]
        pre = []
        for h in hs:
            onehot = lane_h == h
            q = l2n(conv(h)) * (hd ** -0.5)
            k = l2n(conv(heads + h))
            v = conv(2 * heads + h)
            gcol = jnp.sum(jnp.where(onehot, gcum, 0.0), axis=1, keepdims=True)
            bcol = jnp.sum(jnp.where(onehot, beta, 0.0), axis=1, keepdims=True)
            grow = gct_ref[pl.ds(h, 1), :]
            dec = jnp.exp(jnp.where(causal, gcol - grow, -jnp.inf))
            kb = k * bcol
            kq = _dot_nt(jnp.concatenate([kb, q], axis=0).astype(BF16), k.astype(BF16))
            a_mat = jnp.where(strict, kq[:c] * dec, 0.0)
            qk = (kq[c:] * dec).astype(BF16)
            egc = jnp.exp(gcol)
            rhs = jnp.concatenate([v * bcol, kb * egc], axis=1).astype(BF16)
            glast = gcol[c - 1:c, :]
            kd = (k * jnp.exp(glast - gcol)).astype(BF16)
            pre.append((a_mat, qk, rhs, (q * egc).astype(BF16), kd, jnp.exp(glast)))
        invs = _unit_lower_inverses([p[0] for p in pre])
        sols = [_dot(t.astype(BF16), p[2]) for t, p in zip(invs, pre)]
        ws_qs = [_dot(jnp.concatenate([sol[:, hd:].astype(BF16), p[3]], axis=0), st.astype(BF16))
                 for sol, p, st in zip(sols, pre, states)]
        v_news = [(sol[:, :hd] - wq[:c]).astype(BF16) for sol, wq in zip(sols, ws_qs)]
        outs = [wq[c:] + _dot(p[1], vn) for wq, p, vn in zip(ws_qs, pre, v_news)]
        new_states = [st * p[5] + _dot_tn(p[4], vn) for st, p, vn in zip(states, pre, v_news)]
        for h, o in zip(hs, outs):
            on = o * lax.rsqrt(jnp.mean(o * o, axis=-1, keepdims=True) + RMS_EPS) * nw_ref[...]
            o_ref[h] = (on * _silu(gate_ref[h])).astype(o_ref.dtype)
        for h, state in zip(hs, new_states):
            s_ref[0, h] = state
        return carry

    lax.fori_loop(0, heads // group, group_body, 0)

    for j in range(3 * heads):
        xbuf_ref[j, 0:CONV_PAD, :] = xbuf_ref[j, tile:tile + CONV_PAD, :]


def _delta_prompt(z3, ab, abt, conv_w3, a_log, dt_bias, norm_w, batch, seq, tile=128, group=8):
    m = z3.shape[1]
    heads = C_HEADS
    nt = seq // tile

    def spec(cb0):
        return pl.BlockSpec((heads, tile, LANE), lambda b, t: (cb0 // heads, b * nt + t, 0))

    def full(a):
        return pl.BlockSpec(a.shape, lambda b, t: (0,) * a.ndim)

    small = (conv_w3, a_log.reshape(1, heads), dt_bias.reshape(1, heads),
             a_log.reshape(heads, 1), dt_bias.reshape(heads, 1), norm_w.reshape(1, LANE))
    return pl.pallas_call(
        functools.partial(_delta_prompt_body, group=group),
        out_shape=(jax.ShapeDtypeStruct((heads, m, LANE), BF16),
                   jax.ShapeDtypeStruct((batch, heads, LANE, LANE), F32)),
        grid=(batch, nt),
        in_specs=[spec(CB_QC), spec(CB_KC), spec(CB_VC), spec(CB_GC),
                  pl.BlockSpec((tile, 2 * heads), lambda b, t: (b * nt + t, 0)),
                  pl.BlockSpec((2 * heads, tile), lambda b, t: (0, b * nt + t))] + [full(a) for a in small],
        out_specs=(pl.BlockSpec((heads, tile, LANE), lambda b, t: (0, b * nt + t, 0)),
                   pl.BlockSpec((1, heads, LANE, LANE), lambda b, t: (b, 0, 0, 0))),
        scratch_shapes=[pltpu.VMEM((3 * heads, CONV_PAD + tile, LANE), F32),
                        pltpu.VMEM((heads, tile), F32)],
        compiler_params=_params("parallel", "arbitrary"),
        name="delta_prompt",
    )(z3, z3, z3, z3, ab, abt, *small)


def _attn_sample_body(z_ref, kc_ref, vc_ref, o_ref, *, nbuf):
    h = pl.program_id(0)
    b = pl.program_id(1)

    @pl.when(b == 0)
    def _():
        o_ref[...] = jnp.zeros_like(o_ref)

    row = pl.ds(b, 1)
    q = z_ref[CB_QA + h, row, :].astype(BF16)
    kn = z_ref[CB_KA + h, row, :].astype(BF16)
    vn = z_ref[CB_VA + h, row, :].astype(BF16)
    gate = z_ref[CB_GA + h, row, :]
    scale = LANE ** -0.5
    kmat = kc_ref[0].astype(BF16)
    vmat = vc_ref[0].astype(BF16)
    s = _dot_nt(jnp.broadcast_to(q, (MXU_ROWS, LANE)), kmat)[0:1] * scale
    s_new = jnp.sum(q.astype(F32) * kn.astype(F32), axis=1, keepdims=True) * scale
    dist = nbuf - lax.broadcasted_iota(jnp.int32, (1, nbuf), 1)
    ps, pns, ms, ls = [], [], [], []
    for window, d in A_PATTERNS:
        mask = ((dist & (d - 1)) == 0) & (dist <= window)
        sm = jnp.where(mask, s, -jnp.inf)
        m = jnp.maximum(jnp.max(sm, axis=1, keepdims=True), s_new)
        p = jnp.exp(sm - m)
        pn = jnp.exp(s_new - m)
        ps.append(p)
        pns.append(pn)
        ms.append(m)
        ls.append(jnp.sum(p, axis=1, keepdims=True) + pn)
    npat = len(A_PATTERNS)
    pmat = jnp.concatenate(ps + [jnp.zeros((MXU_ROWS - npat, nbuf), F32)], axis=0).astype(BF16)
    omat = _dot(pmat, vmat)
    m_all = functools.reduce(jnp.maximum, ms)
    num = jnp.zeros((1, LANE), F32)
    den = jnp.zeros((1, 1), F32)
    for i in range(npat):
        o_i = omat[i:i + 1] + pns[i].astype(BF16).astype(F32) * vn.astype(F32)
        wgt = jnp.exp(ms[i] - m_all)
        num = num + o_i * wgt
        den = den + ls[i] * wgt
    o_ref[0, row, :] = num / den * _silu(gate)


def _attn_sample(z3s, cache_k, cache_v):
    nb, nbuf = cache_k.shape[0], cache_k.shape[1]
    rows = z3s.shape[1]
    kc = cache_k.reshape(nb, nbuf, A_HEADS * LANE)
    vc = cache_v.reshape(nb, nbuf, A_HEADS * LANE)
    cspec = pl.BlockSpec((1, nbuf, LANE), lambda h, b: (b, 0, h))
    return pl.pallas_call(
        functools.partial(_attn_sample_body, nbuf=nbuf),
        out_shape=jax.ShapeDtypeStruct((A_HEADS, rows, LANE), F32),
        grid=(A_HEADS, nb),
        in_specs=[pl.BlockSpec(z3s.shape, lambda h, b: (0, 0, 0)), cspec, cspec],
        out_specs=pl.BlockSpec((1, rows, LANE), lambda h, b: (h, 0, 0)),
        compiler_params=_params("parallel", "arbitrary"),
        name="attn_sample",
    )(z3s, kc, vc)


def _state_sample_body(z_ref, ab_ref, pool_ref, conv_ref, st_ref, pw_ref, psc_ref, cw_ref, alog_ref, dtb_ref,
                       nw_ref, mb_ref, mc_ref, so_ref):
    b = pl.program_id(0)
    heads = C_HEADS
    hd = LANE

    @pl.when(b == 0)
    def _():
        mb_ref[...] = jnp.zeros_like(mb_ref)
        mc_ref[...] = jnp.zeros_like(mc_ref)

    row = pl.ds(b, 1)

    gw = pw_ref.shape[1]
    cpg = gw // LANE
    for gi, width in enumerate(B_POOLS):
        ys = []
        for j in range(cpg):
            cb = gi * cpg + j
            un = z_ref[CB_UB + cb, row, :]
            prev = pool_ref[0, B_BUF - (width - 1):B_BUF, cb * LANE:(cb + 1) * LANE]
            mean = (jnp.sum(prev, axis=0, keepdims=True) + un) / float(width)
            ys.append(mean - un)
        y = jnp.broadcast_to(jnp.concatenate(ys, axis=1), (MXU_ROWS, gw)).astype(BF16)
        out = _dot(y, pw_ref[gi])[0:1] * psc_ref[:, gi * gw:(gi + 1) * gw]
        for j in range(cpg):
            cb = gi * cpg + j
            mb_ref[cb, row, :] = out[:, j * LANE:(j + 1) * LANE] * _silu(z_ref[CB_GB + cb, row, :])

    arow = ab_ref[row, :]
    lane_ab = lax.broadcasted_iota(jnp.int32, (1, 2 * heads), 1)
    lane_h = lax.broadcasted_iota(jnp.int32, (1, heads), 1)
    eye = lax.broadcasted_iota(jnp.int32, (hd, hd), 0) == lax.broadcasted_iota(jnp.int32, (hd, hd), 1)

    def pick(vec, lanes, idx):
        return jnp.sum(jnp.where(lanes == idx, vec, 0.0), axis=1, keepdims=True)

    def column(x):
        return jnp.sum(jnp.where(eye, jnp.broadcast_to(x, (hd, hd)), 0.0), axis=1, keepdims=True)

    def conv(j):
        st = conv_ref[0, j]
        w = cw_ref[j]
        y = st[0:1] * w[0:1]
        y = y + st[1:2] * w[1:2]
        y = y + st[2:3] * w[2:3]
        y = y + z_ref[CB_QC + j, row, :] * w[3:4]
        return _silu(y)

    def l2n(x):
        return x * lax.rsqrt(jnp.sum(x * x, axis=-1, keepdims=True) + RMS_EPS)

    def head_body(h, carry):
        q = l2n(conv(h)) * (hd ** -0.5)
        k = l2n(conv(heads + h))
        v = conv(2 * heads + h)
        g = -jnp.exp(pick(alog_ref[...], lane_h, h)) * _softplus(pick(arow, lane_ab, h) + pick(dtb_ref[...], lane_h, h))
        beta = _sigmoid(pick(arow, lane_ab, heads + h))
        state = st_ref[0, h] * jnp.exp(g)
        kcol = column(k)
        delta = (v - jnp.sum(kcol * state, axis=0, keepdims=True)) * beta
        state = state + kcol * delta
        o = jnp.sum(column(q) * state, axis=0, keepdims=True)
        on = o * lax.rsqrt(jnp.mean(o * o, axis=-1, keepdims=True) + RMS_EPS) * nw_ref[...]
        mc_ref[h, row, :] = on * _silu(z_ref[CB_GC + h, row, :])
        so_ref[0, h] = state
        return carry

    lax.fori_loop(0, heads, head_body, 0)


def _state_sample(z3s, ab_s, state_pool, state_conv3, state_delta, pool_w, pool_scale, conv_w3, a_log, dt_bias,
                  norm_w):
    nb = state_pool.shape[0]
    rows = z3s.shape[1]
    heads = C_HEADS
    bw = state_pool.shape[2]

    def full(a):
        return pl.BlockSpec(a.shape, lambda b: (0,) * a.ndim)

    def per_b(a):
        return pl.BlockSpec((1,) + a.shape[1:], lambda b: (b,) + (0,) * (a.ndim - 1))

    small = (pool_w.astype(BF16), pool_scale.reshape(1, bw), conv_w3, a_log.reshape(1, heads),
             dt_bias.reshape(1, heads), norm_w.reshape(1, LANE))
    return pl.pallas_call(
        _state_sample_body,
        out_shape=(jax.ShapeDtypeStruct((bw // LANE, rows, LANE), F32),
                   jax.ShapeDtypeStruct((heads, rows, LANE), F32),
                   jax.ShapeDtypeStruct(state_delta.shape, F32)),
        grid=(nb,),
        in_specs=[full(z3s), full(ab_s), per_b(state_pool), per_b(state_conv3), per_b(state_delta)]
        + [full(a) for a in small],
        out_specs=(pl.BlockSpec((bw // LANE, rows, LANE), lambda b: (0, 0, 0)),
                   pl.BlockSpec((heads, rows, LANE), lambda b: (0, 0, 0)),
                   per_b(state_delta)),
        compiler_params=_params("arbitrary"),
        name="state_sample",
    )(z3s, ab_s, state_pool, state_conv3, state_delta, *small)


def _cols(z4, cb0, ncb, r0, r1):
    blk = z4[cb0:cb0 + ncb, :, r0:r1]
    return jnp.transpose(blk, (1, 2, 0, 3)).reshape(blk.shape[1], r1 - r0, ncb * LANE)


def kernel(x_prompt, x_sample, cache_win_k, cache_win_v, state_pool, state_conv, state_delta, norm_w, w_in,
           conv_w, a_log, dt_bias, delta_norm_w, pool_w, pool_scale, w_out, final_norm_w):
    batch, seq, d_model = x_prompt.shape
    nb = x_sample.shape[0]
    depth = w_in.shape[0]
    heads = C_HEADS
    m = batch * seq
    rows_s = 16
    nbuf = cache_win_k.shape[2]
    pool_cb = state_pool.shape[-1] // LANE

    hp = x_prompt.reshape(m, d_model)
    hs = jnp.zeros((rows_s, d_model), F32).at[:nb].set(x_sample.reshape(nb, d_model))

    outs_p = [[] for _ in range(5)]
    outs_s = [[] for _ in range(5)]
    for l in range(depth):
        w_main = w_in[l, :, :N_MAIN].astype(BF16)
        w_ab = w_in[l, :, N_MAIN:].astype(BF16)
        w_abt = w_ab.T
        w_o = w_out[l].astype(BF16)
        conv_w3 = jnp.transpose(conv_w[l].reshape(C_CONV, 3 * heads, LANE), (1, 0, 2))

        h = _rmsnorm(hp, norm_w[l], BF16, 256)
        z3, ab, abt, k_nat, v_nat = _inproj(h, w_main, w_ab, w_abt, 1024, 512)
        mix_a = _attn_prompt(z3, k_nat, v_nat, batch, seq)
        mix_b = _pool_prompt(z3, pool_w[l], pool_scale[l], batch, seq)
        mix_c, s_new = _delta_prompt(z3, ab, abt, conv_w3, a_log[l], dt_bias[l], delta_norm_w[l], batch, seq)
        hp = _outproj(mix_a, mix_b, mix_c, w_o, hp, 1024, 512)

        z4 = z3.reshape(CB_TOTAL, batch, seq, LANE)
        keep_p = min(seq, A_PATTERNS[-1][0])
        outs_p[0].append(k_nat.reshape(batch, seq, A_HEADS, LANE)[:, seq - keep_p:])
        outs_p[1].append(v_nat.reshape(batch, seq, A_HEADS, LANE)[:, seq - keep_p:])
        outs_p[2].append(_cols(z4, CB_UB, pool_cb, seq - B_BUF, seq))
        outs_p[3].append(_cols(z4, CB_QC, 3 * heads, seq - (C_CONV - 1), seq))
        outs_p[4].append(s_new)

        h_s = _rmsnorm(hs, norm_w[l], BF16, rows_s)
        z3s, ab_s, _, _, _ = _inproj(h_s, w_main, w_ab, w_abt, rows_s, 512)
        mix_as = _attn_sample(z3s, cache_win_k[l], cache_win_v[l])
        conv3 = jnp.transpose(state_conv[l].reshape(nb, C_CONV - 1, 3 * heads, LANE), (0, 2, 1, 3))
        mix_bs, mix_cs, st_new = _state_sample(z3s, ab_s, state_pool[l], conv3, state_delta[l], pool_w[l],
                                               pool_scale[l], conv_w3, a_log[l], dt_bias[l], delta_norm_w[l])
        hs = _outproj(mix_as, mix_bs, mix_cs, w_o, hs, rows_s, 512)

        z4s = z3s[:, :nb].reshape(CB_TOTAL, nb, 1, LANE)
        k_new = _cols(z4s, CB_KA, A_HEADS, 0, 1).reshape(nb, 1, A_HEADS, LANE)
        v_new = _cols(z4s, CB_VA, A_HEADS, 0, 1).reshape(nb, 1, A_HEADS, LANE)
        keep = min(nbuf + 1, A_PATTERNS[-1][0])
        outs_s[0].append(jnp.concatenate([cache_win_k[l], k_new], axis=1)[:, nbuf + 1 - keep:])
        outs_s[1].append(jnp.concatenate([cache_win_v[l], v_new], axis=1)[:, nbuf + 1 - keep:])
        outs_s[2].append(jnp.concatenate([state_pool[l], _cols(z4s, CB_UB, pool_cb, 0, 1)], axis=1)[:, 1:])
        outs_s[3].append(jnp.concatenate([state_conv[l], _cols(z4s, CB_QC, 3 * heads, 0, 1)], axis=1)[:, 1:])
        outs_s[4].append(st_new)

    y_prompt = _rmsnorm(hp, final_norm_w, F32, 256).reshape(batch, seq, d_model)
    y_sample = _rmsnorm(hs, final_norm_w, F32, rows_s)[:nb].reshape(nb, 1, d_model)
    stack = lambda xs: jnp.stack(xs, axis=0)
    return (y_prompt, y_sample) + tuple(stack(o) for o in outs_p) + tuple(stack(o) for o in outs_s)
```

```python
import functools

import jax
import jax.numpy as jnp
from jax import lax
from jax.experimental import pallas as pl
from jax.experimental.pallas import tpu as pltpu

F32 = jnp.float32
BF16 = jnp.bfloat16

LANE = 128
MXU_ROWS = 16
VMEM_LIMIT = 56 * 2**20
RMS_EPS = 1e-6

A_HEADS = 8
A_PATTERNS = ((128, 1), (512, 4), (2048, 16))
A_BLOCK = 128
A_UNROLL = 4
B_POOLS = (2, 4, 8, 16)
B_BUF = 15
C_HEADS = 16
C_CONV = 4
CONV_PAD = 8

CB_QA, CB_KA, CB_VA, CB_GA = 0, 8, 16, 24
CB_UB, CB_GB = 32, 40
CB_QC, CB_KC, CB_VC, CB_GC = 48, 64, 80, 96
CB_TOTAL = 112
N_MAIN = CB_TOTAL * LANE

HIGHEST = lax.Precision.HIGHEST


def _params(*sem):
    return pltpu.CompilerParams(dimension_semantics=sem, vmem_limit_bytes=VMEM_LIMIT)


def _sigmoid(x):
    return 1.0 / (1.0 + jnp.exp(-x))


def _silu(x):
    return x * _sigmoid(x)


def _softplus(x):
    return jnp.maximum(x, 0.0) + jnp.log(1.0 + jnp.exp(-jnp.abs(x)))


def _dot(a, b, precision=None):
    return jnp.dot(a, b, preferred_element_type=F32, precision=precision)


def _dot_nt(a, b, precision=None):
    return lax.dot_general(a, b, (((1,), (1,)), ((), ())), preferred_element_type=F32, precision=precision)


def _dot_tn(a, b, precision=None):
    return lax.dot_general(a, b, (((0,), (0,)), ((), ())), preferred_element_type=F32, precision=precision)


def _rms_body(x_ref, g_ref, o_ref):
    x = x_ref[...]
    ms = jnp.mean(x * x, axis=-1, keepdims=True)
    o_ref[...] = (x * lax.rsqrt(ms + RMS_EPS) * g_ref[...]).astype(o_ref.dtype)


def _rmsnorm(x, gain, out_dtype, tm):
    m, d = x.shape
    return pl.pallas_call(
        _rms_body,
        out_shape=jax.ShapeDtypeStruct((m, d), out_dtype),
        grid=(m // tm,),
        in_specs=[pl.BlockSpec((tm, d), lambda i: (i, 0)), pl.BlockSpec((1, d), lambda i: (0, 0))],
        out_specs=pl.BlockSpec((tm, d), lambda i: (i, 0)),
        compiler_params=_params("parallel"),
        name="rmsnorm",
    )(x, gain.reshape(1, d))


def _inproj_body(h_ref, w_ref, wab_ref, z_ref, ab_ref, k_ref, v_ref, *, k_tiles, v_tiles):
    j = pl.program_id(1)
    h = h_ref[...]
    acc = _dot(h, w_ref[...])
    for c in range(acc.shape[1] // LANE):
        z_ref[c] = acc[:, c * LANE:(c + 1) * LANE]

    @pl.when(j == 0)
    def _():
        ab_ref[...] = _dot(h, wab_ref[...])

    @pl.when((j >= k_tiles[0]) & (j < k_tiles[1]))
    def _():
        k_ref[...] = acc

    @pl.when((j >= v_tiles[0]) & (j < v_tiles[1]))
    def _():
        v_ref[...] = acc


def _inproj(h, w_all, layer, w_ab, tm, tn):
    m, d = h.shape
    nab = w_ab.shape[1]
    a_width = A_HEADS * LANE
    k_tiles = (CB_KA * LANE // tn, CB_VA * LANE // tn)
    v_tiles = (CB_VA * LANE // tn, CB_GA * LANE // tn)

    def natural(tiles):
        return pl.BlockSpec((tm, tn), lambda i, j: (i, jnp.clip(j - tiles[0], 0, tiles[1] - tiles[0] - 1)))

    return pl.pallas_call(
        functools.partial(_inproj_body, k_tiles=k_tiles, v_tiles=v_tiles),
        out_shape=(jax.ShapeDtypeStruct((CB_TOTAL, m, LANE), F32),
                   jax.ShapeDtypeStruct((m, nab), F32),
                   jax.ShapeDtypeStruct((m, a_width), F32),
                   jax.ShapeDtypeStruct((m, a_width), F32)),
        grid=(m // tm, N_MAIN // tn),
        in_specs=[pl.BlockSpec((tm, d), lambda i, j: (i, 0)),
                  pl.BlockSpec((None, d, tn), lambda i, j: (layer, 0, j)),
                  pl.BlockSpec((d, nab), lambda i, j: (0, 0))],
        out_specs=(pl.BlockSpec((tn // LANE, tm, LANE), lambda i, j: (j, i, 0)),
                   pl.BlockSpec((tm, nab), lambda i, j: (i, 0)),
                   natural(k_tiles), natural(v_tiles)),
        compiler_params=_params("parallel", "arbitrary"),
        name="inproj",
    )(h, w_all, w_ab)


def _outproj_body(ma_ref, mb_ref, mc_ref, w_ref, x_ref, o_ref, lhs_ref):
    @pl.when(pl.program_id(1) == 0)
    def _():
        c0 = 0
        for ref in (ma_ref, mb_ref, mc_ref):
            for c in range(ref.shape[0]):
                lhs_ref[:, (c0 + c) * LANE:(c0 + c + 1) * LANE] = ref[c].astype(BF16)
            c0 += ref.shape[0]

    o_ref[...] = x_ref[...] + _dot(lhs_ref[...], w_ref[...])


def _outproj(mix_a, mix_b, mix_c, w_all, layer, x, tm, tn):
    m, d = x.shape
    k = w_all.shape[1]

    def mix_spec(a):
        return pl.BlockSpec((a.shape[0], tm, LANE), lambda i, j: (0, i, 0))

    return pl.pallas_call(
        _outproj_body,
        out_shape=jax.ShapeDtypeStruct((m, d), F32),
        grid=(m // tm, d // tn),
        in_specs=[mix_spec(mix_a), mix_spec(mix_b), mix_spec(mix_c),
                  pl.BlockSpec((None, k, tn), lambda i, j: (layer, 0, j)),
                  pl.BlockSpec((tm, tn), lambda i, j: (i, j))],
        out_specs=pl.BlockSpec((tm, tn), lambda i, j: (i, j)),
        scratch_shapes=[pltpu.VMEM((tm, k), BF16)],
        compiler_params=_params("parallel", "arbitrary"),
        name="outproj",
    )(mix_a, mix_b, mix_c, w_all, x)


def _attn_prompt_body(q_ref, k_ref, v_ref, g_ref, o_ref, acc_ref, m_ref, l_ref, *, seq):
    n = A_BLOCK
    scale = LANE ** -0.5
    qi = lax.broadcasted_iota(jnp.int32, (n, n), 0)
    kj = lax.broadcasted_iota(jnp.int32, (n, n), 1)
    cur_mask = kj <= qi
    prev_mask = kj >= qi

    for pi, (window, d) in enumerate(A_PATTERNS):
        nblk = seq // window
        shift = nblk.bit_length() - 1

        def rows(t0, d=d):
            if d == 1:
                return pl.ds(pl.multiple_of(t0, n), n)
            return pl.ds(t0, n, stride=d)

        def blocks(it, carry, pi=pi, window=window, nblk=nblk, shift=shift, rows=rows):
            has_prev = nblk > 1
            t0s, scores = [], []
            for u in range(A_UNROLL):
                idx = it * A_UNROLL + u
                r = idx >> shift
                blk = idx & (nblk - 1)
                t0 = blk * window + r
                t0s.append(t0)
                q = q_ref[0, rows(t0), :].astype(BF16)
                s_c = _dot_nt(q, k_ref[rows(t0), :].astype(BF16))
                vc = v_ref[rows(t0), :].astype(BF16)
                if has_prev:
                    tp = jnp.maximum(blk - 1, 0) * window + r
                    s_p = _dot_nt(q, k_ref[rows(tp), :].astype(BF16))
                    vp = v_ref[rows(tp), :].astype(BF16)
                    scores.append((s_c, vc, s_p, vp, blk > 0))
                else:
                    scores.append((s_c, vc))
            probs = []
            for sc in scores:
                s_c = jnp.where(cur_mask, sc[0] * scale, -jnp.inf)
                m = jnp.max(s_c, axis=1, keepdims=True)
                if has_prev:
                    s_p = jnp.where(prev_mask & sc[4], sc[2] * scale, -jnp.inf)
                    m = jnp.maximum(m, jnp.max(s_p, axis=1, keepdims=True))
                p_c = jnp.exp(s_c - m)
                l = jnp.sum(p_c, axis=1, keepdims=True)
                if has_prev:
                    p_p = jnp.exp(s_p - m)
                    l = l + jnp.sum(p_p, axis=1, keepdims=True)
                    probs.append((m, l, p_c.astype(BF16), p_p.astype(BF16)))
                else:
                    probs.append((m, l, p_c.astype(BF16)))
            outs = []
            for sc, pr in zip(scores, probs):
                o = _dot(pr[2], sc[1])
                if has_prev:
                    o = o + _dot(pr[3], sc[3])
                outs.append(o)
            for t0, pr, o in zip(t0s, probs, outs):
                mb = jnp.broadcast_to(pr[0], (n, LANE))
                lb = jnp.broadcast_to(pr[1], (n, LANE))
                if pi == 0:
                    acc_ref[rows(t0), :] = o
                    m_ref[rows(t0), :] = mb
                    l_ref[rows(t0), :] = lb
                else:
                    m_old = m_ref[rows(t0), :]
                    m_new = jnp.maximum(m_old, mb)
                    w_old = jnp.exp(m_old - m_new)
                    w_cur = jnp.exp(mb - m_new)
                    acc_ref[rows(t0), :] = acc_ref[rows(t0), :] * w_old + o * w_cur
                    l_ref[rows(t0), :] = l_ref[rows(t0), :] * w_old + lb * w_cur
                    m_ref[rows(t0), :] = m_new
            return carry

        lax.fori_loop(0, d * nblk // A_UNROLL, blocks, 0)

    rows_out = 256

    def finish(i, carry):
        sl = pl.ds(pl.multiple_of(i * rows_out, rows_out), rows_out)
        out = acc_ref[sl, :] / l_ref[sl, :] * _silu(g_ref[0, sl, :])
        o_ref[0, sl, :] = out.astype(o_ref.dtype)
        return carry

    lax.fori_loop(0, seq // rows_out, finish, 0)


def _attn_prompt(z3, k_nat, v_nat, batch, seq):
    m = z3.shape[1]

    def spec(cb0):
        return pl.BlockSpec((1, seq, LANE), lambda b, h: (cb0 + h, b, 0))

    nat = pl.BlockSpec((seq, LANE), lambda b, h: (b, h))
    return pl.pallas_call(
        functools.partial(_attn_prompt_body, seq=seq),
        out_shape=jax.ShapeDtypeStruct((A_HEADS, m, LANE), BF16),
        grid=(batch, A_HEADS),
        in_specs=[spec(CB_QA), nat, nat, spec(CB_GA)],
        out_specs=pl.BlockSpec((1, seq, LANE), lambda b, h: (h, b, 0)),
        scratch_shapes=[pltpu.VMEM((seq, LANE), F32)] * 3,
        compiler_params=_params("parallel", "parallel"),
        name="attn_prompt",
    )(z3, k_nat, v_nat, z3)


def _pool_prompt_body(u_ref, g_ref, w_ref, sc_ref, o_ref, *, seq):
    grp = pl.program_id(1)
    row = lax.broadcasted_iota(jnp.int32, (seq, 1), 0)
    for gi, width in enumerate(B_POOLS):
        @pl.when(grp == gi)
        def _(width=width):
            u = jnp.concatenate([u_ref[0], u_ref[1]], axis=1)
            s = u
            step = 1
            while step < width:
                s = s + jnp.where(row >= step, pltpu.roll(s, step, axis=0), 0.0)
                step *= 2
            cnt = jnp.minimum(row + 1, width).astype(F32)
            y = (s / cnt - u).astype(BF16)
            out = _dot(y, w_ref[0]) * sc_ref[0]
            gate = jnp.concatenate([g_ref[0], g_ref[1]], axis=1)
            out = (out * _silu(gate)).astype(o_ref.dtype)
            o_ref[0] = out[:, :LANE]
            o_ref[1] = out[:, LANE:]


def _pool_prompt(z3, pool_w, pool_scale, batch, seq):
    m = z3.shape[1]
    ngrp = len(B_POOLS)
    gw = pool_w.shape[1]
    cpg = gw // LANE

    def spec(cb0):
        return pl.BlockSpec((cpg, seq, LANE), lambda b, g: (cb0 // cpg + g, b, 0))

    return pl.pallas_call(
        functools.partial(_pool_prompt_body, seq=seq),
        out_shape=jax.ShapeDtypeStruct((ngrp * cpg, m, LANE), BF16),
        grid=(batch, ngrp),
        in_specs=[spec(CB_UB), spec(CB_GB),
                  pl.BlockSpec((1, gw, gw), lambda b, g: (g, 0, 0)),
                  pl.BlockSpec((1, 1, gw), lambda b, g: (g, 0, 0))],
        out_specs=pl.BlockSpec((cpg, seq, LANE), lambda b, g: (g, b, 0)),
        compiler_params=_params("parallel", "parallel"),
        name="pool_prompt",
    )(z3, z3, pool_w.astype(BF16), pool_scale.reshape(ngrp, 1, gw))


def _unit_lower_inverses(mats):
    c = mats[0].shape[0]
    eye = (lax.broadcasted_iota(jnp.int32, (c, c), 0) == lax.broadcasted_iota(jnp.int32, (c, c), 1)).astype(F32)
    xs = [eye - a for a in mats]
    pbs = [(-a).astype(BF16) for a in mats]
    k = 2
    while k < c:
        pbs = [_dot(pb, pb).astype(BF16) for pb in pbs]
        xs = [x + _dot(x.astype(BF16), pb) for x, pb in zip(xs, pbs)]
        k *= 2
    return xs


def _delta_prompt_body(q_ref, k_ref, v_ref, gate_ref, ab_ref, cw_ref, alog_ref, dtb_ref, nw_ref,
                       o_ref, s_ref, xbuf_ref, gct_ref, *, group):
    heads = C_HEADS
    c = q_ref.shape[1]
    tile = c
    hd = LANE

    @pl.when(pl.program_id(1) == 0)
    def _():
        s_ref[...] = jnp.zeros_like(s_ref)
        xbuf_ref[:, 0:CONV_PAD, :] = jnp.zeros((3 * heads, CONV_PAD, hd), F32)

    for j in range(heads):
        xbuf_ref[j, CONV_PAD:CONV_PAD + tile, :] = q_ref[j]
        xbuf_ref[heads + j, CONV_PAD:CONV_PAD + tile, :] = k_ref[j]
        xbuf_ref[2 * heads + j, CONV_PAD:CONV_PAD + tile, :] = v_ref[j]

    g = -jnp.exp(alog_ref[...]) * _softplus(ab_ref[:, 0:heads] + dtb_ref[...])
    beta = _sigmoid(ab_ref[:, heads:2 * heads])

    ri = lax.broadcasted_iota(jnp.int32, (c, c), 0)
    ci = lax.broadcasted_iota(jnp.int32, (c, c), 1)
    causal = ri >= ci
    strict = ri > ci
    gcum = _dot(causal.astype(F32), g, HIGHEST)
    eye_h = (lax.broadcasted_iota(jnp.int32, (heads, heads), 0)
             == lax.broadcasted_iota(jnp.int32, (heads, heads), 1)).astype(F32)
    gct_ref[...] = _dot_nt(eye_h, gcum, HIGHEST)
    lane_h = lax.broadcasted_iota(jnp.int32, (1, heads), 1)

    def conv(j):
        w = cw_ref[j]
        first = CONV_PAD - (C_CONV - 1)
        y = xbuf_ref[j, pl.ds(first, tile), :] * w[0:1]
        for i in range(1, C_CONV):
            y = y + xbuf_ref[j, pl.ds(first + i, tile), :] * w[i:i + 1]
        return _silu(y)

    def l2n(x):
        return x * lax.rsqrt(jnp.sum(x * x, axis=-1, keepdims=True) + RMS_EPS)

    def group_body(hg, carry):
        hs = [hg * group + i for i in range(group)]
        states = [s_ref[0, h] for h in hs]
        pre = []
        for h in hs:
            onehot = lane_h == h
            q = l2n(conv(h)) * (hd ** -0.5)
            k = l2n(conv(heads + h))
            v = conv(2 * heads + h)
            gcol = jnp.sum(jnp.where(onehot, gcum, 0.0), axis=1, keepdims=True)
            bcol = jnp.sum(jnp.where(onehot, beta, 0.0), axis=1, keepdims=True)
            grow = gct_ref[pl.ds(h, 1), :]
            dec = jnp.exp(jnp.where(causal, gcol - grow, -jnp.inf))
            kb = k * bcol
            kq = _dot_nt(jnp.concatenate([kb, q], axis=0).astype(BF16), k.astype(BF16))
            a_mat = jnp.where(strict, kq[:c] * dec, 0.0)
            qk = (kq[c:] * dec).astype(BF16)
            egc = jnp.exp(gcol)
            rhs = jnp.concatenate([v * bcol, kb * egc], axis=1).astype(BF16)
            glast = gcol[c - 1:c, :]
            kd = (k * jnp.exp(glast - gcol)).astype(BF16)
            pre.append((a_mat, qk, rhs, (q * egc).astype(BF16), kd, jnp.exp(glast)))
        invs = _unit_lower_inverses([p[0] for p in pre])
        sols = [_dot(t.astype(BF16), p[2]) for t, p in zip(invs, pre)]
        ws_qs = [_dot(jnp.concatenate([sol[:, hd:].astype(BF16), p[3]], axis=0), st.astype(BF16))
                 for sol, p, st in zip(sols, pre, states)]
        v_news = [(sol[:, :hd] - wq[:c]).astype(BF16) for sol, wq in zip(sols, ws_qs)]
        outs = [wq[c:] + _dot(p[1], vn) for wq, p, vn in zip(ws_qs, pre, v_news)]
        new_states = [st * p[5] + _dot_tn(p[4], vn) for st, p, vn in zip(states, pre, v_news)]
        for h, o in zip(hs, outs):
            on = o * lax.rsqrt(jnp.mean(o * o, axis=-1, keepdims=True) + RMS_EPS) * nw_ref[...]
            o_ref[h] = (on * _silu(gate_ref[h])).astype(o_ref.dtype)
        for h, state in zip(hs, new_states):
            s_ref[0, h] = state
        return carry

    lax.fori_loop(0, heads // group, group_body, 0)

    for j in range(3 * heads):
        xbuf_ref[j, 0:CONV_PAD, :] = xbuf_ref[j, tile:tile + CONV_PAD, :]


def _delta_prompt(z3, ab, conv_w3, a_log, dt_bias, norm_w, batch, seq, tile=128, group=8):
    m = z3.shape[1]
    heads = C_HEADS
    nt = seq // tile

    def spec(cb0):
        return pl.BlockSpec((heads, tile, LANE), lambda b, t: (cb0 // heads, b * nt + t, 0))

    def full(a):
        return pl.BlockSpec(a.shape, lambda b, t: (0,) * a.ndim)

    small = (conv_w3, a_log.reshape(1, heads), dt_bias.reshape(1, heads), norm_w.reshape(1, LANE))
    return pl.pallas_call(
        functools.partial(_delta_prompt_body, group=group),
        out_shape=(jax.ShapeDtypeStruct((heads, m, LANE), BF16),
                   jax.ShapeDtypeStruct((batch, heads, LANE, LANE), F32)),
        grid=(batch, nt),
        in_specs=[spec(CB_QC), spec(CB_KC), spec(CB_VC), spec(CB_GC),
                  pl.BlockSpec((tile, 2 * heads), lambda b, t: (b * nt + t, 0))] + [full(a) for a in small],
        out_specs=(pl.BlockSpec((heads, tile, LANE), lambda b, t: (0, b * nt + t, 0)),
                   pl.BlockSpec((1, heads, LANE, LANE), lambda b, t: (b, 0, 0, 0))),
        scratch_shapes=[pltpu.VMEM((3 * heads, CONV_PAD + tile, LANE), F32),
                        pltpu.VMEM((heads, tile), F32)],
        compiler_params=_params("parallel", "arbitrary"),
        name="delta_prompt",
    )(z3, z3, z3, z3, ab, *small)


def _attn_sample_body(z_ref, *refs, npat):
    kv_refs, o_ref = refs[:2 * npat], refs[2 * npat]
    b = pl.program_id(0)
    heads = A_HEADS
    n = A_BLOCK

    @pl.when(b == 0)
    def _():
        o_ref[...] = jnp.zeros_like(o_ref)

    row = pl.ds(b, 1)

    def head_rows(cb0):
        return jnp.concatenate([z_ref[cb0 + h, row, :] for h in range(heads)], axis=0)

    q = head_rows(CB_QA).astype(BF16)
    kn = head_rows(CB_KA).astype(BF16).astype(F32)
    vn = head_rows(CB_VA).astype(BF16).astype(F32)
    gate = head_rows(CB_GA)
    scale = LANE ** -0.5
    qpad = jnp.concatenate([q, jnp.zeros((MXU_ROWS - heads, LANE), BF16)], axis=0)
    s_new = jnp.sum(q.astype(F32) * kn, axis=1, keepdims=True) * scale
    col = lax.broadcasted_iota(jnp.int32, (heads, n * heads), 1)
    own = (col & (heads - 1)) == lax.broadcasted_iota(jnp.int32, (heads, n * heads), 0)
    parts = []
    for i in range(npat):
        kmat = kv_refs[2 * i][...].reshape(n * heads, LANE).astype(BF16)
        vmat = kv_refs[2 * i + 1][...].reshape(n * heads, LANE).astype(BF16)
        s = jnp.where(own, _dot_nt(qpad, kmat)[:heads] * scale, -jnp.inf)
        m = jnp.maximum(jnp.max(s, axis=1, keepdims=True), s_new)
        p = jnp.exp(s - m)
        pn = jnp.exp(s_new - m)
        l = jnp.sum(p, axis=1, keepdims=True) + pn
        ppad = jnp.concatenate([p, jnp.zeros((MXU_ROWS - heads, n * heads), F32)], axis=0).astype(BF16)
        o = _dot(ppad, vmat)[:heads] + pn.astype(BF16).astype(F32) * vn
        parts.append((o, m, l))
    m_all = functools.reduce(jnp.maximum, [m for _, m, _ in parts])
    num = jnp.zeros((heads, LANE), F32)
    den = jnp.zeros((heads, 1), F32)
    for o, m, l in parts:
        wgt = jnp.exp(m - m_all)
        num = num + o * wgt
        den = den + l * wgt
    out = num / den * _silu(gate)
    for h in range(heads):
        o_ref[h, row, :] = out[h:h + 1]


def _attn_sample(z3s, cache_k, cache_v, layer):
    depth, nb, nbuf, heads, hd = cache_k.shape
    rows = z3s.shape[1]
    n = A_BLOCK
    assert heads == A_HEADS and hd == LANE
    assert all(nbuf % window == 0 for window, _ in A_PATTERNS), "cached window shorter than a pattern"
    views, specs = [], []
    for window, d in A_PATTERNS:
        last = nbuf // d // n - 1
        spec = pl.BlockSpec((None, None, n, None, heads, hd), lambda b, last=last: (layer, b, last, 0, 0, 0))
        for cache in (cache_k, cache_v):
            views.append(cache.reshape(depth, nb, nbuf // d, d, heads, hd))
            specs.append(spec)
    return pl.pallas_call(
        functools.partial(_attn_sample_body, npat=len(A_PATTERNS)),
        out_shape=jax.ShapeDtypeStruct((heads, rows, LANE), F32),
        grid=(nb,),
        in_specs=[pl.BlockSpec(z3s.shape, lambda b: (0, 0, 0))] + specs,
        out_specs=pl.BlockSpec((heads, rows, LANE), lambda b: (0, 0, 0)),
        compiler_params=_params("arbitrary"),
        name="attn_sample",
    )(z3s, *views)


def _state_sample_body(z_ref, ab_ref, pool_ref, conv_ref, st_ref, pw_ref, psc_ref, cw_ref, alog_ref, dtb_ref,
                       nw_ref, mb_ref, mc_ref, so_ref):
    b = pl.program_id(0)
    heads = C_HEADS
    hd = LANE

    @pl.when(b == 0)
    def _():
        mb_ref[...] = jnp.zeros_like(mb_ref)
        mc_ref[...] = jnp.zeros_like(mc_ref)

    row = pl.ds(b, 1)

    gw = pw_ref.shape[1]
    cpg = gw // LANE
    for gi, width in enumerate(B_POOLS):
        ys = []
        for j in range(cpg):
            cb = gi * cpg + j
            un = z_ref[CB_UB + cb, row, :]
            prev = pool_ref[0, B_BUF - (width - 1):B_BUF, cb * LANE:(cb + 1) * LANE]
            mean = (jnp.sum(prev, axis=0, keepdims=True) + un) / float(width)
            ys.append(mean - un)
        y = jnp.broadcast_to(jnp.concatenate(ys, axis=1), (MXU_ROWS, gw)).astype(BF16)
        out = _dot(y, pw_ref[gi])[0:1] * psc_ref[:, gi * gw:(gi + 1) * gw]
        for j in range(cpg):
            cb = gi * cpg + j
            mb_ref[cb, row, :] = out[:, j * LANE:(j + 1) * LANE] * _silu(z_ref[CB_GB + cb, row, :])

    arow = ab_ref[row, :]
    lane_ab = lax.broadcasted_iota(jnp.int32, (1, 2 * heads), 1)
    lane_h = lax.broadcasted_iota(jnp.int32, (1, heads), 1)
    eye = lax.broadcasted_iota(jnp.int32, (hd, hd), 0) == lax.broadcasted_iota(jnp.int32, (hd, hd), 1)

    def pick(vec, lanes, idx):
        return jnp.sum(jnp.where(lanes == idx, vec, 0.0), axis=1, keepdims=True)

    def column(x):
        return jnp.sum(jnp.where(eye, jnp.broadcast_to(x, (hd, hd)), 0.0), axis=1, keepdims=True)

    def conv(j):
        st = conv_ref[0, j]
        w = cw_ref[j]
        y = st[0:1] * w[0:1]
        y = y + st[1:2] * w[1:2]
        y = y + st[2:3] * w[2:3]
        y = y + z_ref[CB_QC + j, row, :] * w[3:4]
        return _silu(y)

    def l2n(x):
        return x * lax.rsqrt(jnp.sum(x * x, axis=-1, keepdims=True) + RMS_EPS)

    def head_body(h, carry):
        q = l2n(conv(h)) * (hd ** -0.5)
        k = l2n(conv(heads + h))
        v = conv(2 * heads + h)
        g = -jnp.exp(pick(alog_ref[...], lane_h, h)) * _softplus(pick(arow, lane_ab, h) + pick(dtb_ref[...], lane_h, h))
        beta = _sigmoid(pick(arow, lane_ab, heads + h))
        state = st_ref[0, h] * jnp.exp(g)
        kcol = column(k)
        delta = (v - jnp.sum(kcol * state, axis=0, keepdims=True)) * beta
        state = state + kcol * delta
        o = jnp.sum(column(q) * state, axis=0, keepdims=True)
        on = o * lax.rsqrt(jnp.mean(o * o, axis=-1, keepdims=True) + RMS_EPS) * nw_ref[...]
        mc_ref[h, row, :] = on * _silu(z_ref[CB_GC + h, row, :])
        so_ref[0, h] = state
        return carry

    lax.fori_loop(0, heads, head_body, 0)


def _state_sample(z3s, ab_s, state_pool, state_conv3, state_delta, pool_w, pool_scale, conv_w3, a_log, dt_bias,
                  norm_w):
    nb = state_pool.shape[0]
    rows = z3s.shape[1]
    heads = C_HEADS
    bw = state_pool.shape[2]

    def full(a):
        return pl.BlockSpec(a.shape, lambda b: (0,) * a.ndim)

    def per_b(a):
        return pl.BlockSpec((1,) + a.shape[1:], lambda b: (b,) + (0,) * (a.ndim - 1))

    small = (pool_w.astype(BF16), pool_scale.reshape(1, bw), conv_w3, a_log.reshape(1, heads),
             dt_bias.reshape(1, heads), norm_w.reshape(1, LANE))
    return pl.pallas_call(
        _state_sample_body,
        out_shape=(jax.ShapeDtypeStruct((bw // LANE, rows, LANE), F32),
                   jax.ShapeDtypeStruct((heads, rows, LANE), F32),
                   jax.ShapeDtypeStruct(state_delta.shape, F32)),
        grid=(nb,),
        in_specs=[full(z3s), full(ab_s), per_b(state_pool), per_b(state_conv3), per_b(state_delta)]
        + [full(a) for a in small],
        out_specs=(pl.BlockSpec((bw // LANE, rows, LANE), lambda b: (0, 0, 0)),
                   pl.BlockSpec((heads, rows, LANE), lambda b: (0, 0, 0)),
                   per_b(state_delta)),
        compiler_params=_params("arbitrary"),
        name="state_sample",
    )(z3s, ab_s, state_pool, state_conv3, state_delta, *small)


def _cache_shift_body(ck_ref, cv_ref, kn_ref, vn_ref, ok_ref, ov_ref, sem):
    depth, nb, nbuf = ck_ref.shape[:3]
    copies = []
    for c_ref, n_ref, o_ref in ((ck_ref, kn_ref, ok_ref), (cv_ref, vn_ref, ov_ref)):
        for l in range(depth):
            for b in range(nb):
                copies.append(pltpu.make_async_copy(c_ref.at[l, b, pl.ds(1, nbuf - 1)],
                                                    o_ref.at[l, b, pl.ds(0, nbuf - 1)], sem.at[len(copies)]))
            copies.append(pltpu.make_async_copy(n_ref.at[l], o_ref.at[l, :, pl.ds(nbuf - 1, 1)],
                                                sem.at[len(copies)]))
    for c in copies:
        c.start()
    for c in copies:
        c.wait()


def _cache_shift(cache_k, cache_v, k_new, v_new):
    depth, nb = cache_k.shape[:2]
    anywhere = pl.BlockSpec(memory_space=pl.ANY)
    out = jax.ShapeDtypeStruct(cache_k.shape, cache_k.dtype)
    return pl.pallas_call(
        _cache_shift_body,
        out_shape=(out, out),
        in_specs=[anywhere] * 4,
        out_specs=(anywhere, anywhere),
        scratch_shapes=[pltpu.SemaphoreType.DMA((2 * depth * (nb + 1),))],
        name="cache_shift",
    )(cache_k, cache_v, k_new, v_new)


def _cols(z4, cb0, ncb, r0, r1):
    blk = z4[cb0:cb0 + ncb, :, r0:r1]
    return jnp.transpose(blk, (1, 2, 0, 3)).reshape(blk.shape[1], r1 - r0, ncb * LANE)


def kernel(x_prompt, x_sample, cache_win_k, cache_win_v, state_pool, state_conv, state_delta, norm_w, w_in,
           conv_w, a_log, dt_bias, delta_norm_w, pool_w, pool_scale, w_out, final_norm_w):
    batch, seq, d_model = x_prompt.shape
    nb = x_sample.shape[0]
    depth = w_in.shape[0]
    heads = C_HEADS
    m = batch * seq
    rows_s = 16
    nbuf = cache_win_k.shape[2]
    pool_cb = state_pool.shape[-1] // LANE

    hp = x_prompt.reshape(m, d_model)
    hs = jnp.zeros((rows_s, d_model), F32).at[:nb].set(x_sample.reshape(nb, d_model))

    assert nbuf == A_PATTERNS[-1][0], "the window cache is expected full: one row in, one row out"

    outs_p = [[] for _ in range(5)]
    outs_s = [[] for _ in range(5)]
    new_k, new_v = [], []
    w_in_b = w_in.astype(BF16)
    w_out_b = w_out.astype(BF16)
    for l in range(depth):
        w_ab = w_in[l, :, N_MAIN:].astype(BF16)
        conv_w3 = jnp.transpose(conv_w[l].reshape(C_CONV, 3 * heads, LANE), (1, 0, 2))

        h = _rmsnorm(hp, norm_w[l], BF16, 256)
        z3, ab, k_nat, v_nat = _inproj(h, w_in_b, l, w_ab, 1024, 512)
        mix_a = _attn_prompt(z3, k_nat, v_nat, batch, seq)
        mix_b = _pool_prompt(z3, pool_w[l], pool_scale[l], batch, seq)
        mix_c, s_new = _delta_prompt(z3, ab, conv_w3, a_log[l], dt_bias[l], delta_norm_w[l], batch, seq)
        hp = _outproj(mix_a, mix_b, mix_c, w_out_b, l, hp, 1024, 512)

        z4 = z3.reshape(CB_TOTAL, batch, seq, LANE)
        keep_p = min(seq, A_PATTERNS[-1][0])
        outs_p[0].append(k_nat.reshape(batch, seq, A_HEADS, LANE)[:, seq - keep_p:])
        outs_p[1].append(v_nat.reshape(batch, seq, A_HEADS, LANE)[:, seq - keep_p:])
        outs_p[2].append(_cols(z4, CB_UB, pool_cb, seq - B_BUF, seq))
        outs_p[3].append(_cols(z4, CB_QC, 3 * heads, seq - (C_CONV - 1), seq))
        outs_p[4].append(s_new)

        h_s = _rmsnorm(hs, norm_w[l], BF16, rows_s)
        z3s, ab_s, _, _ = _inproj(h_s, w_in_b, l, w_ab, rows_s, 512)
        mix_as = _attn_sample(z3s, cache_win_k, cache_win_v, l)
        conv3 = jnp.transpose(state_conv[l].reshape(nb, C_CONV - 1, 3 * heads, LANE), (0, 2, 1, 3))
        mix_bs, mix_cs, st_new = _state_sample(z3s, ab_s, state_pool[l], conv3, state_delta[l], pool_w[l],
                                               pool_scale[l], conv_w3, a_log[l], dt_bias[l], delta_norm_w[l])
        hs = _outproj(mix_as, mix_bs, mix_cs, w_out_b, l, hs, rows_s, 512)

        z4s = z3s[:, :nb].reshape(CB_TOTAL, nb, 1, LANE)
        new_k.append(_cols(z4s, CB_KA, A_HEADS, 0, 1).reshape(nb, 1, A_HEADS, LANE))
        new_v.append(_cols(z4s, CB_VA, A_HEADS, 0, 1).reshape(nb, 1, A_HEADS, LANE))
        outs_s[2].append(jnp.concatenate([state_pool[l], _cols(z4s, CB_UB, pool_cb, 0, 1)], axis=1)[:, 1:])
        outs_s[3].append(jnp.concatenate([state_conv[l], _cols(z4s, CB_QC, 3 * heads, 0, 1)], axis=1)[:, 1:])
        outs_s[4].append(st_new)

    y_prompt = _rmsnorm(hp, final_norm_w, F32, 256).reshape(batch, seq, d_model)
    y_sample = _rmsnorm(hs, final_norm_w, F32, rows_s)[:nb].reshape(nb, 1, d_model)
    stack = lambda xs: jnp.stack(xs, axis=0)
    win_k, win_v = _cache_shift(cache_win_k, cache_win_v, stack(new_k), stack(new_v))
    return ((y_prompt, y_sample) + tuple(stack(o) for o in outs_p) + (win_k, win_v)
            + tuple(stack(o) for o in outs_s[2:]))
```

```python
import functools

import jax
import jax.numpy as jnp
from jax import lax
from jax.experimental import pallas as pl
from jax.experimental.pallas import tpu as pltpu

F32 = jnp.float32
BF16 = jnp.bfloat16

LANE = 128
MXU_ROWS = 16
VMEM_LIMIT = 56 * 2**20
RMS_EPS = 1e-6

A_HEADS = 8
A_PATTERNS = ((128, 1), (512, 4), (2048, 16))
A_BLOCK = 128
A_UNROLL = 4
B_POOLS = (2, 4, 8, 16)
B_BUF = 15
C_HEADS = 16
C_CONV = 4
CONV_PAD = 8

CB_QA, CB_KA, CB_VA, CB_GA = 0, 8, 16, 24
CB_UB, CB_GB = 32, 40
CB_QC, CB_KC, CB_VC, CB_GC = 48, 64, 80, 96
CB_TOTAL = 112
N_MAIN = CB_TOTAL * LANE

HIGHEST = lax.Precision.HIGHEST


def _params(*sem):
    return pltpu.CompilerParams(dimension_semantics=sem, vmem_limit_bytes=VMEM_LIMIT)


def _sigmoid(x):
    return 1.0 / (1.0 + jnp.exp(-x))


def _silu(x):
    return x * _sigmoid(x)


def _softplus(x):
    return jnp.maximum(x, 0.0) + jnp.log(1.0 + jnp.exp(-jnp.abs(x)))


def _dot(a, b, precision=None):
    return jnp.dot(a, b, preferred_element_type=F32, precision=precision)


def _dot_nt(a, b, precision=None):
    return lax.dot_general(a, b, (((1,), (1,)), ((), ())), preferred_element_type=F32, precision=precision)


def _dot_tn(a, b, precision=None):
    return lax.dot_general(a, b, (((0,), (0,)), ((), ())), preferred_element_type=F32, precision=precision)


def _rms_body(x_ref, g_ref, o_ref):
    x = x_ref[...]
    ms = jnp.mean(x * x, axis=-1, keepdims=True)
    o_ref[...] = (x * lax.rsqrt(ms + RMS_EPS) * g_ref[...]).astype(o_ref.dtype)


def _rmsnorm(x, gain, out_dtype, tm):
    m, d = x.shape
    return pl.pallas_call(
        _rms_body,
        out_shape=jax.ShapeDtypeStruct((m, d), out_dtype),
        grid=(m // tm,),
        in_specs=[pl.BlockSpec((tm, d), lambda i: (i, 0)), pl.BlockSpec((1, d), lambda i: (0, 0))],
        out_specs=pl.BlockSpec((tm, d), lambda i: (i, 0)),
        compiler_params=_params("parallel"),
        name="rmsnorm",
    )(x, gain.reshape(1, d))


def _inproj_body(h_ref, w_ref, wab_ref, z_ref, ab_ref, k_ref, v_ref, *, k_tiles, v_tiles):
    j = pl.program_id(1)
    h = h_ref[...]
    acc = _dot(h, w_ref[...])
    for c in range(acc.shape[1] // LANE):
        z_ref[c] = acc[:, c * LANE:(c + 1) * LANE]

    @pl.when(j == 0)
    def _():
        ab_ref[...] = _dot(h, wab_ref[...])

    @pl.when((j >= k_tiles[0]) & (j < k_tiles[1]))
    def _():
        k_ref[...] = acc

    @pl.when((j >= v_tiles[0]) & (j < v_tiles[1]))
    def _():
        v_ref[...] = acc


def _inproj(h, w_all, layer, w_ab, tm, tn):
    m, d = h.shape
    nab = w_ab.shape[1]
    a_width = A_HEADS * LANE
    k_tiles = (CB_KA * LANE // tn, CB_VA * LANE // tn)
    v_tiles = (CB_VA * LANE // tn, CB_GA * LANE // tn)

    def natural(tiles):
        return pl.BlockSpec((tm, tn), lambda i, j: (i, jnp.clip(j - tiles[0], 0, tiles[1] - tiles[0] - 1)))

    return pl.pallas_call(
        functools.partial(_inproj_body, k_tiles=k_tiles, v_tiles=v_tiles),
        out_shape=(jax.ShapeDtypeStruct((CB_TOTAL, m, LANE), F32),
                   jax.ShapeDtypeStruct((m, nab), F32),
                   jax.ShapeDtypeStruct((m, a_width), F32),
                   jax.ShapeDtypeStruct((m, a_width), F32)),
        grid=(m // tm, N_MAIN // tn),
        in_specs=[pl.BlockSpec((tm, d), lambda i, j: (i, 0)),
                  pl.BlockSpec((None, d, tn), lambda i, j: (layer, 0, j)),
                  pl.BlockSpec((d, nab), lambda i, j: (0, 0))],
        out_specs=(pl.BlockSpec((tn // LANE, tm, LANE), lambda i, j: (j, i, 0)),
                   pl.BlockSpec((tm, nab), lambda i, j: (i, 0)),
                   natural(k_tiles), natural(v_tiles)),
        compiler_params=_params("parallel", "arbitrary"),
        name="inproj",
    )(h, w_all, w_ab)


def _outproj_body(ma_ref, mb_ref, mc_ref, w_ref, x_ref, o_ref, lhs_ref):
    @pl.when(pl.program_id(1) == 0)
    def _():
        c0 = 0
        for ref in (ma_ref, mb_ref, mc_ref):
            for c in range(ref.shape[0]):
                lhs_ref[:, (c0 + c) * LANE:(c0 + c + 1) * LANE] = ref[c].astype(BF16)
            c0 += ref.shape[0]

    o_ref[...] = x_ref[...] + _dot(lhs_ref[...], w_ref[...])


def _outproj(mix_a, mix_b, mix_c, w_all, layer, x, tm, tn):
    m, d = x.shape
    k = w_all.shape[1]

    def mix_spec(a):
        return pl.BlockSpec((a.shape[0], tm, LANE), lambda i, j: (0, i, 0))

    return pl.pallas_call(
        _outproj_body,
        out_shape=jax.ShapeDtypeStruct((m, d), F32),
        grid=(m // tm, d // tn),
        in_specs=[mix_spec(mix_a), mix_spec(mix_b), mix_spec(mix_c),
                  pl.BlockSpec((None, k, tn), lambda i, j: (layer, 0, j)),
                  pl.BlockSpec((tm, tn), lambda i, j: (i, j))],
        out_specs=pl.BlockSpec((tm, tn), lambda i, j: (i, j)),
        scratch_shapes=[pltpu.VMEM((tm, k), BF16)],
        compiler_params=_params("parallel", "arbitrary"),
        name="outproj",
    )(mix_a, mix_b, mix_c, w_all, x)


def _attn_prompt_body(q_ref, k_ref, v_ref, g_ref, o_ref, acc_ref, m_ref, l_ref, *, seq):
    n = A_BLOCK
    scale = LANE ** -0.5
    qi = lax.broadcasted_iota(jnp.int32, (n, n), 0)
    kj = lax.broadcasted_iota(jnp.int32, (n, n), 1)
    cur_mask = kj <= qi
    prev_mask = kj >= qi

    def rows(t0, d):
        return pl.ds(t0, n) if d == 1 else pl.ds(t0, n, stride=d)

    blocks = []
    for pi, (window, d) in enumerate(A_PATTERNS):
        for r in range(d):
            for blk in range(seq // window):
                t0 = blk * window + r
                blocks.append((pi, rows(t0, d), rows(t0 - window, d) if blk > 0 else None))
    waves = [blocks[i:i + A_UNROLL] for i in range(0, len(blocks), A_UNROLL)]

    def scores(block):
        _, cur, prev = block
        q = q_ref[0, cur, :].astype(BF16)
        sc = [_dot_nt(q, k_ref[cur, :].astype(BF16)), v_ref[cur, :].astype(BF16)]
        if prev is not None:
            sc += [_dot_nt(q, k_ref[prev, :].astype(BF16)), v_ref[prev, :].astype(BF16)]
        return sc

    def softmax(sc):
        s_c = jnp.where(cur_mask, sc[0] * scale, -jnp.inf)
        if len(sc) == 2:
            m = jnp.max(s_c, axis=1, keepdims=True)
            p_c = jnp.exp(s_c - m)
            return m, jnp.sum(p_c, axis=1, keepdims=True), p_c.astype(BF16)
        s_p = jnp.where(prev_mask, sc[2] * scale, -jnp.inf)
        m = jnp.max(jnp.maximum(s_c, s_p), axis=1, keepdims=True)
        p_c = jnp.exp(s_c - m)
        p_p = jnp.exp(s_p - m)
        return m, jnp.sum(p_c + p_p, axis=1, keepdims=True), p_c.astype(BF16), p_p.astype(BF16)

    def values(sc, pr):
        o = _dot(pr[2], sc[1])
        if len(sc) > 2:
            o = o + _dot(pr[3], sc[3])
        return o

    def accumulate(block, pr, o):
        pi, cur, _ = block
        mb = jnp.broadcast_to(pr[0], (n, LANE))
        lb = jnp.broadcast_to(pr[1], (n, LANE))
        if pi == 0:
            acc_ref[cur, :] = o
            m_ref[cur, :] = mb
            l_ref[cur, :] = lb
        else:
            m_old = m_ref[cur, :]
            m_new = jnp.maximum(m_old, mb)
            w_old = jnp.exp(m_old - m_new)
            w_cur = jnp.exp(mb - m_new)
            acc_ref[cur, :] = acc_ref[cur, :] * w_old + o * w_cur
            l_ref[cur, :] = l_ref[cur, :] * w_old + lb * w_cur
            m_ref[cur, :] = m_new

    sc_of, pr_of, o_of = {}, {}, {}
    for step in range(len(waves) + 3):
        if step < len(waves):
            sc_of[step] = [scores(blk) for blk in waves[step]]
        w = step - 1
        if 0 <= w < len(waves):
            pr_of[w] = [softmax(sc) for sc in sc_of[w]]
        w = step - 2
        if 0 <= w < len(waves):
            o_of[w] = [values(sc, pr) for sc, pr in zip(sc_of.pop(w), pr_of[w])]
        w = step - 3
        if 0 <= w < len(waves):
            for blk, pr, o in zip(waves[w], pr_of.pop(w), o_of.pop(w)):
                accumulate(blk, pr, o)

    rows_out = 256

    def finish(i, carry):
        sl = pl.ds(pl.multiple_of(i * rows_out, rows_out), rows_out)
        out = acc_ref[sl, :] / l_ref[sl, :] * _silu(g_ref[0, sl, :])
        o_ref[0, sl, :] = out.astype(o_ref.dtype)
        return carry

    lax.fori_loop(0, seq // rows_out, finish, 0)


def _attn_prompt(z3, k_nat, v_nat, batch, seq):
    m = z3.shape[1]

    def spec(cb0):
        return pl.BlockSpec((1, seq, LANE), lambda b, h: (cb0 + h, b, 0))

    nat = pl.BlockSpec((seq, LANE), lambda b, h: (b, h))
    return pl.pallas_call(
        functools.partial(_attn_prompt_body, seq=seq),
        out_shape=jax.ShapeDtypeStruct((A_HEADS, m, LANE), BF16),
        grid=(batch, A_HEADS),
        in_specs=[spec(CB_QA), nat, nat, spec(CB_GA)],
        out_specs=pl.BlockSpec((1, seq, LANE), lambda b, h: (h, b, 0)),
        scratch_shapes=[pltpu.VMEM((seq, LANE), F32)] * 3,
        compiler_params=_params("parallel", "parallel"),
        name="attn_prompt",
    )(z3, k_nat, v_nat, z3)


def _pool_prompt_body(u_ref, g_ref, w_ref, sc_ref, o_ref, *, seq):
    grp = pl.program_id(1)
    row = lax.broadcasted_iota(jnp.int32, (seq, 1), 0)
    for gi, width in enumerate(B_POOLS):
        @pl.when(grp == gi)
        def _(width=width):
            u = jnp.concatenate([u_ref[0], u_ref[1]], axis=1)
            s = u
            step = 1
            while step < width:
                s = s + jnp.where(row >= step, pltpu.roll(s, step, axis=0), 0.0)
                step *= 2
            cnt = jnp.minimum(row + 1, width).astype(F32)
            y = (s / cnt - u).astype(BF16)
            out = _dot(y, w_ref[0]) * sc_ref[0]
            gate = jnp.concatenate([g_ref[0], g_ref[1]], axis=1)
            out = (out * _silu(gate)).astype(o_ref.dtype)
            o_ref[0] = out[:, :LANE]
            o_ref[1] = out[:, LANE:]


def _pool_prompt(z3, pool_w, pool_scale, batch, seq):
    m = z3.shape[1]
    ngrp = len(B_POOLS)
    gw = pool_w.shape[1]
    cpg = gw // LANE

    def spec(cb0):
        return pl.BlockSpec((cpg, seq, LANE), lambda b, g: (cb0 // cpg + g, b, 0))

    return pl.pallas_call(
        functools.partial(_pool_prompt_body, seq=seq),
        out_shape=jax.ShapeDtypeStruct((ngrp * cpg, m, LANE), BF16),
        grid=(batch, ngrp),
        in_specs=[spec(CB_UB), spec(CB_GB),
                  pl.BlockSpec((1, gw, gw), lambda b, g: (g, 0, 0)),
                  pl.BlockSpec((1, 1, gw), lambda b, g: (g, 0, 0))],
        out_specs=pl.BlockSpec((cpg, seq, LANE), lambda b, g: (g, b, 0)),
        compiler_params=_params("parallel", "parallel"),
        name="pool_prompt",
    )(z3, z3, pool_w.astype(BF16), pool_scale.reshape(ngrp, 1, gw))


def _unit_lower_inverses(mats):
    c = mats[0].shape[0]
    eye = (lax.broadcasted_iota(jnp.int32, (c, c), 0) == lax.broadcasted_iota(jnp.int32, (c, c), 1)).astype(F32)
    xs = [eye - a for a in mats]
    pbs = [(-a).astype(BF16) for a in mats]
    k = 2
    while k < c:
        pbs = [_dot(pb, pb).astype(BF16) for pb in pbs]
        xs = [x + _dot(x.astype(BF16), pb) for x, pb in zip(xs, pbs)]
        k *= 2
    return xs


def _delta_prompt_body(q_ref, k_ref, v_ref, gate_ref, ab_ref, cw_ref, alog_ref, dtb_ref, nw_ref,
                       o_ref, s_ref, xbuf_ref, gct_ref, *, group):
    heads = C_HEADS
    c = q_ref.shape[1]
    tile = c
    hd = LANE

    @pl.when(pl.program_id(1) == 0)
    def _():
        s_ref[...] = jnp.zeros_like(s_ref)
        xbuf_ref[:, 0:CONV_PAD, :] = jnp.zeros((3 * heads, CONV_PAD, hd), F32)

    for j in range(heads):
        xbuf_ref[j, CONV_PAD:CONV_PAD + tile, :] = q_ref[j]
        xbuf_ref[heads + j, CONV_PAD:CONV_PAD + tile, :] = k_ref[j]
        xbuf_ref[2 * heads + j, CONV_PAD:CONV_PAD + tile, :] = v_ref[j]

    g = -jnp.exp(alog_ref[...]) * _softplus(ab_ref[:, 0:heads] + dtb_ref[...])
    beta = _sigmoid(ab_ref[:, heads:2 * heads])

    ri = lax.broadcasted_iota(jnp.int32, (c, c), 0)
    ci = lax.broadcasted_iota(jnp.int32, (c, c), 1)
    causal = ri >= ci
    strict = ri > ci
    gcum = _dot(causal.astype(F32), g, HIGHEST)
    eye_h = (lax.broadcasted_iota(jnp.int32, (heads, heads), 0)
             == lax.broadcasted_iota(jnp.int32, (heads, heads), 1)).astype(F32)
    gct_ref[...] = _dot_nt(eye_h, gcum, HIGHEST)
    lane_h = lax.broadcasted_iota(jnp.int32, (1, heads), 1)

    def conv(j):
        w = cw_ref[j]
        first = CONV_PAD - (C_CONV - 1)
        y = xbuf_ref[j, pl.ds(first, tile), :] * w[0:1]
        for i in range(1, C_CONV):
            y = y + xbuf_ref[j, pl.ds(first + i, tile), :] * w[i:i + 1]
        return _silu(y)

    def l2n(x):
        return x * lax.rsqrt(jnp.sum(x * x, axis=-1, keepdims=True) + RMS_EPS)

    def group_body(hg, carry):
        hs = [hg * group + i for i in range(group)]
        states = [s_ref[0, h] for h in hs]
        pre = []
        for h in hs:
            onehot = lane_h == h
            q = l2n(conv(h)) * (hd ** -0.5)
            k = l2n(conv(heads + h))
            v = conv(2 * heads + h)
            gcol = jnp.sum(jnp.where(onehot, gcum, 0.0), axis=1, keepdims=True)
            bcol = jnp.sum(jnp.where(onehot, beta, 0.0), axis=1, keepdims=True)
            grow = gct_ref[pl.ds(h, 1), :]
            dec = jnp.exp(jnp.where(causal, gcol - grow, -jnp.inf))
            kb = k * bcol
            kq = _dot_nt(jnp.concatenate([kb, q], axis=0).astype(BF16), k.astype(BF16))
            a_mat = jnp.where(strict, kq[:c] * dec, 0.0)
            qk = (kq[c:] * dec).astype(BF16)
            egc = jnp.exp(gcol)
            rhs = jnp.concatenate([v * bcol, kb * egc], axis=1).astype(BF16)
            glast = gcol[c - 1:c, :]
            kd = (k * jnp.exp(glast - gcol)).astype(BF16)
            pre.append((a_mat, qk, rhs, (q * egc).astype(BF16), kd, jnp.exp(glast)))
        invs = _unit_lower_inverses([p[0] for p in pre])
        sols = [_dot(t.astype(BF16), p[2]) for t, p in zip(invs, pre)]
        ws_qs = [_dot(jnp.concatenate([sol[:, hd:].astype(BF16), p[3]], axis=0), st.astype(BF16))
                 for sol, p, st in zip(sols, pre, states)]
        v_news = [(sol[:, :hd] - wq[:c]).astype(BF16) for sol, wq in zip(sols, ws_qs)]
        outs = [wq[c:] + _dot(p[1], vn) for wq, p, vn in zip(ws_qs, pre, v_news)]
        new_states = [st * p[5] + _dot_tn(p[4], vn) for st, p, vn in zip(states, pre, v_news)]
        for h, o in zip(hs, outs):
            on = o * lax.rsqrt(jnp.mean(o * o, axis=-1, keepdims=True) + RMS_EPS) * nw_ref[...]
            o_ref[h] = (on * _silu(gate_ref[h])).astype(o_ref.dtype)
        for h, state in zip(hs, new_states):
            s_ref[0, h] = state
        return carry

    lax.fori_loop(0, heads // group, group_body, 0)

    for j in range(3 * heads):
        xbuf_ref[j, 0:CONV_PAD, :] = xbuf_ref[j, tile:tile + CONV_PAD, :]


def _delta_prompt(z3, ab, conv_w3, a_log, dt_bias, norm_w, batch, seq, tile=128, group=8):
    m = z3.shape[1]
    heads = C_HEADS
    nt = seq // tile

    def spec(cb0):
        return pl.BlockSpec((heads, tile, LANE), lambda b, t: (cb0 // heads, b * nt + t, 0))

    def full(a):
        return pl.BlockSpec(a.shape, lambda b, t: (0,) * a.ndim)

    small = (conv_w3, a_log.reshape(1, heads), dt_bias.reshape(1, heads), norm_w.reshape(1, LANE))
    return pl.pallas_call(
        functools.partial(_delta_prompt_body, group=group),
        out_shape=(jax.ShapeDtypeStruct((heads, m, LANE), BF16),
                   jax.ShapeDtypeStruct((batch, heads, LANE, LANE), F32)),
        grid=(batch, nt),
        in_specs=[spec(CB_QC), spec(CB_KC), spec(CB_VC), spec(CB_GC),
                  pl.BlockSpec((tile, 2 * heads), lambda b, t: (b * nt + t, 0))] + [full(a) for a in small],
        out_specs=(pl.BlockSpec((heads, tile, LANE), lambda b, t: (0, b * nt + t, 0)),
                   pl.BlockSpec((1, heads, LANE, LANE), lambda b, t: (b, 0, 0, 0))),
        scratch_shapes=[pltpu.VMEM((3 * heads, CONV_PAD + tile, LANE), F32),
                        pltpu.VMEM((heads, tile), F32)],
        compiler_params=_params("parallel", "arbitrary"),
        name="delta_prompt",
    )(z3, z3, z3, z3, ab, *small)


def _attn_sample_body(z_ref, *refs, npat):
    kv_refs, o_ref = refs[:2 * npat], refs[2 * npat]
    b = pl.program_id(0)
    heads = A_HEADS
    n = A_BLOCK

    @pl.when(b == 0)
    def _():
        o_ref[...] = jnp.zeros_like(o_ref)

    row = pl.ds(b, 1)

    def head_rows(cb0):
        return jnp.concatenate([z_ref[cb0 + h, row, :] for h in range(heads)], axis=0)

    q = head_rows(CB_QA).astype(BF16)
    kn = head_rows(CB_KA).astype(BF16).astype(F32)
    vn = head_rows(CB_VA).astype(BF16).astype(F32)
    gate = head_rows(CB_GA)
    scale = LANE ** -0.5
    qpad = jnp.concatenate([q, jnp.zeros((MXU_ROWS - heads, LANE), BF16)], axis=0)
    s_new = jnp.sum(q.astype(F32) * kn, axis=1, keepdims=True) * scale
    col = lax.broadcasted_iota(jnp.int32, (heads, n * heads), 1)
    own = (col & (heads - 1)) == lax.broadcasted_iota(jnp.int32, (heads, n * heads), 0)
    parts = []
    for i in range(npat):
        kmat = kv_refs[2 * i][...].reshape(n * heads, LANE).astype(BF16)
        vmat = kv_refs[2 * i + 1][...].reshape(n * heads, LANE).astype(BF16)
        s = jnp.where(own, _dot_nt(qpad, kmat)[:heads] * scale, -jnp.inf)
        m = jnp.maximum(jnp.max(s, axis=1, keepdims=True), s_new)
        p = jnp.exp(s - m)
        pn = jnp.exp(s_new - m)
        l = jnp.sum(p, axis=1, keepdims=True) + pn
        ppad = jnp.concatenate([p, jnp.zeros((MXU_ROWS - heads, n * heads), F32)], axis=0).astype(BF16)
        o = _dot(ppad, vmat)[:heads] + pn.astype(BF16).astype(F32) * vn
        parts.append((o, m, l))
    m_all = functools.reduce(jnp.maximum, [m for _, m, _ in parts])
    num = jnp.zeros((heads, LANE), F32)
    den = jnp.zeros((heads, 1), F32)
    for o, m, l in parts:
        wgt = jnp.exp(m - m_all)
        num = num + o * wgt
        den = den + l * wgt
    out = num / den * _silu(gate)
    for h in range(heads):
        o_ref[h, row, :] = out[h:h + 1]


def _attn_sample(z3s, cache_k, cache_v, layer):
    depth, nb, nbuf, heads, hd = cache_k.shape
    rows = z3s.shape[1]
    n = A_BLOCK
    assert heads == A_HEADS and hd == LANE
    assert all(nbuf % window == 0 for window, _ in A_PATTERNS), "cached window shorter than a pattern"
    views, specs = [], []
    for window, d in A_PATTERNS:
        last = nbuf // d // n - 1
        spec = pl.BlockSpec((None, None, n, None, heads, hd), lambda b, last=last: (layer, b, last, 0, 0, 0))
        for cache in (cache_k, cache_v):
            views.append(cache.reshape(depth, nb, nbuf // d, d, heads, hd))
            specs.append(spec)
    return pl.pallas_call(
        functools.partial(_attn_sample_body, npat=len(A_PATTERNS)),
        out_shape=jax.ShapeDtypeStruct((heads, rows, LANE), F32),
        grid=(nb,),
        in_specs=[pl.BlockSpec(z3s.shape, lambda b: (0, 0, 0))] + specs,
        out_specs=pl.BlockSpec((heads, rows, LANE), lambda b: (0, 0, 0)),
        compiler_params=_params("arbitrary"),
        name="attn_sample",
    )(z3s, *views)


def _state_sample_body(z_ref, ab_ref, pool_ref, conv_ref, st_ref, pw_ref, psc_ref, cw_ref, alog_ref, dtb_ref,
                       nw_ref, mb_ref, mc_ref, so_ref):
    b = pl.program_id(0)
    heads = C_HEADS
    hd = LANE

    @pl.when(b == 0)
    def _():
        mb_ref[...] = jnp.zeros_like(mb_ref)
        mc_ref[...] = jnp.zeros_like(mc_ref)

    row = pl.ds(b, 1)

    gw = pw_ref.shape[1]
    cpg = gw // LANE
    for gi, width in enumerate(B_POOLS):
        ys = []
        for j in range(cpg):
            cb = gi * cpg + j
            un = z_ref[CB_UB + cb, row, :]
            prev = pool_ref[0, B_BUF - (width - 1):B_BUF, cb * LANE:(cb + 1) * LANE]
            mean = (jnp.sum(prev, axis=0, keepdims=True) + un) / float(width)
            ys.append(mean - un)
        y = jnp.broadcast_to(jnp.concatenate(ys, axis=1), (MXU_ROWS, gw)).astype(BF16)
        out = _dot(y, pw_ref[gi])[0:1] * psc_ref[:, gi * gw:(gi + 1) * gw]
        for j in range(cpg):
            cb = gi * cpg + j
            mb_ref[cb, row, :] = out[:, j * LANE:(j + 1) * LANE] * _silu(z_ref[CB_GB + cb, row, :])

    arow = ab_ref[row, :]
    lane_ab = lax.broadcasted_iota(jnp.int32, (1, 2 * heads), 1)
    lane_h = lax.broadcasted_iota(jnp.int32, (1, heads), 1)
    eye = lax.broadcasted_iota(jnp.int32, (hd, hd), 0) == lax.broadcasted_iota(jnp.int32, (hd, hd), 1)

    def pick(vec, lanes, idx):
        return jnp.sum(jnp.where(lanes == idx, vec, 0.0), axis=1, keepdims=True)

    def column(x):
        return jnp.sum(jnp.where(eye, jnp.broadcast_to(x, (hd, hd)), 0.0), axis=1, keepdims=True)

    def conv(j):
        st = conv_ref[0, j]
        w = cw_ref[j]
        y = st[0:1] * w[0:1]
        y = y + st[1:2] * w[1:2]
        y = y + st[2:3] * w[2:3]
        y = y + z_ref[CB_QC + j, row, :] * w[3:4]
        return _silu(y)

    def l2n(x):
        return x * lax.rsqrt(jnp.sum(x * x, axis=-1, keepdims=True) + RMS_EPS)

    def head_body(h, carry):
        q = l2n(conv(h)) * (hd ** -0.5)
        k = l2n(conv(heads + h))
        v = conv(2 * heads + h)
        g = -jnp.exp(pick(alog_ref[...], lane_h, h)) * _softplus(pick(arow, lane_ab, h) + pick(dtb_ref[...], lane_h, h))
        beta = _sigmoid(pick(arow, lane_ab, heads + h))
        state = st_ref[0, h] * jnp.exp(g)
        kcol = column(k)
        delta = (v - jnp.sum(kcol * state, axis=0, keepdims=True)) * beta
        state = state + kcol * delta
        o = jnp.sum(column(q) * state, axis=0, keepdims=True)
        on = o * lax.rsqrt(jnp.mean(o * o, axis=-1, keepdims=True) + RMS_EPS) * nw_ref[...]
        mc_ref[h, row, :] = on * _silu(z_ref[CB_GC + h, row, :])
        so_ref[0, h] = state
        return carry

    lax.fori_loop(0, heads, head_body, 0)


def _state_sample(z3s, ab_s, state_pool, state_conv3, state_delta, pool_w, pool_scale, conv_w3, a_log, dt_bias,
                  norm_w):
    nb = state_pool.shape[0]
    rows = z3s.shape[1]
    heads = C_HEADS
    bw = state_pool.shape[2]

    def full(a):
        return pl.BlockSpec(a.shape, lambda b: (0,) * a.ndim)

    def per_b(a):
        return pl.BlockSpec((1,) + a.shape[1:], lambda b: (b,) + (0,) * (a.ndim - 1))

    small = (pool_w.astype(BF16), pool_scale.reshape(1, bw), conv_w3, a_log.reshape(1, heads),
             dt_bias.reshape(1, heads), norm_w.reshape(1, LANE))
    return pl.pallas_call(
        _state_sample_body,
        out_shape=(jax.ShapeDtypeStruct((bw // LANE, rows, LANE), F32),
                   jax.ShapeDtypeStruct((heads, rows, LANE), F32),
                   jax.ShapeDtypeStruct(state_delta.shape, F32)),
        grid=(nb,),
        in_specs=[full(z3s), full(ab_s), per_b(state_pool), per_b(state_conv3), per_b(state_delta)]
        + [full(a) for a in small],
        out_specs=(pl.BlockSpec((bw // LANE, rows, LANE), lambda b: (0, 0, 0)),
                   pl.BlockSpec((heads, rows, LANE), lambda b: (0, 0, 0)),
                   per_b(state_delta)),
        compiler_params=_params("arbitrary"),
        name="state_sample",
    )(z3s, ab_s, state_pool, state_conv3, state_delta, *small)


def _cols(z4, cb0, ncb, r0, r1):
    blk = z4[cb0:cb0 + ncb, :, r0:r1]
    return jnp.transpose(blk, (1, 2, 0, 3)).reshape(blk.shape[1], r1 - r0, ncb * LANE)


def kernel(x_prompt, x_sample, cache_win_k, cache_win_v, state_pool, state_conv, state_delta, norm_w, w_in,
           conv_w, a_log, dt_bias, delta_norm_w, pool_w, pool_scale, w_out, final_norm_w):
    batch, seq, d_model = x_prompt.shape
    nb = x_sample.shape[0]
    depth = w_in.shape[0]
    heads = C_HEADS
    m = batch * seq
    rows_s = 16
    nbuf = cache_win_k.shape[2]
    pool_cb = state_pool.shape[-1] // LANE

    hp = x_prompt.reshape(m, d_model)
    hs = jnp.zeros((rows_s, d_model), F32).at[:nb].set(x_sample.reshape(nb, d_model))

    assert nbuf == A_PATTERNS[-1][0], "the window cache is expected full: one row in, one row out"

    outs_p = [[] for _ in range(5)]
    outs_s = [[] for _ in range(5)]
    new_k, new_v = [], []
    w_in_b = w_in.astype(BF16)
    w_out_b = w_out.astype(BF16)
    for l in range(depth):
        w_ab = w_in[l, :, N_MAIN:].astype(BF16)
        conv_w3 = jnp.transpose(conv_w[l].reshape(C_CONV, 3 * heads, LANE), (1, 0, 2))

        h = _rmsnorm(hp, norm_w[l], BF16, 256)
        z3, ab, k_nat, v_nat = _inproj(h, w_in_b, l, w_ab, 1024, 512)
        mix_a = _attn_prompt(z3, k_nat, v_nat, batch, seq)
        mix_b = _pool_prompt(z3, pool_w[l], pool_scale[l], batch, seq)
        mix_c, s_new = _delta_prompt(z3, ab, conv_w3, a_log[l], dt_bias[l], delta_norm_w[l], batch, seq)
        hp = _outproj(mix_a, mix_b, mix_c, w_out_b, l, hp, 1024, 512)

        z4 = z3.reshape(CB_TOTAL, batch, seq, LANE)
        keep_p = min(seq, A_PATTERNS[-1][0])
        outs_p[0].append(k_nat.reshape(batch, seq, A_HEADS, LANE)[:, seq - keep_p:])
        outs_p[1].append(v_nat.reshape(batch, seq, A_HEADS, LANE)[:, seq - keep_p:])
        outs_p[2].append(_cols(z4, CB_UB, pool_cb, seq - B_BUF, seq))
        outs_p[3].append(_cols(z4, CB_QC, 3 * heads, seq - (C_CONV - 1), seq))
        outs_p[4].append(s_new)

        h_s = _rmsnorm(hs, norm_w[l], BF16, rows_s)
        z3s, ab_s, _, _ = _inproj(h_s, w_in_b, l, w_ab, rows_s, 512)
        mix_as = _attn_sample(z3s, cache_win_k, cache_win_v, l)
        conv3 = jnp.transpose(state_conv[l].reshape(nb, C_CONV - 1, 3 * heads, LANE), (0, 2, 1, 3))
        mix_bs, mix_cs, st_new = _state_sample(z3s, ab_s, state_pool[l], conv3, state_delta[l], pool_w[l],
                                               pool_scale[l], conv_w3, a_log[l], dt_bias[l], delta_norm_w[l])
        hs = _outproj(mix_as, mix_bs, mix_cs, w_out_b, l, hs, rows_s, 512)

        z4s = z3s[:, :nb].reshape(CB_TOTAL, nb, 1, LANE)
        new_k.append(_cols(z4s, CB_KA, A_HEADS, 0, 1).reshape(nb, 1, A_HEADS, LANE))
        new_v.append(_cols(z4s, CB_VA, A_HEADS, 0, 1).reshape(nb, 1, A_HEADS, LANE))
        outs_s[2].append(jnp.concatenate([state_pool[l], _cols(z4s, CB_UB, pool_cb, 0, 1)], axis=1)[:, 1:])
        outs_s[3].append(jnp.concatenate([state_conv[l], _cols(z4s, CB_QC, 3 * heads, 0, 1)], axis=1)[:, 1:])
        outs_s[4].append(st_new)

    y_prompt = _rmsnorm(hp, final_norm_w, F32, 256).reshape(batch, seq, d_model)
    y_sample = _rmsnorm(hs, final_norm_w, F32, rows_s)[:nb].reshape(nb, 1, d_model)
    stack = lambda xs: jnp.stack(xs, axis=0)
    win_k = jnp.concatenate([cache_win_k[:, :, 1:], stack(new_k)], axis=2)
    win_v = jnp.concatenate([cache_win_v[:, :, 1:], stack(new_v)], axis=2)
    return ((y_prompt, y_sample) + tuple(stack(o) for o in outs_p) + (win_k, win_v)
            + tuple(stack(o) for o in outs_s[2:]))
```

```python
import functools

import jax
import jax.numpy as jnp
from jax import lax
from jax.experimental import pallas as pl
from jax.experimental.pallas import tpu as pltpu

F32 = jnp.float32
BF16 = jnp.bfloat16

LANE = 128
MXU_ROWS = 16
VMEM_LIMIT = 56 * 2**20
RMS_EPS = 1e-6

A_HEADS = 8
A_PATTERNS = ((128, 1), (512, 4), (2048, 16))
A_BLOCK = 128
A_UNROLL = 4
B_POOLS = (2, 4, 8, 16)
B_BUF = 15
C_HEADS = 16
C_CONV = 4
CACHE_COPY_ROWS = 256
CONV_PAD = 8

CB_QA, CB_KA, CB_VA, CB_GA = 0, 8, 16, 24
CB_UB, CB_GB = 32, 40
CB_QC, CB_KC, CB_VC, CB_GC = 48, 64, 80, 96
CB_TOTAL = 112
N_MAIN = CB_TOTAL * LANE

HIGHEST = lax.Precision.HIGHEST


def _params(*sem):
    return pltpu.CompilerParams(dimension_semantics=sem, vmem_limit_bytes=VMEM_LIMIT)


def _sigmoid(x):
    return 1.0 / (1.0 + jnp.exp(-x))


def _silu(x):
    return x * _sigmoid(x)


def _softplus(x):
    return jnp.maximum(x, 0.0) + jnp.log(1.0 + jnp.exp(-jnp.abs(x)))


def _dot(a, b, precision=None):
    return jnp.dot(a, b, preferred_element_type=F32, precision=precision)


def _dot_nt(a, b, precision=None):
    return lax.dot_general(a, b, (((1,), (1,)), ((), ())), preferred_element_type=F32, precision=precision)


def _dot_tn(a, b, precision=None):
    return lax.dot_general(a, b, (((0,), (0,)), ((), ())), preferred_element_type=F32, precision=precision)


def _rms_body(x_ref, g_ref, o_ref):
    x = x_ref[...]
    ms = jnp.mean(x * x, axis=-1, keepdims=True)
    o_ref[...] = (x * lax.rsqrt(ms + RMS_EPS) * g_ref[...]).astype(o_ref.dtype)


def _rmsnorm(x, gain, out_dtype, tm):
    m, d = x.shape
    return pl.pallas_call(
        _rms_body,
        out_shape=jax.ShapeDtypeStruct((m, d), out_dtype),
        grid=(m // tm,),
        in_specs=[pl.BlockSpec((tm, d), lambda i: (i, 0)), pl.BlockSpec((1, d), lambda i: (0, 0))],
        out_specs=pl.BlockSpec((tm, d), lambda i: (i, 0)),
        compiler_params=_params("parallel"),
        name="rmsnorm",
    )(x, gain.reshape(1, d))


def _inproj_body(h_ref, w_ref, wab_ref, z_ref, ab_ref, k_ref, v_ref, *, k_tiles, v_tiles):
    j = pl.program_id(1)
    h = h_ref[...]
    acc = _dot(h, w_ref[...].astype(BF16))
    for c in range(acc.shape[1] // LANE):
        z_ref[c] = acc[:, c * LANE:(c + 1) * LANE]

    @pl.when(j == 0)
    def _():
        ab_ref[...] = _dot(h, wab_ref[...])

    @pl.when((j >= k_tiles[0]) & (j < k_tiles[1]))
    def _():
        k_ref[...] = acc

    @pl.when((j >= v_tiles[0]) & (j < v_tiles[1]))
    def _():
        v_ref[...] = acc


def _inproj(h, w_all, layer, w_ab, tm, tn):
    m, d = h.shape
    nab = w_ab.shape[1]
    a_width = A_HEADS * LANE
    k_tiles = (CB_KA * LANE // tn, CB_VA * LANE // tn)
    v_tiles = (CB_VA * LANE // tn, CB_GA * LANE // tn)

    def natural(tiles):
        return pl.BlockSpec((tm, tn), lambda i, j: (i, jnp.clip(j - tiles[0], 0, tiles[1] - tiles[0] - 1)))

    return pl.pallas_call(
        functools.partial(_inproj_body, k_tiles=k_tiles, v_tiles=v_tiles),
        out_shape=(jax.ShapeDtypeStruct((CB_TOTAL, m, LANE), F32),
                   jax.ShapeDtypeStruct((m, nab), F32),
                   jax.ShapeDtypeStruct((m, a_width), F32),
                   jax.ShapeDtypeStruct((m, a_width), F32)),
        grid=(m // tm, N_MAIN // tn),
        in_specs=[pl.BlockSpec((tm, d), lambda i, j: (i, 0)),
                  pl.BlockSpec((None, d, tn), lambda i, j: (layer, 0, j)),
                  pl.BlockSpec((d, nab), lambda i, j: (0, 0))],
        out_specs=(pl.BlockSpec((tn // LANE, tm, LANE), lambda i, j: (j, i, 0)),
                   pl.BlockSpec((tm, nab), lambda i, j: (i, 0)),
                   natural(k_tiles), natural(v_tiles)),
        compiler_params=_params("parallel", "arbitrary"),
        name="inproj",
    )(h, w_all, w_ab)


def _outproj_body(ma_ref, mb_ref, mc_ref, w_ref, x_ref, o_ref, lhs_ref):
    @pl.when(pl.program_id(1) == 0)
    def _():
        c0 = 0
        for ref in (ma_ref, mb_ref, mc_ref):
            for c in range(ref.shape[0]):
                lhs_ref[:, (c0 + c) * LANE:(c0 + c + 1) * LANE] = ref[c].astype(BF16)
            c0 += ref.shape[0]

    o_ref[...] = x_ref[...] + _dot(lhs_ref[...], w_ref[...].astype(BF16))


def _outproj(mix_a, mix_b, mix_c, w_all, layer, x, tm, tn):
    m, d = x.shape
    k = w_all.shape[1]

    def mix_spec(a):
        return pl.BlockSpec((a.shape[0], tm, LANE), lambda i, j: (0, i, 0))

    return pl.pallas_call(
        _outproj_body,
        out_shape=jax.ShapeDtypeStruct((m, d), F32),
        grid=(m // tm, d // tn),
        in_specs=[mix_spec(mix_a), mix_spec(mix_b), mix_spec(mix_c),
                  pl.BlockSpec((None, k, tn), lambda i, j: (layer, 0, j)),
                  pl.BlockSpec((tm, tn), lambda i, j: (i, j))],
        out_specs=pl.BlockSpec((tm, tn), lambda i, j: (i, j)),
        scratch_shapes=[pltpu.VMEM((tm, k), BF16)],
        compiler_params=_params("parallel", "arbitrary"),
        name="outproj",
    )(mix_a, mix_b, mix_c, w_all, x)


def _attn_prompt_body(q_ref, k_ref, v_ref, g_ref, o_ref, acc_ref, m_ref, l_ref, *, seq):
    n = A_BLOCK
    scale = LANE ** -0.5
    qi = lax.broadcasted_iota(jnp.int32, (n, n), 0)
    kj = lax.broadcasted_iota(jnp.int32, (n, n), 1)
    cur_mask = kj <= qi
    prev_mask = kj >= qi

    def rows(t0, d):
        return pl.ds(t0, n) if d == 1 else pl.ds(t0, n, stride=d)

    blocks = []
    for pi, (window, d) in enumerate(A_PATTERNS):
        for r in range(d):
            for blk in range(seq // window):
                t0 = blk * window + r
                blocks.append((pi, rows(t0, d), rows(t0 - window, d) if blk > 0 else None))
    waves = [blocks[i:i + A_UNROLL] for i in range(0, len(blocks), A_UNROLL)]

    def scores(block):
        _, cur, prev = block
        q = q_ref[0, cur, :].astype(BF16)
        sc = [_dot_nt(q, k_ref[cur, :].astype(BF16)), v_ref[cur, :].astype(BF16)]
        if prev is not None:
            sc += [_dot_nt(q, k_ref[prev, :].astype(BF16)), v_ref[prev, :].astype(BF16)]
        return sc

    def softmax(sc):
        s_c = jnp.where(cur_mask, sc[0] * scale, -jnp.inf)
        if len(sc) == 2:
            m = jnp.max(s_c, axis=1, keepdims=True)
            p_c = jnp.exp(s_c - m)
            return m, jnp.sum(p_c, axis=1, keepdims=True), p_c.astype(BF16)
        s_p = jnp.where(prev_mask, sc[2] * scale, -jnp.inf)
        m = jnp.max(jnp.maximum(s_c, s_p), axis=1, keepdims=True)
        p_c = jnp.exp(s_c - m)
        p_p = jnp.exp(s_p - m)
        return m, jnp.sum(p_c + p_p, axis=1, keepdims=True), p_c.astype(BF16), p_p.astype(BF16)

    def values(sc, pr):
        o = _dot(pr[2], sc[1])
        if len(sc) > 2:
            o = o + _dot(pr[3], sc[3])
        return o

    def accumulate(block, pr, o):
        pi, cur, _ = block
        mb = jnp.broadcast_to(pr[0], (n, LANE))
        lb = jnp.broadcast_to(pr[1], (n, LANE))
        if pi == 0:
            acc_ref[cur, :] = o
            m_ref[cur, :] = mb
            l_ref[cur, :] = lb
        else:
            m_old = m_ref[cur, :]
            m_new = jnp.maximum(m_old, mb)
            w_old = jnp.exp(m_old - m_new)
            w_cur = jnp.exp(mb - m_new)
            acc_ref[cur, :] = acc_ref[cur, :] * w_old + o * w_cur
            l_ref[cur, :] = l_ref[cur, :] * w_old + lb * w_cur
            m_ref[cur, :] = m_new

    sc_of, pr_of, o_of = {}, {}, {}
    for step in range(len(waves) + 3):
        if step < len(waves):
            sc_of[step] = [scores(blk) for blk in waves[step]]
        w = step - 1
        if 0 <= w < len(waves):
            pr_of[w] = [softmax(sc) for sc in sc_of[w]]
        w = step - 2
        if 0 <= w < len(waves):
            o_of[w] = [values(sc, pr) for sc, pr in zip(sc_of.pop(w), pr_of[w])]
        w = step - 3
        if 0 <= w < len(waves):
            for blk, pr, o in zip(waves[w], pr_of.pop(w), o_of.pop(w)):
                accumulate(blk, pr, o)

    rows_out = 256

    def finish(i, carry):
        sl = pl.ds(pl.multiple_of(i * rows_out, rows_out), rows_out)
        out = acc_ref[sl, :] / l_ref[sl, :] * _silu(g_ref[0, sl, :])
        o_ref[0, sl, :] = out.astype(o_ref.dtype)
        return carry

    lax.fori_loop(0, seq // rows_out, finish, 0)


def _attn_prompt(z3, k_nat, v_nat, batch, seq):
    m = z3.shape[1]

    def spec(cb0):
        return pl.BlockSpec((1, seq, LANE), lambda b, h: (cb0 + h, b, 0))

    nat = pl.BlockSpec((seq, LANE), lambda b, h: (b, h))
    return pl.pallas_call(
        functools.partial(_attn_prompt_body, seq=seq),
        out_shape=jax.ShapeDtypeStruct((A_HEADS, m, LANE), BF16),
        grid=(batch, A_HEADS),
        in_specs=[spec(CB_QA), nat, nat, spec(CB_GA)],
        out_specs=pl.BlockSpec((1, seq, LANE), lambda b, h: (h, b, 0)),
        scratch_shapes=[pltpu.VMEM((seq, LANE), F32)] * 3,
        compiler_params=_params("parallel", "parallel"),
        name="attn_prompt",
    )(z3, k_nat, v_nat, z3)


def _pool_prompt_body(u_ref, g_ref, w_ref, sc_ref, o_ref, *, seq):
    grp = pl.program_id(1)
    row = lax.broadcasted_iota(jnp.int32, (seq, 1), 0)
    for gi, width in enumerate(B_POOLS):
        @pl.when(grp == gi)
        def _(width=width):
            u = jnp.concatenate([u_ref[0], u_ref[1]], axis=1)
            s = u
            step = 1
            while step < width:
                s = s + jnp.where(row >= step, pltpu.roll(s, step, axis=0), 0.0)
                step *= 2
            cnt = jnp.minimum(row + 1, width).astype(F32)
            y = (s / cnt - u).astype(BF16)
            out = _dot(y, w_ref[0]) * sc_ref[0]
            gate = jnp.concatenate([g_ref[0], g_ref[1]], axis=1)
            out = (out * _silu(gate)).astype(o_ref.dtype)
            o_ref[0] = out[:, :LANE]
            o_ref[1] = out[:, LANE:]


def _pool_prompt(z3, pool_w, pool_scale, batch, seq):
    m = z3.shape[1]
    ngrp = len(B_POOLS)
    gw = pool_w.shape[1]
    cpg = gw // LANE

    def spec(cb0):
        return pl.BlockSpec((cpg, seq, LANE), lambda b, g: (cb0 // cpg + g, b, 0))

    return pl.pallas_call(
        functools.partial(_pool_prompt_body, seq=seq),
        out_shape=jax.ShapeDtypeStruct((ngrp * cpg, m, LANE), BF16),
        grid=(batch, ngrp),
        in_specs=[spec(CB_UB), spec(CB_GB),
                  pl.BlockSpec((1, gw, gw), lambda b, g: (g, 0, 0)),
                  pl.BlockSpec((1, 1, gw), lambda b, g: (g, 0, 0))],
        out_specs=pl.BlockSpec((cpg, seq, LANE), lambda b, g: (g, b, 0)),
        compiler_params=_params("parallel", "parallel"),
        name="pool_prompt",
    )(z3, z3, pool_w.astype(BF16), pool_scale.reshape(ngrp, 1, gw))


def _unit_lower_inverses(mats):
    c = mats[0].shape[0]
    eye = (lax.broadcasted_iota(jnp.int32, (c, c), 0) == lax.broadcasted_iota(jnp.int32, (c, c), 1)).astype(F32)
    xs = [eye - a for a in mats]
    pbs = [(-a).astype(BF16) for a in mats]
    k = 2
    while k < c:
        pbs = [_dot(pb, pb).astype(BF16) for pb in pbs]
        xs = [x + _dot(x.astype(BF16), pb) for x, pb in zip(xs, pbs)]
        k *= 2
    return xs


def _delta_prompt_body(q_ref, k_ref, v_ref, gate_ref, ab_ref, cw_ref, alog_ref, dtb_ref, nw_ref,
                       o_ref, s_ref, xbuf_ref, gct_ref, *, group):
    heads = C_HEADS
    c = q_ref.shape[1]
    tile = c
    hd = LANE

    @pl.when(pl.program_id(1) == 0)
    def _():
        s_ref[...] = jnp.zeros_like(s_ref)
        xbuf_ref[:, 0:CONV_PAD, :] = jnp.zeros((3 * heads, CONV_PAD, hd), F32)

    for j in range(heads):
        xbuf_ref[j, CONV_PAD:CONV_PAD + tile, :] = q_ref[j]
        xbuf_ref[heads + j, CONV_PAD:CONV_PAD + tile, :] = k_ref[j]
        xbuf_ref[2 * heads + j, CONV_PAD:CONV_PAD + tile, :] = v_ref[j]

    g = -jnp.exp(alog_ref[...]) * _softplus(ab_ref[:, 0:heads] + dtb_ref[...])
    beta = _sigmoid(ab_ref[:, heads:2 * heads])

    ri = lax.broadcasted_iota(jnp.int32, (c, c), 0)
    ci = lax.broadcasted_iota(jnp.int32, (c, c), 1)
    causal = ri >= ci
    strict = ri > ci
    gcum = _dot(causal.astype(F32), g, HIGHEST)
    eye_h = (lax.broadcasted_iota(jnp.int32, (heads, heads), 0)
             == lax.broadcasted_iota(jnp.int32, (heads, heads), 1)).astype(F32)
    gct_ref[...] = _dot_nt(eye_h, gcum, HIGHEST)
    lane_h = lax.broadcasted_iota(jnp.int32, (1, heads), 1)

    def conv(j):
        w = cw_ref[j]
        first = CONV_PAD - (C_CONV - 1)
        y = xbuf_ref[j, pl.ds(first, tile), :] * w[0:1]
        for i in range(1, C_CONV):
            y = y + xbuf_ref[j, pl.ds(first + i, tile), :] * w[i:i + 1]
        return _silu(y)

    def l2n(x):
        return x * lax.rsqrt(jnp.sum(x * x, axis=-1, keepdims=True) + RMS_EPS)

    def group_body(hg, carry):
        hs = [hg * group + i for i in range(group)]
        states = [s_ref[0, h] for h in hs]
        pre = []
        for h in hs:
            onehot = lane_h == h
            q = l2n(conv(h)) * (hd ** -0.5)
            k = l2n(conv(heads + h))
            v = conv(2 * heads + h)
            gcol = jnp.sum(jnp.where(onehot, gcum, 0.0), axis=1, keepdims=True)
            bcol = jnp.sum(jnp.where(onehot, beta, 0.0), axis=1, keepdims=True)
            grow = gct_ref[pl.ds(h, 1), :]
            dec = jnp.exp(jnp.where(causal, gcol - grow, -jnp.inf))
            kb = k * bcol
            kq = _dot_nt(jnp.concatenate([kb, q], axis=0).astype(BF16), k.astype(BF16))
            a_mat = jnp.where(strict, kq[:c] * dec, 0.0)
            qk = (kq[c:] * dec).astype(BF16)
            egc = jnp.exp(gcol)
            rhs = jnp.concatenate([v * bcol, kb * egc], axis=1).astype(BF16)
            glast = gcol[c - 1:c, :]
            kd = (k * jnp.exp(glast - gcol)).astype(BF16)
            pre.append((a_mat, qk, rhs, (q * egc).astype(BF16), kd, jnp.exp(glast)))
        invs = _unit_lower_inverses([p[0] for p in pre])
        sols = [_dot(t.astype(BF16), p[2]) for t, p in zip(invs, pre)]
        ws_qs = [_dot(jnp.concatenate([sol[:, hd:].astype(BF16), p[3]], axis=0), st.astype(BF16))
                 for sol, p, st in zip(sols, pre, states)]
        v_news = [(sol[:, :hd] - wq[:c]).astype(BF16) for sol, wq in zip(sols, ws_qs)]
        outs = [wq[c:] + _dot(p[1], vn) for wq, p, vn in zip(ws_qs, pre, v_news)]
        new_states = [st * p[5] + _dot_tn(p[4], vn) for st, p, vn in zip(states, pre, v_news)]
        for h, o in zip(hs, outs):
            on = o * lax.rsqrt(jnp.mean(o * o, axis=-1, keepdims=True) + RMS_EPS) * nw_ref[...]
            o_ref[h] = (on * _silu(gate_ref[h])).astype(o_ref.dtype)
        for h, state in zip(hs, new_states):
            s_ref[0, h] = state
        return carry

    lax.fori_loop(0, heads // group, group_body, 0)

    for j in range(3 * heads):
        xbuf_ref[j, 0:CONV_PAD, :] = xbuf_ref[j, tile:tile + CONV_PAD, :]


def _delta_prompt(z3, ab, conv_w3, a_log, dt_bias, norm_w, batch, seq, tile=128, group=8):
    m = z3.shape[1]
    heads = C_HEADS
    nt = seq // tile

    def spec(cb0):
        return pl.BlockSpec((heads, tile, LANE), lambda b, t: (cb0 // heads, b * nt + t, 0))

    def full(a):
        return pl.BlockSpec(a.shape, lambda b, t: (0,) * a.ndim)

    small = (conv_w3, a_log.reshape(1, heads), dt_bias.reshape(1, heads), norm_w.reshape(1, LANE))
    return pl.pallas_call(
        functools.partial(_delta_prompt_body, group=group),
        out_shape=(jax.ShapeDtypeStruct((heads, m, LANE), BF16),
                   jax.ShapeDtypeStruct((batch, heads, LANE, LANE), F32)),
        grid=(batch, nt),
        in_specs=[spec(CB_QC), spec(CB_KC), spec(CB_VC), spec(CB_GC),
                  pl.BlockSpec((tile, 2 * heads), lambda b, t: (b * nt + t, 0))] + [full(a) for a in small],
        out_specs=(pl.BlockSpec((heads, tile, LANE), lambda b, t: (0, b * nt + t, 0)),
                   pl.BlockSpec((1, heads, LANE, LANE), lambda b, t: (b, 0, 0, 0))),
        scratch_shapes=[pltpu.VMEM((3 * heads, CONV_PAD + tile, LANE), F32),
                        pltpu.VMEM((heads, tile), F32)],
        compiler_params=_params("parallel", "arbitrary"),
        name="delta_prompt",
    )(z3, z3, z3, z3, ab, *small)


def _attn_sample_body(z_ref, *refs, npat):
    kv_refs, o_ref = refs[:2 * npat], refs[2 * npat]
    b = pl.program_id(0)
    heads = A_HEADS
    n = A_BLOCK

    @pl.when(b == 0)
    def _():
        o_ref[...] = jnp.zeros_like(o_ref)

    row = pl.ds(b, 1)

    def head_rows(cb0):
        return jnp.concatenate([z_ref[cb0 + h, row, :] for h in range(heads)], axis=0)

    q = head_rows(CB_QA).astype(BF16)
    kn = head_rows(CB_KA).astype(BF16).astype(F32)
    vn = head_rows(CB_VA).astype(BF16).astype(F32)
    gate = head_rows(CB_GA)
    scale = LANE ** -0.5
    qpad = jnp.concatenate([q, jnp.zeros((MXU_ROWS - heads, LANE), BF16)], axis=0)
    s_new = jnp.sum(q.astype(F32) * kn, axis=1, keepdims=True) * scale
    col = lax.broadcasted_iota(jnp.int32, (heads, n * heads), 1)
    own = (col & (heads - 1)) == lax.broadcasted_iota(jnp.int32, (heads, n * heads), 0)
    parts = []
    for i in range(npat):
        kmat = kv_refs[2 * i][...].reshape(n * heads, LANE).astype(BF16)
        vmat = kv_refs[2 * i + 1][...].reshape(n * heads, LANE).astype(BF16)
        s = jnp.where(own, _dot_nt(qpad, kmat)[:heads] * scale, -jnp.inf)
        m = jnp.maximum(jnp.max(s, axis=1, keepdims=True), s_new)
        p = jnp.exp(s - m)
        pn = jnp.exp(s_new - m)
        l = jnp.sum(p, axis=1, keepdims=True) + pn
        ppad = jnp.concatenate([p, jnp.zeros((MXU_ROWS - heads, n * heads), F32)], axis=0).astype(BF16)
        o = _dot(ppad, vmat)[:heads] + pn.astype(BF16).astype(F32) * vn
        parts.append((o, m, l))
    m_all = functools.reduce(jnp.maximum, [m for _, m, _ in parts])
    num = jnp.zeros((heads, LANE), F32)
    den = jnp.zeros((heads, 1), F32)
    for o, m, l in parts:
        wgt = jnp.exp(m - m_all)
        num = num + o * wgt
        den = den + l * wgt
    out = num / den * _silu(gate)
    for h in range(heads):
        o_ref[h, row, :] = out[h:h + 1]


def _attn_sample(z3s, cache_k, cache_v, layer):
    depth, nb, nbuf, heads, hd = cache_k.shape
    rows = z3s.shape[1]
    n = A_BLOCK
    assert heads == A_HEADS and hd == LANE
    assert all(nbuf % window == 0 for window, _ in A_PATTERNS), "cached window shorter than a pattern"
    views, specs = [], []
    for window, d in A_PATTERNS:
        last = nbuf // d // n - 1
        spec = pl.BlockSpec((None, None, n, None, heads, hd), lambda b, last=last: (layer, b, last, 0, 0, 0))
        for cache in (cache_k, cache_v):
            views.append(cache.reshape(depth, nb, nbuf // d, d, heads, hd))
            specs.append(spec)
    return pl.pallas_call(
        functools.partial(_attn_sample_body, npat=len(A_PATTERNS)),
        out_shape=jax.ShapeDtypeStruct((heads, rows, LANE), F32),
        grid=(nb,),
        in_specs=[pl.BlockSpec(z3s.shape, lambda b: (0, 0, 0))] + specs,
        out_specs=pl.BlockSpec((heads, rows, LANE), lambda b: (0, 0, 0)),
        compiler_params=_params("arbitrary"),
        name="attn_sample",
    )(z3s, *views)


def _state_sample_body(z_ref, ab_ref, pool_ref, conv_ref, st_ref, pw_ref, psc_ref, cw_ref, alog_ref, dtb_ref,
                       nw_ref, mb_ref, mc_ref, so_ref):
    b = pl.program_id(0)
    heads = C_HEADS
    hd = LANE

    @pl.when(b == 0)
    def _():
        mb_ref[...] = jnp.zeros_like(mb_ref)
        mc_ref[...] = jnp.zeros_like(mc_ref)

    row = pl.ds(b, 1)

    gw = pw_ref.shape[1]
    cpg = gw // LANE
    for gi, width in enumerate(B_POOLS):
        ys = []
        for j in range(cpg):
            cb = gi * cpg + j
            un = z_ref[CB_UB + cb, row, :]
            prev = pool_ref[0, B_BUF - (width - 1):B_BUF, cb * LANE:(cb + 1) * LANE]
            mean = (jnp.sum(prev, axis=0, keepdims=True) + un) / float(width)
            ys.append(mean - un)
        y = jnp.broadcast_to(jnp.concatenate(ys, axis=1), (MXU_ROWS, gw)).astype(BF16)
        out = _dot(y, pw_ref[gi])[0:1] * psc_ref[:, gi * gw:(gi + 1) * gw]
        for j in range(cpg):
            cb = gi * cpg + j
            mb_ref[cb, row, :] = out[:, j * LANE:(j + 1) * LANE] * _silu(z_ref[CB_GB + cb, row, :])

    arow = ab_ref[row, :]
    lane_ab = lax.broadcasted_iota(jnp.int32, (1, 2 * heads), 1)
    lane_h = lax.broadcasted_iota(jnp.int32, (1, heads), 1)
    eye = lax.broadcasted_iota(jnp.int32, (hd, hd), 0) == lax.broadcasted_iota(jnp.int32, (hd, hd), 1)

    def pick(vec, lanes, idx):
        return jnp.sum(jnp.where(lanes == idx, vec, 0.0), axis=1, keepdims=True)

    def column(x):
        return jnp.sum(jnp.where(eye, jnp.broadcast_to(x, (hd, hd)), 0.0), axis=1, keepdims=True)

    def conv(j):
        st = conv_ref[0, j]
        w = cw_ref[j]
        y = st[0:1] * w[0:1]
        y = y + st[1:2] * w[1:2]
        y = y + st[2:3] * w[2:3]
        y = y + z_ref[CB_QC + j, row, :] * w[3:4]
        return _silu(y)

    def l2n(x):
        return x * lax.rsqrt(jnp.sum(x * x, axis=-1, keepdims=True) + RMS_EPS)

    def head_body(h, carry):
        q = l2n(conv(h)) * (hd ** -0.5)
        k = l2n(conv(heads + h))
        v = conv(2 * heads + h)
        g = -jnp.exp(pick(alog_ref[...], lane_h, h)) * _softplus(pick(arow, lane_ab, h) + pick(dtb_ref[...], lane_h, h))
        beta = _sigmoid(pick(arow, lane_ab, heads + h))
        state = st_ref[0, h] * jnp.exp(g)
        kcol = column(k)
        delta = (v - jnp.sum(kcol * state, axis=0, keepdims=True)) * beta
        state = state + kcol * delta
        o = jnp.sum(column(q) * state, axis=0, keepdims=True)
        on = o * lax.rsqrt(jnp.mean(o * o, axis=-1, keepdims=True) + RMS_EPS) * nw_ref[...]
        mc_ref[h, row, :] = on * _silu(z_ref[CB_GC + h, row, :])
        so_ref[0, h] = state
        return carry

    lax.fori_loop(0, heads, head_body, 0)


def _state_sample(z3s, ab_s, state_pool, state_conv3, state_delta, pool_w, pool_scale, conv_w3, a_log, dt_bias,
                  norm_w):
    nb = state_pool.shape[0]
    rows = z3s.shape[1]
    heads = C_HEADS
    bw = state_pool.shape[2]

    def full(a):
        return pl.BlockSpec(a.shape, lambda b: (0,) * a.ndim)

    def per_b(a):
        return pl.BlockSpec((1,) + a.shape[1:], lambda b: (b,) + (0,) * (a.ndim - 1))

    small = (pool_w.astype(BF16), pool_scale.reshape(1, bw), conv_w3, a_log.reshape(1, heads),
             dt_bias.reshape(1, heads), norm_w.reshape(1, LANE))
    return pl.pallas_call(
        _state_sample_body,
        out_shape=(jax.ShapeDtypeStruct((bw // LANE, rows, LANE), F32),
                   jax.ShapeDtypeStruct((heads, rows, LANE), F32),
                   jax.ShapeDtypeStruct(state_delta.shape, F32)),
        grid=(nb,),
        in_specs=[full(z3s), full(ab_s), per_b(state_pool), per_b(state_conv3), per_b(state_delta)]
        + [full(a) for a in small],
        out_specs=(pl.BlockSpec((bw // LANE, rows, LANE), lambda b: (0, 0, 0)),
                   pl.BlockSpec((heads, rows, LANE), lambda b: (0, 0, 0)),
                   per_b(state_delta)),
        compiler_params=_params("arbitrary"),
        name="state_sample",
    )(z3s, ab_s, state_pool, state_conv3, state_delta, *small)


def _cache_shift_body(c_ref, n_ref, o_ref):
    nbuf = c_ref.shape[0]
    for i in range(0, nbuf - 1, CACHE_COPY_ROWS):
        rows = min(CACHE_COPY_ROWS, nbuf - 1 - i)
        o_ref[pl.ds(i, rows)] = c_ref[pl.ds(i + 1, rows)]
    o_ref[pl.ds(nbuf - 1, 1)] = n_ref[...]


def _cache_shift(cache, new_rows):
    depth, nb, nbuf, heads, hd = cache.shape
    return pl.pallas_call(
        _cache_shift_body,
        out_shape=jax.ShapeDtypeStruct(cache.shape, cache.dtype),
        grid=(depth, nb),
        in_specs=[pl.BlockSpec((None, None, nbuf, heads, hd), lambda l, b: (l, b, 0, 0, 0)),
                  pl.BlockSpec((None, None, 1, heads, hd), lambda l, b: (l, b, 0, 0, 0))],
        out_specs=pl.BlockSpec((None, None, nbuf, heads, hd), lambda l, b: (l, b, 0, 0, 0)),
        compiler_params=_params("parallel", "parallel"),
        name="cache_shift",
    )(cache, new_rows)


def _cols(z4, cb0, ncb, r0, r1):
    blk = z4[cb0:cb0 + ncb, :, r0:r1]
    return jnp.transpose(blk, (1, 2, 0, 3)).reshape(blk.shape[1], r1 - r0, ncb * LANE)


def kernel(x_prompt, x_sample, cache_win_k, cache_win_v, state_pool, state_conv, state_delta, norm_w, w_in,
           conv_w, a_log, dt_bias, delta_norm_w, pool_w, pool_scale, w_out, final_norm_w):
    batch, seq, d_model = x_prompt.shape
    nb = x_sample.shape[0]
    depth = w_in.shape[0]
    heads = C_HEADS
    m = batch * seq
    rows_s = 16
    nbuf = cache_win_k.shape[2]
    pool_cb = state_pool.shape[-1] // LANE

    hp = x_prompt.reshape(m, d_model)
    hs = jnp.zeros((rows_s, d_model), F32).at[:nb].set(x_sample.reshape(nb, d_model))

    assert nbuf == A_PATTERNS[-1][0], "the window cache is expected full: one row in, one row out"

    outs_p = [[] for _ in range(5)]
    outs_s = [[] for _ in range(5)]
    new_k, new_v = [], []
    w_in_b, w_out_b = w_in, w_out
    for l in range(depth):
        w_ab = w_in[l, :, N_MAIN:].astype(BF16)
        conv_w3 = jnp.transpose(conv_w[l].reshape(C_CONV, 3 * heads, LANE), (1, 0, 2))

        h = _rmsnorm(hp, norm_w[l], BF16, 256)
        z3, ab, k_nat, v_nat = _inproj(h, w_in_b, l, w_ab, 1024, 512)
        mix_a = _attn_prompt(z3, k_nat, v_nat, batch, seq)
        mix_b = _pool_prompt(z3, pool_w[l], pool_scale[l], batch, seq)
        mix_c, s_new = _delta_prompt(z3, ab, conv_w3, a_log[l], dt_bias[l], delta_norm_w[l], batch, seq)
        hp = _outproj(mix_a, mix_b, mix_c, w_out_b, l, hp, 1024, 512)

        z4 = z3.reshape(CB_TOTAL, batch, seq, LANE)
        keep_p = min(seq, A_PATTERNS[-1][0])
        outs_p[0].append(k_nat.reshape(batch, seq, A_HEADS, LANE)[:, seq - keep_p:])
        outs_p[1].append(v_nat.reshape(batch, seq, A_HEADS, LANE)[:, seq - keep_p:])
        outs_p[2].append(_cols(z4, CB_UB, pool_cb, seq - B_BUF, seq))
        outs_p[3].append(_cols(z4, CB_QC, 3 * heads, seq - (C_CONV - 1), seq))
        outs_p[4].append(s_new)

        h_s = _rmsnorm(hs, norm_w[l], BF16, rows_s)
        z3s, ab_s, _, _ = _inproj(h_s, w_in_b, l, w_ab, rows_s, 512)
        mix_as = _attn_sample(z3s, cache_win_k, cache_win_v, l)
        conv3 = jnp.transpose(state_conv[l].reshape(nb, C_CONV - 1, 3 * heads, LANE), (0, 2, 1, 3))
        mix_bs, mix_cs, st_new = _state_sample(z3s, ab_s, state_pool[l], conv3, state_delta[l], pool_w[l],
                                               pool_scale[l], conv_w3, a_log[l], dt_bias[l], delta_norm_w[l])
        hs = _outproj(mix_as, mix_bs, mix_cs, w_out_b, l, hs, rows_s, 512)

        z4s = z3s[:, :nb].reshape(CB_TOTAL, nb, 1, LANE)
        new_k.append(_cols(z4s, CB_KA, A_HEADS, 0, 1).reshape(nb, 1, A_HEADS, LANE))
        new_v.append(_cols(z4s, CB_VA, A_HEADS, 0, 1).reshape(nb, 1, A_HEADS, LANE))
        outs_s[2].append(jnp.concatenate([state_pool[l], _cols(z4s, CB_UB, pool_cb, 0, 1)], axis=1)[:, 1:])
        outs_s[3].append(jnp.concatenate([state_conv[l], _cols(z4s, CB_QC, 3 * heads, 0, 1)], axis=1)[:, 1:])
        outs_s[4].append(st_new)

    y_prompt = _rmsnorm(hp, final_norm_w, F32, 256).reshape(batch, seq, d_model)
    y_sample = _rmsnorm(hs, final_norm_w, F32, rows_s)[:nb].reshape(nb, 1, d_model)
    stack = lambda xs: jnp.stack(xs, axis=0)
    win_k = _cache_shift(cache_win_k, stack(new_k))
    win_v = _cache_shift(cache_win_v, stack(new_v))
    return ((y_prompt, y_sample) + tuple(stack(o) for o in outs_p) + (win_k, win_v)
            + tuple(stack(o) for o in outs_s[2:]))
```

```python
import functools

import jax
import jax.numpy as jnp
from jax import lax
from jax.experimental import pallas as pl
from jax.experimental.pallas import tpu as pltpu

F32 = jnp.float32
BF16 = jnp.bfloat16

LANE = 128
MXU_ROWS = 16
VMEM_LIMIT = 56 * 2**20
RMS_EPS = 1e-6

A_HEADS = 8
A_PATTERNS = ((128, 1), (512, 4), (2048, 16))
A_BLOCK = 128
A_UNROLL = 4
B_POOLS = (2, 4, 8, 16)
B_BUF = 15
C_HEADS = 16
C_CONV = 4
CACHE_COPY_ROWS = 256
CONV_PAD = 8

CB_QA, CB_KA, CB_VA, CB_GA = 0, 8, 16, 24
CB_UB, CB_GB = 32, 40
CB_QC, CB_KC, CB_VC, CB_GC = 48, 64, 80, 96
CB_TOTAL = 112
N_MAIN = CB_TOTAL * LANE

HIGHEST = lax.Precision.HIGHEST


def _params(*sem):
    return pltpu.CompilerParams(dimension_semantics=sem, vmem_limit_bytes=VMEM_LIMIT)


def _sigmoid(x):
    return 1.0 / (1.0 + jnp.exp(-x))


def _silu(x):
    return x * _sigmoid(x)


def _softplus(x):
    return jnp.maximum(x, 0.0) + jnp.log(1.0 + jnp.exp(-jnp.abs(x)))


def _dot(a, b, precision=None):
    return jnp.dot(a, b, preferred_element_type=F32, precision=precision)


def _dot_nt(a, b, precision=None):
    return lax.dot_general(a, b, (((1,), (1,)), ((), ())), preferred_element_type=F32, precision=precision)


def _dot_tn(a, b, precision=None):
    return lax.dot_general(a, b, (((0,), (0,)), ((), ())), preferred_element_type=F32, precision=precision)


def _rms_body(x_ref, g_ref, o_ref):
    x = x_ref[...]
    ms = jnp.mean(x * x, axis=-1, keepdims=True)
    o_ref[...] = (x * lax.rsqrt(ms + RMS_EPS) * g_ref[...]).astype(o_ref.dtype)


def _rmsnorm(x, gain, out_dtype, tm):
    m, d = x.shape
    return pl.pallas_call(
        _rms_body,
        out_shape=jax.ShapeDtypeStruct((m, d), out_dtype),
        grid=(m // tm,),
        in_specs=[pl.BlockSpec((tm, d), lambda i: (i, 0)), pl.BlockSpec((1, d), lambda i: (0, 0))],
        out_specs=pl.BlockSpec((tm, d), lambda i: (i, 0)),
        compiler_params=_params("parallel"),
        name="rmsnorm",
    )(x, gain.reshape(1, d))


def _inproj_body(h_ref, w_ref, wab_ref, z_ref, ab_ref, k_ref, v_ref, *, k_tiles, v_tiles):
    j = pl.program_id(1)
    h = h_ref[...]
    acc = _dot_nt(h, w_ref[...].astype(BF16))
    for c in range(acc.shape[1] // LANE):
        z_ref[c] = acc[:, c * LANE:(c + 1) * LANE]

    @pl.when(j == 0)
    def _():
        ab_ref[...] = _dot_nt(h, wab_ref[...].astype(BF16))

    @pl.when((j >= k_tiles[0]) & (j < k_tiles[1]))
    def _():
        k_ref[...] = acc

    @pl.when((j >= v_tiles[0]) & (j < v_tiles[1]))
    def _():
        v_ref[...] = acc


def _inproj(h, w_t, layer, tm, tn):
    m, d = h.shape
    nab = w_t.shape[1] - N_MAIN
    assert N_MAIN % nab == 0
    a_width = A_HEADS * LANE
    k_tiles = (CB_KA * LANE // tn, CB_VA * LANE // tn)
    v_tiles = (CB_VA * LANE // tn, CB_GA * LANE // tn)

    def natural(tiles):
        return pl.BlockSpec((tm, tn), lambda i, j: (i, jnp.clip(j - tiles[0], 0, tiles[1] - tiles[0] - 1)))

    return pl.pallas_call(
        functools.partial(_inproj_body, k_tiles=k_tiles, v_tiles=v_tiles),
        out_shape=(jax.ShapeDtypeStruct((CB_TOTAL, m, LANE), F32),
                   jax.ShapeDtypeStruct((m, nab), F32),
                   jax.ShapeDtypeStruct((m, a_width), F32),
                   jax.ShapeDtypeStruct((m, a_width), F32)),
        grid=(m // tm, N_MAIN // tn),
        in_specs=[pl.BlockSpec((tm, d), lambda i, j: (i, 0)),
                  pl.BlockSpec((None, tn, d), lambda i, j: (layer, j, 0)),
                  pl.BlockSpec((None, nab, d), lambda i, j: (layer, N_MAIN // nab, 0))],
        out_specs=(pl.BlockSpec((tn // LANE, tm, LANE), lambda i, j: (j, i, 0)),
                   pl.BlockSpec((tm, nab), lambda i, j: (i, 0)),
                   natural(k_tiles), natural(v_tiles)),
        compiler_params=_params("parallel", "arbitrary"),
        name="inproj",
    )(h, w_t, w_t)


def _outproj_body(ma_ref, mb_ref, mc_ref, w_ref, x_ref, o_ref, lhs_ref):
    @pl.when(pl.program_id(1) == 0)
    def _():
        c0 = 0
        for ref in (ma_ref, mb_ref, mc_ref):
            for c in range(ref.shape[0]):
                lhs_ref[:, (c0 + c) * LANE:(c0 + c + 1) * LANE] = ref[c].astype(BF16)
            c0 += ref.shape[0]

    o_ref[...] = x_ref[...] + _dot(lhs_ref[...], w_ref[...].astype(BF16))


def _outproj(mix_a, mix_b, mix_c, w_all, layer, x, tm, tn):
    m, d = x.shape
    k = w_all.shape[1]

    def mix_spec(a):
        return pl.BlockSpec((a.shape[0], tm, LANE), lambda i, j: (0, i, 0))

    return pl.pallas_call(
        _outproj_body,
        out_shape=jax.ShapeDtypeStruct((m, d), F32),
        grid=(m // tm, d // tn),
        in_specs=[mix_spec(mix_a), mix_spec(mix_b), mix_spec(mix_c),
                  pl.BlockSpec((None, k, tn), lambda i, j: (layer, 0, j)),
                  pl.BlockSpec((tm, tn), lambda i, j: (i, j))],
        out_specs=pl.BlockSpec((tm, tn), lambda i, j: (i, j)),
        scratch_shapes=[pltpu.VMEM((tm, k), BF16)],
        compiler_params=_params("parallel", "arbitrary"),
        name="outproj",
    )(mix_a, mix_b, mix_c, w_all, x)


def _attn_prompt_body(q_ref, k_ref, v_ref, g_ref, o_ref, acc_ref, m_ref, l_ref, *, seq):
    n = A_BLOCK
    scale = LANE ** -0.5
    qi = lax.broadcasted_iota(jnp.int32, (n, n), 0)
    kj = lax.broadcasted_iota(jnp.int32, (n, n), 1)
    cur_mask = kj <= qi
    prev_mask = kj >= qi

    def rows(t0, d):
        return pl.ds(t0, n) if d == 1 else pl.ds(t0, n, stride=d)

    blocks = []
    for pi, (window, d) in enumerate(A_PATTERNS):
        for r in range(d):
            for blk in range(seq // window):
                t0 = blk * window + r
                blocks.append((pi, rows(t0, d), rows(t0 - window, d) if blk > 0 else None))
    waves = [blocks[i:i + A_UNROLL] for i in range(0, len(blocks), A_UNROLL)]

    def scores(block):
        _, cur, prev = block
        q = q_ref[0, cur, :].astype(BF16)
        sc = [_dot_nt(q, k_ref[cur, :].astype(BF16)), v_ref[cur, :].astype(BF16)]
        if prev is not None:
            sc += [_dot_nt(q, k_ref[prev, :].astype(BF16)), v_ref[prev, :].astype(BF16)]
        return sc

    def softmax(sc):
        s_c = jnp.where(cur_mask, sc[0] * scale, -jnp.inf)
        if len(sc) == 2:
            m = jnp.max(s_c, axis=1, keepdims=True)
            p_c = jnp.exp(s_c - m)
            return m, jnp.sum(p_c, axis=1, keepdims=True), p_c.astype(BF16)
        s_p = jnp.where(prev_mask, sc[2] * scale, -jnp.inf)
        m = jnp.max(jnp.maximum(s_c, s_p), axis=1, keepdims=True)
        p_c = jnp.exp(s_c - m)
        p_p = jnp.exp(s_p - m)
        return m, jnp.sum(p_c + p_p, axis=1, keepdims=True), p_c.astype(BF16), p_p.astype(BF16)

    def values(sc, pr):
        o = _dot(pr[2], sc[1])
        if len(sc) > 2:
            o = o + _dot(pr[3], sc[3])
        return o

    def accumulate(block, pr, o):
        pi, cur, _ = block
        mb = jnp.broadcast_to(pr[0], (n, LANE))
        lb = jnp.broadcast_to(pr[1], (n, LANE))
        if pi == 0:
            acc_ref[cur, :] = o
            m_ref[cur, :] = mb
            l_ref[cur, :] = lb
        else:
            m_old = m_ref[cur, :]
            m_new = jnp.maximum(m_old, mb)
            w_old = jnp.exp(m_old - m_new)
            w_cur = jnp.exp(mb - m_new)
            acc_ref[cur, :] = acc_ref[cur, :] * w_old + o * w_cur
            l_ref[cur, :] = l_ref[cur, :] * w_old + lb * w_cur
            m_ref[cur, :] = m_new

    sc_of, pr_of, o_of = {}, {}, {}
    for step in range(len(waves) + 3):
        if step < len(waves):
            sc_of[step] = [scores(blk) for blk in waves[step]]
        w = step - 1
        if 0 <= w < len(waves):
            pr_of[w] = [softmax(sc) for sc in sc_of[w]]
        w = step - 2
        if 0 <= w < len(waves):
            o_of[w] = [values(sc, pr) for sc, pr in zip(sc_of.pop(w), pr_of[w])]
        w = step - 3
        if 0 <= w < len(waves):
            for blk, pr, o in zip(waves[w], pr_of.pop(w), o_of.pop(w)):
                accumulate(blk, pr, o)

    rows_out = 256

    def finish(i, carry):
        sl = pl.ds(pl.multiple_of(i * rows_out, rows_out), rows_out)
        out = acc_ref[sl, :] / l_ref[sl, :] * _silu(g_ref[0, sl, :])
        o_ref[0, sl, :] = out.astype(o_ref.dtype)
        return carry

    lax.fori_loop(0, seq // rows_out, finish, 0)


def _attn_prompt(z3, k_nat, v_nat, batch, seq):
    m = z3.shape[1]

    def spec(cb0):
        return pl.BlockSpec((1, seq, LANE), lambda b, h: (cb0 + h, b, 0))

    nat = pl.BlockSpec((seq, LANE), lambda b, h: (b, h))
    return pl.pallas_call(
        functools.partial(_attn_prompt_body, seq=seq),
        out_shape=jax.ShapeDtypeStruct((A_HEADS, m, LANE), BF16),
        grid=(batch, A_HEADS),
        in_specs=[spec(CB_QA), nat, nat, spec(CB_GA)],
        out_specs=pl.BlockSpec((1, seq, LANE), lambda b, h: (h, b, 0)),
        scratch_shapes=[pltpu.VMEM((seq, LANE), F32)] * 3,
        compiler_params=_params("parallel", "parallel"),
        name="attn_prompt",
    )(z3, k_nat, v_nat, z3)


def _pool_prompt_body(u_ref, g_ref, w_ref, sc_ref, o_ref, *, seq):
    grp = pl.program_id(1)
    row = lax.broadcasted_iota(jnp.int32, (seq, 1), 0)
    for gi, width in enumerate(B_POOLS):
        @pl.when(grp == gi)
        def _(width=width):
            u = jnp.concatenate([u_ref[0], u_ref[1]], axis=1)
            s = u
            step = 1
            while step < width:
                s = s + jnp.where(row >= step, pltpu.roll(s, step, axis=0), 0.0)
                step *= 2
            cnt = jnp.minimum(row + 1, width).astype(F32)
            y = (s / cnt - u).astype(BF16)
            out = _dot(y, w_ref[0]) * sc_ref[0]
            gate = jnp.concatenate([g_ref[0], g_ref[1]], axis=1)
            out = (out * _silu(gate)).astype(o_ref.dtype)
            o_ref[0] = out[:, :LANE]
            o_ref[1] = out[:, LANE:]


def _pool_prompt(z3, pool_w, pool_scale, batch, seq):
    m = z3.shape[1]
    ngrp = len(B_POOLS)
    gw = pool_w.shape[1]
    cpg = gw // LANE

    def spec(cb0):
        return pl.BlockSpec((cpg, seq, LANE), lambda b, g: (cb0 // cpg + g, b, 0))

    return pl.pallas_call(
        functools.partial(_pool_prompt_body, seq=seq),
        out_shape=jax.ShapeDtypeStruct((ngrp * cpg, m, LANE), BF16),
        grid=(batch, ngrp),
        in_specs=[spec(CB_UB), spec(CB_GB),
                  pl.BlockSpec((1, gw, gw), lambda b, g: (g, 0, 0)),
                  pl.BlockSpec((1, 1, gw), lambda b, g: (g, 0, 0))],
        out_specs=pl.BlockSpec((cpg, seq, LANE), lambda b, g: (g, b, 0)),
        compiler_params=_params("parallel", "parallel"),
        name="pool_prompt",
    )(z3, z3, pool_w.astype(BF16), pool_scale.reshape(ngrp, 1, gw))


def _unit_lower_inverses(mats):
    c = mats[0].shape[0]
    eye = (lax.broadcasted_iota(jnp.int32, (c, c), 0) == lax.broadcasted_iota(jnp.int32, (c, c), 1)).astype(F32)
    xs = [eye - a for a in mats]
    pbs = [(-a).astype(BF16) for a in mats]
    k = 2
    while k < c:
        pbs = [_dot(pb, pb).astype(BF16) for pb in pbs]
        xs = [x + _dot(x.astype(BF16), pb) for x, pb in zip(xs, pbs)]
        k *= 2
    return xs


def _delta_prompt_body(q_ref, k_ref, v_ref, gate_ref, ab_ref, cw_ref, alog_ref, dtb_ref, nw_ref,
                       o_ref, s_ref, xbuf_ref, gct_ref, *, group):
    heads = C_HEADS
    c = q_ref.shape[1]
    tile = c
    hd = LANE

    @pl.when(pl.program_id(1) == 0)
    def _():
        s_ref[...] = jnp.zeros_like(s_ref)
        xbuf_ref[:, 0:CONV_PAD, :] = jnp.zeros((3 * heads, CONV_PAD, hd), F32)

    for j in range(heads):
        xbuf_ref[j, CONV_PAD:CONV_PAD + tile, :] = q_ref[j]
        xbuf_ref[heads + j, CONV_PAD:CONV_PAD + tile, :] = k_ref[j]
        xbuf_ref[2 * heads + j, CONV_PAD:CONV_PAD + tile, :] = v_ref[j]

    g = -jnp.exp(alog_ref[...]) * _softplus(ab_ref[:, 0:heads] + dtb_ref[...])
    beta = _sigmoid(ab_ref[:, heads:2 * heads])

    ri = lax.broadcasted_iota(jnp.int32, (c, c), 0)
    ci = lax.broadcasted_iota(jnp.int32, (c, c), 1)
    causal = ri >= ci
    strict = ri > ci
    gcum = _dot(causal.astype(F32), g, HIGHEST)
    eye_h = (lax.broadcasted_iota(jnp.int32, (heads, heads), 0)
             == lax.broadcasted_iota(jnp.int32, (heads, heads), 1)).astype(F32)
    gct_ref[...] = _dot_nt(eye_h, gcum, HIGHEST)
    lane_h = lax.broadcasted_iota(jnp.int32, (1, heads), 1)

    def conv(j):
        w = cw_ref[j]
        first = CONV_PAD - (C_CONV - 1)
        y = xbuf_ref[j, pl.ds(first, tile), :] * w[0:1]
        for i in range(1, C_CONV):
            y = y + xbuf_ref[j, pl.ds(first + i, tile), :] * w[i:i + 1]
        return _silu(y)

    def l2n(x):
        return x * lax.rsqrt(jnp.sum(x * x, axis=-1, keepdims=True) + RMS_EPS)

    def group_body(hg, carry):
        hs = [hg * group + i for i in range(group)]
        states = [s_ref[0, h] for h in hs]
        pre = []
        for h in hs:
            onehot = lane_h == h
            q = l2n(conv(h)) * (hd ** -0.5)
            k = l2n(conv(heads + h))
            v = conv(2 * heads + h)
            gcol = jnp.sum(jnp.where(onehot, gcum, 0.0), axis=1, keepdims=True)
            bcol = jnp.sum(jnp.where(onehot, beta, 0.0), axis=1, keepdims=True)
            grow = gct_ref[pl.ds(h, 1), :]
            dec = jnp.exp(jnp.where(causal, gcol - grow, -jnp.inf))
            kb = k * bcol
            kq = _dot_nt(jnp.concatenate([kb, q], axis=0).astype(BF16), k.astype(BF16))
            a_mat = jnp.where(strict, kq[:c] * dec, 0.0)
            qk = (kq[c:] * dec).astype(BF16)
            egc = jnp.exp(gcol)
            rhs = jnp.concatenate([v * bcol, kb * egc], axis=1).astype(BF16)
            glast = gcol[c - 1:c, :]
            kd = (k * jnp.exp(glast - gcol)).astype(BF16)
            pre.append((a_mat, qk, rhs, (q * egc).astype(BF16), kd, jnp.exp(glast)))
        invs = _unit_lower_inverses([p[0] for p in pre])
        sols = [_dot(t.astype(BF16), p[2]) for t, p in zip(invs, pre)]
        ws_qs = [_dot(jnp.concatenate([sol[:, hd:].astype(BF16), p[3]], axis=0), st.astype(BF16))
                 for sol, p, st in zip(sols, pre, states)]
        v_news = [(sol[:, :hd] - wq[:c]).astype(BF16) for sol, wq in zip(sols, ws_qs)]
        outs = [wq[c:] + _dot(p[1], vn) for wq, p, vn in zip(ws_qs, pre, v_news)]
        new_states = [st * p[5] + _dot_tn(p[4], vn) for st, p, vn in zip(states, pre, v_news)]
        for h, o in zip(hs, outs):
            on = o * lax.rsqrt(jnp.mean(o * o, axis=-1, keepdims=True) + RMS_EPS) * nw_ref[...]
            o_ref[h] = (on * _silu(gate_ref[h])).astype(o_ref.dtype)
        for h, state in zip(hs, new_states):
            s_ref[0, h] = state
        return carry

    lax.fori_loop(0, heads // group, group_body, 0)

    for j in range(3 * heads):
        xbuf_ref[j, 0:CONV_PAD, :] = xbuf_ref[j, tile:tile + CONV_PAD, :]


def _delta_prompt(z3, ab, conv_w3, a_log, dt_bias, norm_w, batch, seq, tile=128, group=8):
    m = z3.shape[1]
    heads = C_HEADS
    nt = seq // tile

    def spec(cb0):
        return pl.BlockSpec((heads, tile, LANE), lambda b, t: (cb0 // heads, b * nt + t, 0))

    def full(a):
        return pl.BlockSpec(a.shape, lambda b, t: (0,) * a.ndim)

    small = (conv_w3, a_log.reshape(1, heads), dt_bias.reshape(1, heads), norm_w.reshape(1, LANE))
    return pl.pallas_call(
        functools.partial(_delta_prompt_body, group=group),
        out_shape=(jax.ShapeDtypeStruct((heads, m, LANE), BF16),
                   jax.ShapeDtypeStruct((batch, heads, LANE, LANE), F32)),
        grid=(batch, nt),
        in_specs=[spec(CB_QC), spec(CB_KC), spec(CB_VC), spec(CB_GC),
                  pl.BlockSpec((tile, 2 * heads), lambda b, t: (b * nt + t, 0))] + [full(a) for a in small],
        out_specs=(pl.BlockSpec((heads, tile, LANE), lambda b, t: (0, b * nt + t, 0)),
                   pl.BlockSpec((1, heads, LANE, LANE), lambda b, t: (b, 0, 0, 0))),
        scratch_shapes=[pltpu.VMEM((3 * heads, CONV_PAD + tile, LANE), F32),
                        pltpu.VMEM((heads, tile), F32)],
        compiler_params=_params("parallel", "arbitrary"),
        name="delta_prompt",
    )(z3, z3, z3, z3, ab, *small)


def _attn_sample_body(z_ref, *refs, npat):
    kv_refs, o_ref = refs[:2 * npat], refs[2 * npat]
    b = pl.program_id(0)
    heads = A_HEADS
    n = A_BLOCK

    @pl.when(b == 0)
    def _():
        o_ref[...] = jnp.zeros_like(o_ref)

    row = pl.ds(b, 1)

    def head_rows(cb0):
        return jnp.concatenate([z_ref[cb0 + h, row, :] for h in range(heads)], axis=0)

    q = head_rows(CB_QA).astype(BF16)
    kn = head_rows(CB_KA).astype(BF16).astype(F32)
    vn = head_rows(CB_VA).astype(BF16).astype(F32)
    gate = head_rows(CB_GA)
    scale = LANE ** -0.5
    qpad = jnp.concatenate([q, jnp.zeros((MXU_ROWS - heads, LANE), BF16)], axis=0)
    s_new = jnp.sum(q.astype(F32) * kn, axis=1, keepdims=True) * scale
    col = lax.broadcasted_iota(jnp.int32, (heads, n * heads), 1)
    own = (col & (heads - 1)) == lax.broadcasted_iota(jnp.int32, (heads, n * heads), 0)
    parts = []
    for i in range(npat):
        kmat = kv_refs[2 * i][...].reshape(n * heads, LANE).astype(BF16)
        vmat = kv_refs[2 * i + 1][...].reshape(n * heads, LANE).astype(BF16)
        s = jnp.where(own, _dot_nt(qpad, kmat)[:heads] * scale, -jnp.inf)
        m = jnp.maximum(jnp.max(s, axis=1, keepdims=True), s_new)
        p = jnp.exp(s - m)
        pn = jnp.exp(s_new - m)
        l = jnp.sum(p, axis=1, keepdims=True) + pn
        ppad = jnp.concatenate([p, jnp.zeros((MXU_ROWS - heads, n * heads), F32)], axis=0).astype(BF16)
        o = _dot(ppad, vmat)[:heads] + pn.astype(BF16).astype(F32) * vn
        parts.append((o, m, l))
    m_all = functools.reduce(jnp.maximum, [m for _, m, _ in parts])
    num = jnp.zeros((heads, LANE), F32)
    den = jnp.zeros((heads, 1), F32)
    for o, m, l in parts:
        wgt = jnp.exp(m - m_all)
        num = num + o * wgt
        den = den + l * wgt
    out = num / den * _silu(gate)
    for h in range(heads):
        o_ref[h, row, :] = out[h:h + 1]


def _attn_sample(z3s, cache_k, cache_v, layer):
    depth, nb, nbuf, heads, hd = cache_k.shape
    rows = z3s.shape[1]
    n = A_BLOCK
    assert heads == A_HEADS and hd == LANE
    assert all(nbuf % window == 0 for window, _ in A_PATTERNS), "cached window shorter than a pattern"
    views, specs = [], []
    for window, d in A_PATTERNS:
        last = nbuf // d // n - 1
        spec = pl.BlockSpec((None, None, n, None, heads, hd), lambda b, last=last: (layer, b, last, 0, 0, 0))
        for cache in (cache_k, cache_v):
            views.append(cache.reshape(depth, nb, nbuf // d, d, heads, hd))
            specs.append(spec)
    return pl.pallas_call(
        functools.partial(_attn_sample_body, npat=len(A_PATTERNS)),
        out_shape=jax.ShapeDtypeStruct((heads, rows, LANE), F32),
        grid=(nb,),
        in_specs=[pl.BlockSpec(z3s.shape, lambda b: (0, 0, 0))] + specs,
        out_specs=pl.BlockSpec((heads, rows, LANE), lambda b: (0, 0, 0)),
        compiler_params=_params("arbitrary"),
        name="attn_sample",
    )(z3s, *views)


def _state_sample_body(z_ref, ab_ref, pool_ref, conv_ref, st_ref, pw_ref, psc_ref, cw_ref, alog_ref, dtb_ref,
                       nw_ref, mb_ref, mc_ref, so_ref):
    b = pl.program_id(0)
    heads = C_HEADS
    hd = LANE

    @pl.when(b == 0)
    def _():
        mb_ref[...] = jnp.zeros_like(mb_ref)
        mc_ref[...] = jnp.zeros_like(mc_ref)

    row = pl.ds(b, 1)

    gw = pw_ref.shape[1]
    cpg = gw // LANE
    for gi, width in enumerate(B_POOLS):
        ys = []
        for j in range(cpg):
            cb = gi * cpg + j
            un = z_ref[CB_UB + cb, row, :]
            prev = pool_ref[0, B_BUF - (width - 1):B_BUF, cb * LANE:(cb + 1) * LANE]
            mean = (jnp.sum(prev, axis=0, keepdims=True) + un) / float(width)
            ys.append(mean - un)
        y = jnp.broadcast_to(jnp.concatenate(ys, axis=1), (MXU_ROWS, gw)).astype(BF16)
        out = _dot(y, pw_ref[gi])[0:1] * psc_ref[:, gi * gw:(gi + 1) * gw]
        for j in range(cpg):
            cb = gi * cpg + j
            mb_ref[cb, row, :] = out[:, j * LANE:(j + 1) * LANE] * _silu(z_ref[CB_GB + cb, row, :])

    arow = ab_ref[row, :]
    lane_ab = lax.broadcasted_iota(jnp.int32, (1, 2 * heads), 1)
    lane_h = lax.broadcasted_iota(jnp.int32, (1, heads), 1)
    eye = lax.broadcasted_iota(jnp.int32, (hd, hd), 0) == lax.broadcasted_iota(jnp.int32, (hd, hd), 1)

    def pick(vec, lanes, idx):
        return jnp.sum(jnp.where(lanes == idx, vec, 0.0), axis=1, keepdims=True)

    def column(x):
        return jnp.sum(jnp.where(eye, jnp.broadcast_to(x, (hd, hd)), 0.0), axis=1, keepdims=True)

    def conv(j):
        st = conv_ref[0, j]
        w = cw_ref[j]
        y = st[0:1] * w[0:1]
        y = y + st[1:2] * w[1:2]
        y = y + st[2:3] * w[2:3]
        y = y + z_ref[CB_QC + j, row, :] * w[3:4]
        return _silu(y)

    def l2n(x):
        return x * lax.rsqrt(jnp.sum(x * x, axis=-1, keepdims=True) + RMS_EPS)

    def head_body(h, carry):
        q = l2n(conv(h)) * (hd ** -0.5)
        k = l2n(conv(heads + h))
        v = conv(2 * heads + h)
        g = -jnp.exp(pick(alog_ref[...], lane_h, h)) * _softplus(pick(arow, lane_ab, h) + pick(dtb_ref[...], lane_h, h))
        beta = _sigmoid(pick(arow, lane_ab, heads + h))
        state = st_ref[0, h] * jnp.exp(g)
        kcol = column(k)
        delta = (v - jnp.sum(kcol * state, axis=0, keepdims=True)) * beta
        state = state + kcol * delta
        o = jnp.sum(column(q) * state, axis=0, keepdims=True)
        on = o * lax.rsqrt(jnp.mean(o * o, axis=-1, keepdims=True) + RMS_EPS) * nw_ref[...]
        mc_ref[h, row, :] = on * _silu(z_ref[CB_GC + h, row, :])
        so_ref[0, h] = state
        return carry

    lax.fori_loop(0, heads, head_body, 0)


def _state_sample(z3s, ab_s, state_pool, state_conv3, state_delta, pool_w, pool_scale, conv_w3, a_log, dt_bias,
                  norm_w):
    nb = state_pool.shape[0]
    rows = z3s.shape[1]
    heads = C_HEADS
    bw = state_pool.shape[2]

    def full(a):
        return pl.BlockSpec(a.shape, lambda b: (0,) * a.ndim)

    def per_b(a):
        return pl.BlockSpec((1,) + a.shape[1:], lambda b: (b,) + (0,) * (a.ndim - 1))

    small = (pool_w.astype(BF16), pool_scale.reshape(1, bw), conv_w3, a_log.reshape(1, heads),
             dt_bias.reshape(1, heads), norm_w.reshape(1, LANE))
    return pl.pallas_call(
        _state_sample_body,
        out_shape=(jax.ShapeDtypeStruct((bw // LANE, rows, LANE), F32),
                   jax.ShapeDtypeStruct((heads, rows, LANE), F32),
                   jax.ShapeDtypeStruct(state_delta.shape, F32)),
        grid=(nb,),
        in_specs=[full(z3s), full(ab_s), per_b(state_pool), per_b(state_conv3), per_b(state_delta)]
        + [full(a) for a in small],
        out_specs=(pl.BlockSpec((bw // LANE, rows, LANE), lambda b: (0, 0, 0)),
                   pl.BlockSpec((heads, rows, LANE), lambda b: (0, 0, 0)),
                   per_b(state_delta)),
        compiler_params=_params("arbitrary"),
        name="state_sample",
    )(z3s, ab_s, state_pool, state_conv3, state_delta, *small)


def _cache_shift_body(c_ref, n_ref, o_ref):
    nbuf = c_ref.shape[0]
    for i in range(0, nbuf - 1, CACHE_COPY_ROWS):
        rows = min(CACHE_COPY_ROWS, nbuf - 1 - i)
        o_ref[pl.ds(i, rows)] = c_ref[pl.ds(i + 1, rows)]
    o_ref[pl.ds(nbuf - 1, 1)] = n_ref[...]


def _cache_shift(cache, new_rows):
    depth, nb, nbuf, heads, hd = cache.shape
    return pl.pallas_call(
        _cache_shift_body,
        out_shape=jax.ShapeDtypeStruct(cache.shape, cache.dtype),
        grid=(depth, nb),
        in_specs=[pl.BlockSpec((None, None, nbuf, heads, hd), lambda l, b: (l, b, 0, 0, 0)),
                  pl.BlockSpec((None, None, 1, heads, hd), lambda l, b: (l, b, 0, 0, 0))],
        out_specs=pl.BlockSpec((None, None, nbuf, heads, hd), lambda l, b: (l, b, 0, 0, 0)),
        compiler_params=_params("parallel", "parallel"),
        name="cache_shift",
    )(cache, new_rows)


def _cols(z4, cb0, ncb, r0, r1):
    blk = z4[cb0:cb0 + ncb, :, r0:r1]
    return jnp.transpose(blk, (1, 2, 0, 3)).reshape(blk.shape[1], r1 - r0, ncb * LANE)


def kernel(x_prompt, x_sample, cache_win_k, cache_win_v, state_pool, state_conv, state_delta, norm_w, w_in,
           conv_w, a_log, dt_bias, delta_norm_w, pool_w, pool_scale, w_out, final_norm_w):
    batch, seq, d_model = x_prompt.shape
    nb = x_sample.shape[0]
    depth = w_in.shape[0]
    heads = C_HEADS
    m = batch * seq
    rows_s = 16
    nbuf = cache_win_k.shape[2]
    pool_cb = state_pool.shape[-1] // LANE

    hp = x_prompt.reshape(m, d_model)
    hs = jnp.zeros((rows_s, d_model), F32).at[:nb].set(x_sample.reshape(nb, d_model))

    assert nbuf == A_PATTERNS[-1][0], "the window cache is expected full: one row in, one row out"

    w_in_t = jnp.swapaxes(w_in, 1, 2)
    outs_p = [[] for _ in range(5)]
    outs_s = [[] for _ in range(5)]
    new_k, new_v = [], []
    for l in range(depth):
        conv_w3 = jnp.transpose(conv_w[l].reshape(C_CONV, 3 * heads, LANE), (1, 0, 2))

        h = _rmsnorm(hp, norm_w[l], BF16, 256)
        z3, ab, k_nat, v_nat = _inproj(h, w_in_t, l, 1024, 512)
        mix_a = _attn_prompt(z3, k_nat, v_nat, batch, seq)
        mix_b = _pool_prompt(z3, pool_w[l], pool_scale[l], batch, seq)
        mix_c, s_new = _delta_prompt(z3, ab, conv_w3, a_log[l], dt_bias[l], delta_norm_w[l], batch, seq)
        hp = _outproj(mix_a, mix_b, mix_c, w_out, l, hp, 1024, 512)

        z4 = z3.reshape(CB_TOTAL, batch, seq, LANE)
        keep_p = min(seq, A_PATTERNS[-1][0])
        outs_p[0].append(k_nat.reshape(batch, seq, A_HEADS, LANE)[:, seq - keep_p:])
        outs_p[1].append(v_nat.reshape(batch, seq, A_HEADS, LANE)[:, seq - keep_p:])
        outs_p[2].append(_cols(z4, CB_UB, pool_cb, seq - B_BUF, seq))
        outs_p[3].append(_cols(z4, CB_QC, 3 * heads, seq - (C_CONV - 1), seq))
        outs_p[4].append(s_new)

        h_s = _rmsnorm(hs, norm_w[l], BF16, rows_s)
        z3s, ab_s, _, _ = _inproj(h_s, w_in_t, l, rows_s, 512)
        mix_as = _attn_sample(z3s, cache_win_k, cache_win_v, l)
        conv3 = jnp.transpose(state_conv[l].reshape(nb, C_CONV - 1, 3 * heads, LANE), (0, 2, 1, 3))
        mix_bs, mix_cs, st_new = _state_sample(z3s, ab_s, state_pool[l], conv3, state_delta[l], pool_w[l],
                                               pool_scale[l], conv_w3, a_log[l], dt_bias[l], delta_norm_w[l])
        hs = _outproj(mix_as, mix_bs, mix_cs, w_out, l, hs, rows_s, 512)

        z4s = z3s[:, :nb].reshape(CB_TOTAL, nb, 1, LANE)
        new_k.append(_cols(z4s, CB_KA, A_HEADS, 0, 1).reshape(nb, 1, A_HEADS, LANE))
        new_v.append(_cols(z4s, CB_VA, A_HEADS, 0, 1).reshape(nb, 1, A_HEADS, LANE))
        outs_s[2].append(jnp.concatenate([state_pool[l], _cols(z4s, CB_UB, pool_cb, 0, 1)], axis=1)[:, 1:])
        outs_s[3].append(jnp.concatenate([state_conv[l], _cols(z4s, CB_QC, 3 * heads, 0, 1)], axis=1)[:, 1:])
        outs_s[4].append(st_new)

    y_prompt = _rmsnorm(hp, final_norm_w, F32, 256).reshape(batch, seq, d_model)
    y_sample = _rmsnorm(hs, final_norm_w, F32, rows_s)[:nb].reshape(nb, 1, d_model)
    stack = lambda xs: jnp.stack(xs, axis=0)
    win_k = _cache_shift(cache_win_k, stack(new_k))
    win_v = _cache_shift(cache_win_v, stack(new_v))
    return ((y_prompt, y_sample) + tuple(stack(o) for o in outs_p) + (win_k, win_v)
            + tuple(stack(o) for o in outs_s[2:]))
```

```python
import functools

import jax
import jax.numpy as jnp
from jax import lax
from jax.experimental import pallas as pl
from jax.experimental.pallas import tpu as pltpu

F32 = jnp.float32
BF16 = jnp.bfloat16

LANE = 128
ROW_CHUNK = 1024
MXU_ROWS = 16
VMEM_LIMIT = 56 * 2**20
RMS_EPS = 1e-6

A_HEADS = 8
A_PATTERNS = ((128, 1), (512, 4), (2048, 16))
A_BLOCK = 128
A_UNROLL = 4
B_POOLS = (2, 4, 8, 16)
B_BUF = 15
C_HEADS = 16
C_CONV = 4
CACHE_COPY_ROWS = 256
CONV_PAD = 8

CB_QA, CB_KA, CB_VA, CB_GA = 0, 8, 16, 24
CB_UB, CB_GB = 32, 40
CB_QC, CB_KC, CB_VC, CB_GC = 48, 64, 80, 96
CB_TOTAL = 112
N_MAIN = CB_TOTAL * LANE

HIGHEST = lax.Precision.HIGHEST
ROW_RESIDENT = pl.Buffered(1)


def _params(*sem):
    return pltpu.CompilerParams(dimension_semantics=sem, vmem_limit_bytes=VMEM_LIMIT)


def _sigmoid(x):
    return 1.0 / (1.0 + jnp.exp(-x))


def _silu(x):
    half = 0.5 * x
    return half + half * jnp.tanh(half)


def _softplus(x):
    return jnp.maximum(x, 0.0) + jnp.log(1.0 + jnp.exp(-jnp.abs(x)))


def _dot(a, b, precision=None):
    return jnp.dot(a, b, preferred_element_type=F32, precision=precision)


def _dot_nt(a, b, precision=None):
    return lax.dot_general(a, b, (((1,), (1,)), ((), ())), preferred_element_type=F32, precision=precision)


def _dot_tn(a, b, precision=None):
    return lax.dot_general(a, b, (((0,), (0,)), ((), ())), preferred_element_type=F32, precision=precision)


def _rms_body(x_ref, g_ref, o_ref):
    x = x_ref[...]
    ms = jnp.mean(x * x, axis=-1, keepdims=True)
    o_ref[...] = (x * lax.rsqrt(ms + RMS_EPS) * g_ref[...]).astype(o_ref.dtype)


def _rmsnorm(x, gain, out_dtype, tm):
    m, d = x.shape
    return pl.pallas_call(
        _rms_body,
        out_shape=jax.ShapeDtypeStruct((m, d), out_dtype),
        grid=(m // tm,),
        in_specs=[pl.BlockSpec((tm, d), lambda i: (i, 0)), pl.BlockSpec((1, d), lambda i: (0, 0))],
        out_specs=pl.BlockSpec((tm, d), lambda i: (i, 0)),
        compiler_params=_params("parallel"),
        name="rmsnorm",
    )(x, gain.reshape(1, d))


def _inproj_body(h_ref, w_ref, wab_ref, z_ref, ab_ref):
    tm = h_ref.shape[0]
    w = w_ref[...].astype(BF16)
    for r0 in range(0, tm, ROW_CHUNK):
        rows = pl.ds(r0, min(ROW_CHUNK, tm - r0))
        h = h_ref[rows, :]
        acc = _dot_nt(h, w)
        for c in range(acc.shape[1] // LANE):
            z_ref[c, rows, :] = acc[:, c * LANE:(c + 1) * LANE]

        @pl.when(pl.program_id(1) == 0)
        def _():
            ab_ref[rows, :] = _dot_nt(h, wab_ref[...].astype(BF16))


def _inproj(h, w_t, layer, tm, tn):
    m, d = h.shape
    nab = w_t.shape[1] - N_MAIN
    assert N_MAIN % nab == 0
    return pl.pallas_call(
        _inproj_body,
        out_shape=(jax.ShapeDtypeStruct((CB_TOTAL, m, LANE), F32),
                   jax.ShapeDtypeStruct((m, nab), F32)),
        grid=(m // tm, N_MAIN // tn),
        in_specs=[pl.BlockSpec((tm, d), lambda i, j: (i, 0), pipeline_mode=ROW_RESIDENT),
                  pl.BlockSpec((None, tn, d), lambda i, j: (layer, j, 0)),
                  pl.BlockSpec((None, nab, d), lambda i, j: (layer, N_MAIN // nab, 0))],
        out_specs=(pl.BlockSpec((tn // LANE, tm, LANE), lambda i, j: (j, i, 0)),
                   pl.BlockSpec((tm, nab), lambda i, j: (i, 0))),
        compiler_params=_params("parallel", "arbitrary"),
        name="inproj",
    )(h, w_t, w_t)


def _outproj_body(ma_ref, mb_ref, mc_ref, w_ref, x_ref, o_ref, lhs_ref):
    @pl.when(pl.program_id(1) == 0)
    def _():
        c0 = 0
        for ref in (ma_ref, mb_ref, mc_ref):
            for c in range(ref.shape[0]):
                lhs_ref[:, (c0 + c) * LANE:(c0 + c + 1) * LANE] = ref[c].astype(BF16)
            c0 += ref.shape[0]

    w = w_ref[...].astype(BF16)
    tm = o_ref.shape[0]
    for r0 in range(0, tm, ROW_CHUNK):
        rows = pl.ds(r0, min(ROW_CHUNK, tm - r0))
        o_ref[rows, :] = x_ref[rows, :] + _dot(lhs_ref[rows, :], w)


def _outproj(mix_a, mix_b, mix_c, w_all, layer, x, tm, tn):
    m, d = x.shape
    k = w_all.shape[1]

    def mix_spec(a):
        return pl.BlockSpec((a.shape[0], tm, LANE), lambda i, j: (0, i, 0), pipeline_mode=ROW_RESIDENT)

    return pl.pallas_call(
        _outproj_body,
        out_shape=jax.ShapeDtypeStruct((m, d), F32),
        grid=(m // tm, d // tn),
        in_specs=[mix_spec(mix_a), mix_spec(mix_b), mix_spec(mix_c),
                  pl.BlockSpec((None, k, tn), lambda i, j: (layer, 0, j)),
                  pl.BlockSpec((tm, tn), lambda i, j: (i, j))],
        out_specs=pl.BlockSpec((tm, tn), lambda i, j: (i, j)),
        scratch_shapes=[pltpu.VMEM((tm, k), BF16)],
        compiler_params=_params("parallel", "arbitrary"),
        name="outproj",
    )(mix_a, mix_b, mix_c, w_all, x)


def _attn_prompt_body(q_ref, k_ref, v_ref, g_ref, o_ref, kn_ref, vn_ref, acc_ref, m_ref, l_ref, *, seq):
    n = A_BLOCK
    scale = LANE ** -0.5
    kn_ref[...] = k_ref[0]
    vn_ref[...] = v_ref[0]
    qi = lax.broadcasted_iota(jnp.int32, (n, n), 0)
    kj = lax.broadcasted_iota(jnp.int32, (n, n), 1)
    cur_mask = kj <= qi
    prev_mask = kj >= qi

    def rows(t0, d):
        return pl.ds(t0, n) if d == 1 else pl.ds(t0, n, stride=d)

    blocks = []
    for pi, (window, d) in enumerate(A_PATTERNS):
        for r in range(d):
            for blk in range(seq // window):
                t0 = blk * window + r
                blocks.append((pi, rows(t0, d), rows(t0 - window, d) if blk > 0 else None))
    waves = [blocks[i:i + A_UNROLL] for i in range(0, len(blocks), A_UNROLL)]

    def scores(block):
        _, cur, prev = block
        q = q_ref[0, cur, :].astype(BF16)
        sc = [_dot_nt(q, k_ref[0, cur, :].astype(BF16)), v_ref[0, cur, :].astype(BF16)]
        if prev is not None:
            sc += [_dot_nt(q, k_ref[0, prev, :].astype(BF16)), v_ref[0, prev, :].astype(BF16)]
        return sc

    def softmax(sc):
        s_c = jnp.where(cur_mask, sc[0] * scale, -jnp.inf)
        if len(sc) == 2:
            m = jnp.max(s_c, axis=1, keepdims=True)
            p_c = jnp.exp(s_c - m)
            return m, jnp.sum(p_c, axis=1, keepdims=True), p_c.astype(BF16)
        s_p = jnp.where(prev_mask, sc[2] * scale, -jnp.inf)
        m = jnp.max(jnp.maximum(s_c, s_p), axis=1, keepdims=True)
        p_c = jnp.exp(s_c - m)
        p_p = jnp.exp(s_p - m)
        return m, jnp.sum(p_c + p_p, axis=1, keepdims=True), p_c.astype(BF16), p_p.astype(BF16)

    def values(sc, pr):
        o = _dot(pr[2], sc[1])
        if len(sc) > 2:
            o = o + _dot(pr[3], sc[3])
        return o

    def accumulate(block, pr, o):
        pi, cur, _ = block
        mb = jnp.broadcast_to(pr[0], (n, LANE))
        lb = jnp.broadcast_to(pr[1], (n, LANE))
        if pi == 0:
            acc_ref[cur, :] = o
            m_ref[cur, :] = mb
            l_ref[cur, :] = lb
        else:
            m_old = m_ref[cur, :]
            m_new = jnp.maximum(m_old, mb)
            w_old = jnp.exp(m_old - m_new)
            w_cur = jnp.exp(mb - m_new)
            acc_ref[cur, :] = acc_ref[cur, :] * w_old + o * w_cur
            l_ref[cur, :] = l_ref[cur, :] * w_old + lb * w_cur
            m_ref[cur, :] = m_new

    sc_of, pr_of, o_of = {}, {}, {}
    for step in range(len(waves) + 3):
        if step < len(waves):
            sc_of[step] = [scores(blk) for blk in waves[step]]
        w = step - 1
        if 0 <= w < len(waves):
            pr_of[w] = [softmax(sc) for sc in sc_of[w]]
        w = step - 2
        if 0 <= w < len(waves):
            o_of[w] = [values(sc, pr) for sc, pr in zip(sc_of.pop(w), pr_of[w])]
        w = step - 3
        if 0 <= w < len(waves):
            for blk, pr, o in zip(waves[w], pr_of.pop(w), o_of.pop(w)):
                accumulate(blk, pr, o)

    rows_out = 256

    def finish(i, carry):
        sl = pl.ds(pl.multiple_of(i * rows_out, rows_out), rows_out)
        out = acc_ref[sl, :] / l_ref[sl, :] * _silu(g_ref[0, sl, :])
        o_ref[0, sl, :] = out.astype(o_ref.dtype)
        return carry

    lax.fori_loop(0, seq // rows_out, finish, 0)


def _attn_prompt(z3, batch, seq):
    m = z3.shape[1]

    def spec(cb0):
        return pl.BlockSpec((1, seq, LANE), lambda b, h: (cb0 + h, b, 0))

    nat = pl.BlockSpec((seq, LANE), lambda b, h: (b, h))
    nat_shape = jax.ShapeDtypeStruct((m, A_HEADS * LANE), F32)
    return pl.pallas_call(
        functools.partial(_attn_prompt_body, seq=seq),
        out_shape=(jax.ShapeDtypeStruct((A_HEADS, m, LANE), BF16), nat_shape, nat_shape),
        grid=(batch, A_HEADS),
        in_specs=[spec(CB_QA), spec(CB_KA), spec(CB_VA), spec(CB_GA)],
        out_specs=(pl.BlockSpec((1, seq, LANE), lambda b, h: (h, b, 0)), nat, nat),
        scratch_shapes=[pltpu.VMEM((seq, LANE), F32)] * 3,
        compiler_params=_params("parallel", "parallel"),
        name="attn_prompt",
    )(z3, z3, z3, z3)


def _pool_prompt_body(u_ref, g_ref, w_ref, sc_ref, o_ref, *, seq):
    grp = pl.program_id(1)
    row = lax.broadcasted_iota(jnp.int32, (seq, 1), 0)
    for gi, width in enumerate(B_POOLS):
        @pl.when(grp == gi)
        def _(width=width):
            u = jnp.concatenate([u_ref[0], u_ref[1]], axis=1)
            s = u
            step = 1
            while step < width:
                s = s + jnp.where(row >= step, pltpu.roll(s, step, axis=0), 0.0)
                step *= 2
            cnt = jnp.minimum(row + 1, width).astype(F32)
            y = (s / cnt - u).astype(BF16)
            out = _dot(y, w_ref[0]) * sc_ref[0]
            gate = jnp.concatenate([g_ref[0], g_ref[1]], axis=1)
            out = (out * _silu(gate)).astype(o_ref.dtype)
            o_ref[0] = out[:, :LANE]
            o_ref[1] = out[:, LANE:]


def _pool_prompt(z3, pool_w, pool_scale, batch, seq):
    m = z3.shape[1]
    ngrp = len(B_POOLS)
    gw = pool_w.shape[1]
    cpg = gw // LANE

    def spec(cb0):
        return pl.BlockSpec((cpg, seq, LANE), lambda b, g: (cb0 // cpg + g, b, 0))

    return pl.pallas_call(
        functools.partial(_pool_prompt_body, seq=seq),
        out_shape=jax.ShapeDtypeStruct((ngrp * cpg, m, LANE), BF16),
        grid=(batch, ngrp),
        in_specs=[spec(CB_UB), spec(CB_GB),
                  pl.BlockSpec((1, gw, gw), lambda b, g: (g, 0, 0)),
                  pl.BlockSpec((1, 1, gw), lambda b, g: (g, 0, 0))],
        out_specs=pl.BlockSpec((cpg, seq, LANE), lambda b, g: (g, b, 0)),
        compiler_params=_params("parallel", "parallel"),
        name="pool_prompt",
    )(z3, z3, pool_w.astype(BF16), pool_scale.reshape(ngrp, 1, gw))


def _unit_lower_inverses(mats):
    c = mats[0].shape[0]
    eye = (lax.broadcasted_iota(jnp.int32, (c, c), 0) == lax.broadcasted_iota(jnp.int32, (c, c), 1)).astype(F32)
    xs = [eye - a for a in mats]
    pbs = [(-a).astype(BF16) for a in mats]
    k = 2
    while k < c:
        pbs = [_dot(pb, pb).astype(BF16) for pb in pbs]
        xs = [x + _dot(x.astype(BF16), pb) for x, pb in zip(xs, pbs)]
        k *= 2
    return xs


def _delta_prompt_body(q_ref, k_ref, v_ref, gate_ref, ab_ref, cw_ref, alog_ref, dtb_ref, nw_ref,
                       o_ref, s_ref, xbuf_ref, gct_ref, *, group):
    heads = C_HEADS
    c = q_ref.shape[1]
    tile = c
    hd = LANE

    @pl.when(pl.program_id(1) == 0)
    def _():
        s_ref[...] = jnp.zeros_like(s_ref)
        xbuf_ref[:, 0:CONV_PAD, :] = jnp.zeros((3 * heads, CONV_PAD, hd), F32)

    srcs = (q_ref, k_ref, v_ref)
    for s, ref in enumerate(srcs):
        for j in range(heads):
            xbuf_ref[s * heads + j, CONV_PAD:2 * CONV_PAD, :] = ref[j, 0:CONV_PAD, :]

    g = -jnp.exp(alog_ref[...]) * _softplus(ab_ref[:, 0:heads] + dtb_ref[...])
    beta = _sigmoid(ab_ref[:, heads:2 * heads])

    ri = lax.broadcasted_iota(jnp.int32, (c, c), 0)
    ci = lax.broadcasted_iota(jnp.int32, (c, c), 1)
    causal = ri >= ci
    strict = ri > ci
    gcum = _dot(causal.astype(F32), g, HIGHEST)
    eye_h = (lax.broadcasted_iota(jnp.int32, (heads, heads), 0)
             == lax.broadcasted_iota(jnp.int32, (heads, heads), 1)).astype(F32)
    gct_ref[...] = _dot_nt(eye_h, gcum, HIGHEST)
    lane_h = lax.broadcasted_iota(jnp.int32, (1, heads), 1)

    def conv(s, h):
        j = s * heads + h
        w = cw_ref[j]
        first = CONV_PAD - (C_CONV - 1)
        y_edge = xbuf_ref[j, pl.ds(first, CONV_PAD), :] * w[0:1]
        y_rest = srcs[s][h, pl.ds(first, tile - CONV_PAD), :] * w[0:1]
        for i in range(1, C_CONV):
            y_edge = y_edge + xbuf_ref[j, pl.ds(first + i, CONV_PAD), :] * w[i:i + 1]
            y_rest = y_rest + srcs[s][h, pl.ds(first + i, tile - CONV_PAD), :] * w[i:i + 1]
        return _silu(jnp.concatenate([y_edge, y_rest], axis=0))

    def l2n(x):
        return x * lax.rsqrt(jnp.sum(x * x, axis=-1, keepdims=True) + RMS_EPS)

    def group_body(hg, carry):
        hs = [hg * group + i for i in range(group)]
        states = [s_ref[0, h] for h in hs]
        pre = []
        for h in hs:
            onehot = lane_h == h
            q = l2n(conv(0, h)) * (hd ** -0.5)
            k = l2n(conv(1, h))
            v = conv(2, h)
            gcol = jnp.sum(jnp.where(onehot, gcum, 0.0), axis=1, keepdims=True)
            bcol = jnp.sum(jnp.where(onehot, beta, 0.0), axis=1, keepdims=True)
            grow = gct_ref[pl.ds(h, 1), :]
            dec = jnp.exp(jnp.where(causal, gcol - grow, -jnp.inf))
            kb = k * bcol
            kq = _dot_nt(jnp.concatenate([kb, q], axis=0).astype(BF16), k.astype(BF16))
            a_mat = jnp.where(strict, kq[:c] * dec, 0.0)
            qk = (kq[c:] * dec).astype(BF16)
            egc = jnp.exp(gcol)
            rhs = jnp.concatenate([v * bcol, kb * egc], axis=1).astype(BF16)
            glast = gcol[c - 1:c, :]
            kd = (k * jnp.exp(glast - gcol)).astype(BF16)
            pre.append((a_mat, qk, rhs, (q * egc).astype(BF16), kd, jnp.exp(glast)))
        invs = _unit_lower_inverses([p[0] for p in pre])
        sols = [_dot(t.astype(BF16), p[2]) for t, p in zip(invs, pre)]
        ws_qs = [_dot(jnp.concatenate([sol[:, hd:].astype(BF16), p[3]], axis=0), st.astype(BF16))
                 for sol, p, st in zip(sols, pre, states)]
        v_news = [(sol[:, :hd] - wq[:c]).astype(BF16) for sol, wq in zip(sols, ws_qs)]
        outs = [wq[c:] + _dot(p[1], vn) for wq, p, vn in zip(ws_qs, pre, v_news)]
        new_states = [st * p[5] + _dot_tn(p[4], vn) for st, p, vn in zip(states, pre, v_news)]
        for h, o in zip(hs, outs):
            on = o * lax.rsqrt(jnp.mean(o * o, axis=-1, keepdims=True) + RMS_EPS) * nw_ref[...]
            o_ref[h] = (on * _silu(gate_ref[h])).astype(o_ref.dtype)
        for h, state in zip(hs, new_states):
            s_ref[0, h] = state
        return carry

    lax.fori_loop(0, heads // group, group_body, 0)

    for s, ref in enumerate(srcs):
        for j in range(heads):
            xbuf_ref[s * heads + j, 0:CONV_PAD, :] = ref[j, tile - CONV_PAD:tile, :]


def _delta_prompt(z3, ab, conv_w3, a_log, dt_bias, norm_w, batch, seq, tile=128, group=16):
    m = z3.shape[1]
    heads = C_HEADS
    nt = seq // tile

    def spec(cb0):
        return pl.BlockSpec((heads, tile, LANE), lambda b, t: (cb0 // heads, b * nt + t, 0))

    def full(a):
        return pl.BlockSpec(a.shape, lambda b, t: (0,) * a.ndim)

    small = (conv_w3, a_log.reshape(1, heads), dt_bias.reshape(1, heads), norm_w.reshape(1, LANE))
    return pl.pallas_call(
        functools.partial(_delta_prompt_body, group=group),
        out_shape=(jax.ShapeDtypeStruct((heads, m, LANE), BF16),
                   jax.ShapeDtypeStruct((batch, heads, LANE, LANE), F32)),
        grid=(batch, nt),
        in_specs=[spec(CB_QC), spec(CB_KC), spec(CB_VC), spec(CB_GC),
                  pl.BlockSpec((tile, 2 * heads), lambda b, t: (b * nt + t, 0))] + [full(a) for a in small],
        out_specs=(pl.BlockSpec((heads, tile, LANE), lambda b, t: (0, b * nt + t, 0)),
                   pl.BlockSpec((1, heads, LANE, LANE), lambda b, t: (b, 0, 0, 0))),
        scratch_shapes=[pltpu.VMEM((3 * heads, 2 * CONV_PAD, LANE), F32),
                        pltpu.VMEM((heads, tile), F32)],
        compiler_params=_params("parallel", "arbitrary"),
        name="delta_prompt",
    )(z3, z3, z3, z3, ab, *small)


def _attn_sample_body(z_ref, *refs, npat):
    kv_refs, o_ref = refs[:2 * npat], refs[2 * npat]
    b = pl.program_id(0)
    heads = A_HEADS
    n = A_BLOCK

    @pl.when(b == 0)
    def _():
        o_ref[...] = jnp.zeros_like(o_ref)

    row = pl.ds(b, 1)

    def head_rows(cb0):
        return jnp.concatenate([z_ref[cb0 + h, row, :] for h in range(heads)], axis=0)

    q = head_rows(CB_QA).astype(BF16)
    kn = head_rows(CB_KA).astype(BF16).astype(F32)
    vn = head_rows(CB_VA).astype(BF16).astype(F32)
    gate = head_rows(CB_GA)
    scale = LANE ** -0.5
    qpad = jnp.concatenate([q, jnp.zeros((MXU_ROWS - heads, LANE), BF16)], axis=0)
    s_new = jnp.sum(q.astype(F32) * kn, axis=1, keepdims=True) * scale
    col = lax.broadcasted_iota(jnp.int32, (heads, n * heads), 1)
    own = (col & (heads - 1)) == lax.broadcasted_iota(jnp.int32, (heads, n * heads), 0)
    parts = []
    for i in range(npat):
        kmat = kv_refs[2 * i][...].reshape(n * heads, LANE).astype(BF16)
        vmat = kv_refs[2 * i + 1][...].reshape(n * heads, LANE).astype(BF16)
        s = jnp.where(own, _dot_nt(qpad, kmat)[:heads] * scale, -jnp.inf)
        m = jnp.maximum(jnp.max(s, axis=1, keepdims=True), s_new)
        p = jnp.exp(s - m)
        pn = jnp.exp(s_new - m)
        l = jnp.sum(p, axis=1, keepdims=True) + pn
        ppad = jnp.concatenate([p, jnp.zeros((MXU_ROWS - heads, n * heads), F32)], axis=0).astype(BF16)
        o = _dot(ppad, vmat)[:heads] + pn.astype(BF16).astype(F32) * vn
        parts.append((o, m, l))
    m_all = functools.reduce(jnp.maximum, [m for _, m, _ in parts])
    num = jnp.zeros((heads, LANE), F32)
    den = jnp.zeros((heads, 1), F32)
    for o, m, l in parts:
        wgt = jnp.exp(m - m_all)
        num = num + o * wgt
        den = den + l * wgt
    out = num / den * _silu(gate)
    for h in range(heads):
        o_ref[h, row, :] = out[h:h + 1]


def _attn_sample(z3s, cache_k, cache_v, layer):
    depth, nb, nbuf, heads, hd = cache_k.shape
    rows = z3s.shape[1]
    n = A_BLOCK
    assert heads == A_HEADS and hd == LANE
    assert all(nbuf % window == 0 for window, _ in A_PATTERNS), "cached window shorter than a pattern"
    views, specs = [], []
    for window, d in A_PATTERNS:
        last = nbuf // d // n - 1
        spec = pl.BlockSpec((None, None, n, None, heads, hd), lambda b, last=last: (layer, b, last, 0, 0, 0))
        for cache in (cache_k, cache_v):
            views.append(cache.reshape(depth, nb, nbuf // d, d, heads, hd))
            specs.append(spec)
    return pl.pallas_call(
        functools.partial(_attn_sample_body, npat=len(A_PATTERNS)),
        out_shape=jax.ShapeDtypeStruct((heads, rows, LANE), F32),
        grid=(nb,),
        in_specs=[pl.BlockSpec(z3s.shape, lambda b: (0, 0, 0))] + specs,
        out_specs=pl.BlockSpec((heads, rows, LANE), lambda b: (0, 0, 0)),
        compiler_params=_params("arbitrary"),
        name="attn_sample",
    )(z3s, *views)


def _state_sample_body(z_ref, ab_ref, pool_ref, conv_ref, st_ref, pw_ref, psc_ref, cw_ref, alog_ref, dtb_ref,
                       nw_ref, mb_ref, mc_ref, so_ref):
    b = pl.program_id(0)
    heads = C_HEADS
    hd = LANE

    @pl.when(b == 0)
    def _():
        mb_ref[...] = jnp.zeros_like(mb_ref)
        mc_ref[...] = jnp.zeros_like(mc_ref)

    row = pl.ds(b, 1)

    gw = pw_ref.shape[1]
    cpg = gw // LANE
    for gi, width in enumerate(B_POOLS):
        ys = []
        for j in range(cpg):
            cb = gi * cpg + j
            un = z_ref[CB_UB + cb, row, :]
            prev = pool_ref[0, B_BUF - (width - 1):B_BUF, cb * LANE:(cb + 1) * LANE]
            mean = (jnp.sum(prev, axis=0, keepdims=True) + un) / float(width)
            ys.append(mean - un)
        y = jnp.broadcast_to(jnp.concatenate(ys, axis=1), (MXU_ROWS, gw)).astype(BF16)
        out = _dot(y, pw_ref[gi])[0:1] * psc_ref[:, gi * gw:(gi + 1) * gw]
        for j in range(cpg):
            cb = gi * cpg + j
            mb_ref[cb, row, :] = out[:, j * LANE:(j + 1) * LANE] * _silu(z_ref[CB_GB + cb, row, :])

    arow = ab_ref[row, :]
    lane_ab = lax.broadcasted_iota(jnp.int32, (1, 2 * heads), 1)
    lane_h = lax.broadcasted_iota(jnp.int32, (1, heads), 1)
    eye = lax.broadcasted_iota(jnp.int32, (hd, hd), 0) == lax.broadcasted_iota(jnp.int32, (hd, hd), 1)

    def pick(vec, lanes, idx):
        return jnp.sum(jnp.where(lanes == idx, vec, 0.0), axis=1, keepdims=True)

    def column(x):
        return jnp.sum(jnp.where(eye, jnp.broadcast_to(x, (hd, hd)), 0.0), axis=1, keepdims=True)

    def conv(j):
        st = conv_ref[0, j]
        w = cw_ref[j]
        y = st[0:1] * w[0:1]
        y = y + st[1:2] * w[1:2]
        y = y + st[2:3] * w[2:3]
        y = y + z_ref[CB_QC + j, row, :] * w[3:4]
        return _silu(y)

    def l2n(x):
        return x * lax.rsqrt(jnp.sum(x * x, axis=-1, keepdims=True) + RMS_EPS)

    def head_body(h, carry):
        q = l2n(conv(h)) * (hd ** -0.5)
        k = l2n(conv(heads + h))
        v = conv(2 * heads + h)
        g = -jnp.exp(pick(alog_ref[...], lane_h, h)) * _softplus(pick(arow, lane_ab, h) + pick(dtb_ref[...], lane_h, h))
        beta = _sigmoid(pick(arow, lane_ab, heads + h))
        state = st_ref[0, h] * jnp.exp(g)
        kcol = column(k)
        delta = (v - jnp.sum(kcol * state, axis=0, keepdims=True)) * beta
        state = state + kcol * delta
        o = jnp.sum(column(q) * state, axis=0, keepdims=True)
        on = o * lax.rsqrt(jnp.mean(o * o, axis=-1, keepdims=True) + RMS_EPS) * nw_ref[...]
        mc_ref[h, row, :] = on * _silu(z_ref[CB_GC + h, row, :])
        so_ref[0, h] = state
        return carry

    lax.fori_loop(0, heads, head_body, 0)


def _state_sample(z3s, ab_s, state_pool, state_conv3, state_delta, pool_w, pool_scale, conv_w3, a_log, dt_bias,
                  norm_w):
    nb = state_pool.shape[0]
    rows = z3s.shape[1]
    heads = C_HEADS
    bw = state_pool.shape[2]

    def full(a):
        return pl.BlockSpec(a.shape, lambda b: (0,) * a.ndim)

    def per_b(a):
        return pl.BlockSpec((1,) + a.shape[1:], lambda b: (b,) + (0,) * (a.ndim - 1))

    small = (pool_w.astype(BF16), pool_scale.reshape(1, bw), conv_w3, a_log.reshape(1, heads),
             dt_bias.reshape(1, heads), norm_w.reshape(1, LANE))
    return pl.pallas_call(
        _state_sample_body,
        out_shape=(jax.ShapeDtypeStruct((bw // LANE, rows, LANE), F32),
                   jax.ShapeDtypeStruct((heads, rows, LANE), F32),
                   jax.ShapeDtypeStruct(state_delta.shape, F32)),
        grid=(nb,),
        in_specs=[full(z3s), full(ab_s), per_b(state_pool), per_b(state_conv3), per_b(state_delta)]
        + [full(a) for a in small],
        out_specs=(pl.BlockSpec((bw // LANE, rows, LANE), lambda b: (0, 0, 0)),
                   pl.BlockSpec((heads, rows, LANE), lambda b: (0, 0, 0)),
                   per_b(state_delta)),
        compiler_params=_params("arbitrary"),
        name="state_sample",
    )(z3s, ab_s, state_pool, state_conv3, state_delta, *small)


def _cache_shift_body(c_ref, n_ref, o_ref):
    nbuf = c_ref.shape[0]
    for i in range(0, nbuf - 1, CACHE_COPY_ROWS):
        rows = min(CACHE_COPY_ROWS, nbuf - 1 - i)
        o_ref[pl.ds(i, rows)] = c_ref[pl.ds(i + 1, rows)]
    o_ref[pl.ds(nbuf - 1, 1)] = n_ref[...]


def _cache_shift(cache, new_rows):
    depth, nb, nbuf, heads, hd = cache.shape
    return pl.pallas_call(
        _cache_shift_body,
        out_shape=jax.ShapeDtypeStruct(cache.shape, cache.dtype),
        grid=(depth, nb),
        in_specs=[pl.BlockSpec((None, None, nbuf, heads, hd), lambda l, b: (l, b, 0, 0, 0)),
                  pl.BlockSpec((None, None, 1, heads, hd), lambda l, b: (l, b, 0, 0, 0))],
        out_specs=pl.BlockSpec((None, None, nbuf, heads, hd), lambda l, b: (l, b, 0, 0, 0)),
        compiler_params=_params("parallel", "parallel"),
        name="cache_shift",
    )(cache, new_rows)


def _cols(z4, cb0, ncb, r0, r1):
    blk = z4[cb0:cb0 + ncb, :, r0:r1]
    return jnp.transpose(blk, (1, 2, 0, 3)).reshape(blk.shape[1], r1 - r0, ncb * LANE)


def kernel(x_prompt, x_sample, cache_win_k, cache_win_v, state_pool, state_conv, state_delta, norm_w, w_in,
           conv_w, a_log, dt_bias, delta_norm_w, pool_w, pool_scale, w_out, final_norm_w):
    batch, seq, d_model = x_prompt.shape
    nb = x_sample.shape[0]
    depth = w_in.shape[0]
    heads = C_HEADS
    m = batch * seq
    rows_s = 16
    nbuf = cache_win_k.shape[2]
    pool_cb = state_pool.shape[-1] // LANE

    hp = x_prompt.reshape(m, d_model)
    hs = jnp.zeros((rows_s, d_model), F32).at[:nb].set(x_sample.reshape(nb, d_model))

    assert nbuf == A_PATTERNS[-1][0], "the window cache is expected full: one row in, one row out"

    w_in_t = jnp.swapaxes(w_in, 1, 2)
    outs_p = [[] for _ in range(5)]
    outs_s = [[] for _ in range(5)]
    new_k, new_v = [], []
    for l in range(depth):
        conv_w3 = jnp.transpose(conv_w[l].reshape(C_CONV, 3 * heads, LANE), (1, 0, 2))

        h = _rmsnorm(hp, norm_w[l], BF16, 256)
        z3, ab = _inproj(h, w_in_t, l, 2048, 512)
        mix_a, k_nat, v_nat = _attn_prompt(z3, batch, seq)
        mix_b = _pool_prompt(z3, pool_w[l], pool_scale[l], batch, seq)
        mix_c, s_new = _delta_prompt(z3, ab, conv_w3, a_log[l], dt_bias[l], delta_norm_w[l], batch, seq)
        hp = _outproj(mix_a, mix_b, mix_c, w_out, l, hp, 2048, 256)

        z4 = z3.reshape(CB_TOTAL, batch, seq, LANE)
        keep_p = min(seq, A_PATTERNS[-1][0])
        outs_p[0].append(k_nat.reshape(batch, seq, A_HEADS, LANE)[:, seq - keep_p:])
        outs_p[1].append(v_nat.reshape(batch, seq, A_HEADS, LANE)[:, seq - keep_p:])
        outs_p[2].append(_cols(z4, CB_UB, pool_cb, seq - B_BUF, seq))
        outs_p[3].append(_cols(z4, CB_QC, 3 * heads, seq - (C_CONV - 1), seq))
        outs_p[4].append(s_new)

        h_s = _rmsnorm(hs, norm_w[l], BF16, rows_s)
        z3s, ab_s = _inproj(h_s, w_in_t, l, rows_s, 512)
        mix_as = _attn_sample(z3s, cache_win_k, cache_win_v, l)
        conv3 = jnp.transpose(state_conv[l].reshape(nb, C_CONV - 1, 3 * heads, LANE), (0, 2, 1, 3))
        mix_bs, mix_cs, st_new = _state_sample(z3s, ab_s, state_pool[l], conv3, state_delta[l], pool_w[l],
                                               pool_scale[l], conv_w3, a_log[l], dt_bias[l], delta_norm_w[l])
        hs = _outproj(mix_as, mix_bs, mix_cs, w_out, l, hs, rows_s, 512)

        z4s = z3s[:, :nb].reshape(CB_TOTAL, nb, 1, LANE)
        new_k.append(_cols(z4s, CB_KA, A_HEADS, 0, 1).reshape(nb, 1, A_HEADS, LANE))
        new_v.append(_cols(z4s, CB_VA, A_HEADS, 0, 1).reshape(nb, 1, A_HEADS, LANE))
        outs_s[2].append(jnp.concatenate([state_pool[l], _cols(z4s, CB_UB, pool_cb, 0, 1)], axis=1)[:, 1:])
        outs_s[3].append(jnp.concatenate([state_conv[l], _cols(z4s, CB_QC, 3 * heads, 0, 1)], axis=1)[:, 1:])
        outs_s[4].append(st_new)

    y_prompt = _rmsnorm(hp, final_norm_w, F32, 256).reshape(batch, seq, d_model)
    y_sample = _rmsnorm(hs, final_norm_w, F32, rows_s)[:nb].reshape(nb, 1, d_model)
    stack = lambda xs: jnp.stack(xs, axis=0)
    win_k = _cache_shift(cache_win_k, stack(new_k))
    win_v = _cache_shift(cache_win_v, stack(new_v))
    return ((y_prompt, y_sample) + tuple(stack(o) for o in outs_p) + (win_k, win_v)
            + tuple(stack(o) for o in outs_s[2:]))
```

```python
import functools

import jax
import jax.numpy as jnp
from jax import lax
from jax.experimental import pallas as pl
from jax.experimental.pallas import tpu as pltpu

F32 = jnp.float32
BF16 = jnp.bfloat16

LANE = 128
MXU_DEPTH = 256
ROW_CHUNK = 1024
MXU_ROWS = 16
VMEM_LIMIT = 56 * 2**20
RMS_EPS = 1e-6

A_HEADS = 8
A_PATTERNS = ((128, 1), (512, 4), (2048, 16))
A_BLOCK = 128
A_UNROLL = 4
B_POOLS = (2, 4, 8, 16)
B_BUF = 15
C_HEADS = 16
C_CONV = 4
CACHE_COPY_ROWS = 256
CONV_PAD = 8

CB_QA, CB_KA, CB_VA, CB_GA = 0, 8, 16, 24
CB_UB, CB_GB = 32, 40
CB_QC, CB_KC, CB_VC, CB_GC = 48, 64, 80, 96
CB_TOTAL = 112
N_MAIN = CB_TOTAL * LANE

HIGHEST = lax.Precision.HIGHEST
ROW_RESIDENT = pl.Buffered(1)


def _params(*sem):
    return pltpu.CompilerParams(dimension_semantics=sem, vmem_limit_bytes=VMEM_LIMIT)


def _sigmoid(x):
    return 1.0 / (1.0 + jnp.exp(-x))


def _silu(x):
    half = 0.5 * x
    return half + half * jnp.tanh(half)


def _softplus(x):
    return jnp.maximum(x, 0.0) + jnp.log(1.0 + jnp.exp(-jnp.abs(x)))


def _dot(a, b, precision=None):
    return jnp.dot(a, b, preferred_element_type=F32, precision=precision)


def _dot_nt(a, b, precision=None):
    return lax.dot_general(a, b, (((1,), (1,)), ((), ())), preferred_element_type=F32, precision=precision)


def _dot_tn(a, b, precision=None):
    return lax.dot_general(a, b, (((0,), (0,)), ((), ())), preferred_element_type=F32, precision=precision)


def _rms_body(x_ref, g_ref, o_ref):
    x = x_ref[...]
    ms = jnp.mean(x * x, axis=-1, keepdims=True)
    o_ref[...] = (x * lax.rsqrt(ms + RMS_EPS) * g_ref[...]).astype(o_ref.dtype)


def _rmsnorm(x, gain, out_dtype, tm):
    m, d = x.shape
    return pl.pallas_call(
        _rms_body,
        out_shape=jax.ShapeDtypeStruct((m, d), out_dtype),
        grid=(m // tm,),
        in_specs=[pl.BlockSpec((tm, d), lambda i: (i, 0)), pl.BlockSpec((1, d), lambda i: (0, 0))],
        out_specs=pl.BlockSpec((tm, d), lambda i: (i, 0)),
        compiler_params=_params("parallel"),
        name="rmsnorm",
    )(x, gain.reshape(1, d))


def _inproj_body(h_ref, hs_ref, w_ref, wab_ref, z_ref, ab_ref, zs_ref, abs_ref):
    i, j = pl.program_id(0), pl.program_id(1)
    tm = h_ref.shape[0]
    w = w_ref[...].astype(BF16)
    for r0 in range(0, tm, ROW_CHUNK):
        rows = pl.ds(r0, min(ROW_CHUNK, tm - r0))
        h = h_ref[rows, :]
        acc = _dot_nt(h, w)
        for c in range(acc.shape[1] // LANE):
            z_ref[c, rows, :] = acc[:, c * LANE:(c + 1) * LANE]

        @pl.when(j == 0)
        def _():
            ab_ref[rows, :] = _dot_nt(h, wab_ref[...].astype(BF16))

    @pl.when(i == 0)
    def _():
        hs = hs_ref[...]
        acc_s = _dot_nt(hs, w)
        for c in range(acc_s.shape[1] // LANE):
            zs_ref[c] = acc_s[:, c * LANE:(c + 1) * LANE]

        @pl.when(j == 0)
        def _():
            abs_ref[...] = _dot_nt(hs, wab_ref[...].astype(BF16))


def _inproj(h, hs, w_t, layer, tm, tn):
    m, d = h.shape
    ms = hs.shape[0]
    nab = w_t.shape[1] - N_MAIN
    nj = N_MAIN // tn
    assert N_MAIN % nab == 0
    return pl.pallas_call(
        _inproj_body,
        out_shape=(jax.ShapeDtypeStruct((CB_TOTAL, m, LANE), F32),
                   jax.ShapeDtypeStruct((m, nab), F32),
                   jax.ShapeDtypeStruct((CB_TOTAL, ms, LANE), F32),
                   jax.ShapeDtypeStruct((ms, nab), F32)),
        grid=(m // tm, nj),
        in_specs=[pl.BlockSpec((tm, d), lambda i, j: (i, 0), pipeline_mode=ROW_RESIDENT),
                  pl.BlockSpec((ms, d), lambda i, j: (0, 0)),
                  pl.BlockSpec((None, tn, d), lambda i, j: (layer, j, 0)),
                  pl.BlockSpec((None, nab, d), lambda i, j: (layer, N_MAIN // nab, 0))],
        out_specs=(pl.BlockSpec((tn // LANE, tm, LANE), lambda i, j: (j, i, 0)),
                   pl.BlockSpec((tm, nab), lambda i, j: (i, 0)),
                   pl.BlockSpec((tn // LANE, ms, LANE), lambda i, j: (jnp.where(i == 0, j, nj - 1), 0, 0)),
                   pl.BlockSpec((ms, nab), lambda i, j: (0, 0))),
        compiler_params=_params("arbitrary", "arbitrary"),
        name="inproj",
    )(h, hs, w_t, w_t)


def _outproj_body(ma_ref, mb_ref, mc_ref, w_ref, x_ref, o_ref):
    w = w_ref[...].astype(BF16)
    tm = o_ref.shape[0]
    blocks = [(ref, c) for ref in (ma_ref, mb_ref, mc_ref) for c in range(ref.shape[0])]
    for r0 in range(0, tm, ROW_CHUNK):
        rows = pl.ds(r0, min(ROW_CHUNK, tm - r0))
        acc = x_ref[rows, :]
        for p in range(0, len(blocks), MXU_DEPTH // LANE):
            lhs = jnp.concatenate([ref[c, rows, :].astype(BF16) for ref, c in blocks[p:p + MXU_DEPTH // LANE]], axis=1)
            acc = acc + _dot(lhs, w[p * LANE:p * LANE + MXU_DEPTH, :])
        o_ref[rows, :] = acc


def _outproj(mix_a, mix_b, mix_c, w_all, layer, x, tm, tn):
    m, d = x.shape
    k = w_all.shape[1]

    def mix_spec(a):
        return pl.BlockSpec((a.shape[0], tm, LANE), lambda i, j: (0, i, 0), pipeline_mode=ROW_RESIDENT)

    return pl.pallas_call(
        _outproj_body,
        out_shape=jax.ShapeDtypeStruct((m, d), F32),
        grid=(m // tm, d // tn),
        in_specs=[mix_spec(mix_a), mix_spec(mix_b), mix_spec(mix_c),
                  pl.BlockSpec((None, k, tn), lambda i, j: (layer, 0, j)),
                  pl.BlockSpec((tm, tn), lambda i, j: (i, j))],
        out_specs=pl.BlockSpec((tm, tn), lambda i, j: (i, j)),
        compiler_params=_params("parallel", "arbitrary"),
        name="outproj",
    )(mix_a, mix_b, mix_c, w_all, x)


def _attn_prompt_body(q_ref, k_ref, v_ref, g_ref, o_ref, kn_ref, vn_ref, acc_ref, m_ref, l_ref, *, seq):
    n = A_BLOCK
    scale = LANE ** -0.5
    kn_ref[...] = k_ref[0]
    vn_ref[...] = v_ref[0]
    qi = lax.broadcasted_iota(jnp.int32, (n, n), 0)
    kj = lax.broadcasted_iota(jnp.int32, (n, n), 1)
    cur_mask = kj <= qi
    prev_mask = kj >= qi

    def rows(t0, d):
        return pl.ds(t0, n) if d == 1 else pl.ds(t0, n, stride=d)

    blocks = []
    for pi, (window, d) in enumerate(sorted(A_PATTERNS, key=lambda p: -p[1])):
        for r in range(d):
            for blk in range(seq // window):
                t0 = blk * window + r
                blocks.append((pi, rows(t0, d), rows(t0 - window, d) if blk > 0 else None))
    waves = [blocks[i:i + A_UNROLL] for i in range(0, len(blocks), A_UNROLL)]

    def scores(block):
        _, cur, prev = block
        q = q_ref[0, cur, :].astype(BF16)
        sc = [_dot_nt(q, k_ref[0, cur, :].astype(BF16)), v_ref[0, cur, :].astype(BF16)]
        if prev is not None:
            sc += [_dot_nt(q, k_ref[0, prev, :].astype(BF16)), v_ref[0, prev, :].astype(BF16)]
        return sc

    def softmax(sc):
        s_c = jnp.where(cur_mask, sc[0] * scale, -jnp.inf)
        if len(sc) == 2:
            m = jnp.max(s_c, axis=1, keepdims=True)
            p_c = jnp.exp(s_c - m)
            return m, jnp.sum(p_c, axis=1, keepdims=True), p_c.astype(BF16)
        s_p = jnp.where(prev_mask, sc[2] * scale, -jnp.inf)
        m = jnp.max(jnp.maximum(s_c, s_p), axis=1, keepdims=True)
        p_c = jnp.exp(s_c - m)
        p_p = jnp.exp(s_p - m)
        return m, jnp.sum(p_c + p_p, axis=1, keepdims=True), p_c.astype(BF16), p_p.astype(BF16)

    def values(sc, pr):
        o = _dot(pr[2], sc[1])
        if len(sc) > 2:
            o = o + _dot(pr[3], sc[3])
        return o

    def accumulate(block, pr, o):
        pi, cur, _ = block
        mb = jnp.broadcast_to(pr[0], (n, LANE))
        lb = jnp.broadcast_to(pr[1], (n, LANE))
        if pi == 0:
            acc_ref[cur, :] = o
            m_ref[cur, :] = mb
            l_ref[cur, :] = lb
        else:
            m_old = m_ref[cur, :]
            m_new = jnp.maximum(m_old, mb)
            w_old = jnp.exp(m_old - m_new)
            w_cur = jnp.exp(mb - m_new)
            acc_ref[cur, :] = acc_ref[cur, :] * w_old + o * w_cur
            l_ref[cur, :] = l_ref[cur, :] * w_old + lb * w_cur
            m_ref[cur, :] = m_new

    sc_of, pr_of, o_of = {}, {}, {}
    for step in range(len(waves) + 3):
        if step < len(waves):
            sc_of[step] = [scores(blk) for blk in waves[step]]
        w = step - 1
        if 0 <= w < len(waves):
            pr_of[w] = [softmax(sc) for sc in sc_of[w]]
        w = step - 2
        if 0 <= w < len(waves):
            o_of[w] = [values(sc, pr) for sc, pr in zip(sc_of.pop(w), pr_of[w])]
        w = step - 3
        if 0 <= w < len(waves):
            for blk, pr, o in zip(waves[w], pr_of.pop(w), o_of.pop(w)):
                accumulate(blk, pr, o)

    rows_out = 256

    def finish(i, carry):
        sl = pl.ds(pl.multiple_of(i * rows_out, rows_out), rows_out)
        out = acc_ref[sl, :] / l_ref[sl, :] * _silu(g_ref[0, sl, :])
        o_ref[0, sl, :] = out.astype(o_ref.dtype)
        return carry

    lax.fori_loop(0, seq // rows_out, finish, 0)


def _attn_prompt(z3, batch, seq):
    m = z3.shape[1]

    def spec(cb0):
        return pl.BlockSpec((1, seq, LANE), lambda b, h: (cb0 + h, b, 0))

    nat = pl.BlockSpec((seq, LANE), lambda b, h: (b, h))
    nat_shape = jax.ShapeDtypeStruct((m, A_HEADS * LANE), F32)
    return pl.pallas_call(
        functools.partial(_attn_prompt_body, seq=seq),
        out_shape=(jax.ShapeDtypeStruct((A_HEADS, m, LANE), BF16), nat_shape, nat_shape),
        grid=(batch, A_HEADS),
        in_specs=[spec(CB_QA), spec(CB_KA), spec(CB_VA), spec(CB_GA)],
        out_specs=(pl.BlockSpec((1, seq, LANE), lambda b, h: (h, b, 0)), nat, nat),
        scratch_shapes=[pltpu.VMEM((seq, LANE), F32)] * 3,
        compiler_params=_params("parallel", "parallel"),
        name="attn_prompt",
    )(z3, z3, z3, z3)


def _pool_prompt_body(u_ref, g_ref, w_ref, sc_ref, o_ref, *, seq):
    grp = pl.program_id(1)
    row = lax.broadcasted_iota(jnp.int32, (seq, 1), 0)
    for gi, width in enumerate(B_POOLS):
        @pl.when(grp == gi)
        def _(width=width):
            u = jnp.concatenate([u_ref[0], u_ref[1]], axis=1)
            s = u
            step = 1
            while step < width:
                s = s + jnp.where(row >= step, pltpu.roll(s, step, axis=0), 0.0)
                step *= 2
            cnt = jnp.minimum(row + 1, width).astype(F32)
            y = (s / cnt - u).astype(BF16)
            out = _dot(y, w_ref[0]) * sc_ref[0]
            gate = jnp.concatenate([g_ref[0], g_ref[1]], axis=1)
            out = (out * _silu(gate)).astype(o_ref.dtype)
            o_ref[0] = out[:, :LANE]
            o_ref[1] = out[:, LANE:]


def _pool_prompt(z3, pool_w, pool_scale, batch, seq):
    m = z3.shape[1]
    ngrp = len(B_POOLS)
    gw = pool_w.shape[1]
    cpg = gw // LANE

    def spec(cb0):
        return pl.BlockSpec((cpg, seq, LANE), lambda b, g: (cb0 // cpg + g, b, 0))

    return pl.pallas_call(
        functools.partial(_pool_prompt_body, seq=seq),
        out_shape=jax.ShapeDtypeStruct((ngrp * cpg, m, LANE), BF16),
        grid=(batch, ngrp),
        in_specs=[spec(CB_UB), spec(CB_GB),
                  pl.BlockSpec((1, gw, gw), lambda b, g: (g, 0, 0)),
                  pl.BlockSpec((1, 1, gw), lambda b, g: (g, 0, 0))],
        out_specs=pl.BlockSpec((cpg, seq, LANE), lambda b, g: (g, b, 0)),
        compiler_params=_params("parallel", "parallel"),
        name="pool_prompt",
    )(z3, z3, pool_w.astype(BF16), pool_scale.reshape(ngrp, 1, gw))


def _unit_lower_inverses(mats):
    c = mats[0].shape[0]
    eye = (lax.broadcasted_iota(jnp.int32, (c, c), 0) == lax.broadcasted_iota(jnp.int32, (c, c), 1)).astype(F32)
    xs = [eye - a for a in mats]
    pbs = [(-a).astype(BF16) for a in mats]
    k = 2
    while k < c:
        pbs = [_dot(pb, pb).astype(BF16) for pb in pbs]
        xs = [x + _dot(x.astype(BF16), pb) for x, pb in zip(xs, pbs)]
        k *= 2
    return xs


def _delta_prompt_body(q_ref, k_ref, v_ref, gate_ref, ab_ref, cw_ref, alog_ref, dtb_ref, nw_ref,
                       o_ref, s_ref, xbuf_ref, gct_ref, *, group):
    heads = C_HEADS
    c = q_ref.shape[1]
    tile = c
    hd = LANE

    @pl.when(pl.program_id(1) == 0)
    def _():
        s_ref[...] = jnp.zeros_like(s_ref)
        xbuf_ref[:, 0:CONV_PAD, :] = jnp.zeros((3 * heads, CONV_PAD, hd), F32)

    srcs = (q_ref, k_ref, v_ref)
    for s, ref in enumerate(srcs):
        for j in range(heads):
            xbuf_ref[s * heads + j, CONV_PAD:2 * CONV_PAD, :] = ref[j, 0:CONV_PAD, :]

    g = -jnp.exp(alog_ref[...]) * _softplus(ab_ref[:, 0:heads] + dtb_ref[...])
    beta = _sigmoid(ab_ref[:, heads:2 * heads])

    ri = lax.broadcasted_iota(jnp.int32, (c, c), 0)
    ci = lax.broadcasted_iota(jnp.int32, (c, c), 1)
    causal = ri >= ci
    strict = ri > ci
    gcum = _dot(causal.astype(F32), g, HIGHEST)
    eye_h = (lax.broadcasted_iota(jnp.int32, (heads, heads), 0)
             == lax.broadcasted_iota(jnp.int32, (heads, heads), 1)).astype(F32)
    gct_ref[...] = _dot_nt(eye_h, gcum, HIGHEST)
    lane_h = lax.broadcasted_iota(jnp.int32, (1, heads), 1)

    def conv(s, h):
        j = s * heads + h
        w = cw_ref[j]
        first = CONV_PAD - (C_CONV - 1)
        y_edge = xbuf_ref[j, pl.ds(first, CONV_PAD), :] * w[0:1]
        y_rest = srcs[s][h, pl.ds(first, tile - CONV_PAD), :] * w[0:1]
        for i in range(1, C_CONV):
            y_edge = y_edge + xbuf_ref[j, pl.ds(first + i, CONV_PAD), :] * w[i:i + 1]
            y_rest = y_rest + srcs[s][h, pl.ds(first + i, tile - CONV_PAD), :] * w[i:i + 1]
        return _silu(jnp.concatenate([y_edge, y_rest], axis=0))

    def l2n(x):
        return x * lax.rsqrt(jnp.sum(x * x, axis=-1, keepdims=True) + RMS_EPS)

    def group_body(hg, carry):
        hs = [hg * group + i for i in range(group)]
        states = [s_ref[0, h] for h in hs]
        pre = []
        for h in hs:
            onehot = lane_h == h
            q = l2n(conv(0, h)) * (hd ** -0.5)
            k = l2n(conv(1, h))
            v = conv(2, h)
            gcol = jnp.sum(jnp.where(onehot, gcum, 0.0), axis=1, keepdims=True)
            bcol = jnp.sum(jnp.where(onehot, beta, 0.0), axis=1, keepdims=True)
            grow = gct_ref[pl.ds(h, 1), :]
            dec = jnp.exp(jnp.where(causal, gcol - grow, -jnp.inf))
            kb = k * bcol
            kq = _dot_nt(jnp.concatenate([kb, q], axis=0).astype(BF16), k.astype(BF16))
            a_mat = jnp.where(strict, kq[:c] * dec, 0.0)
            qk = (kq[c:] * dec).astype(BF16)
            egc = jnp.exp(gcol)
            rhs = jnp.concatenate([v * bcol, kb * egc], axis=1).astype(BF16)
            glast = gcol[c - 1:c, :]
            kd = (k * jnp.exp(glast - gcol)).astype(BF16)
            pre.append((a_mat, qk, rhs, (q * egc).astype(BF16), kd, jnp.exp(glast)))
        invs = _unit_lower_inverses([p[0] for p in pre])
        sols = [_dot(t.astype(BF16), p[2]) for t, p in zip(invs, pre)]
        ws_qs = [_dot(jnp.concatenate([sol[:, hd:].astype(BF16), p[3]], axis=0), st.astype(BF16))
                 for sol, p, st in zip(sols, pre, states)]
        v_news = [(sol[:, :hd] - wq[:c]).astype(BF16) for sol, wq in zip(sols, ws_qs)]
        outs = [wq[c:] + _dot(p[1], vn) for wq, p, vn in zip(ws_qs, pre, v_news)]
        new_states = [st * p[5] + _dot_tn(p[4], vn) for st, p, vn in zip(states, pre, v_news)]
        for h, o in zip(hs, outs):
            on = o * lax.rsqrt(jnp.mean(o * o, axis=-1, keepdims=True) + RMS_EPS) * nw_ref[...]
            o_ref[h] = (on * _silu(gate_ref[h])).astype(o_ref.dtype)
        for h, state in zip(hs, new_states):
            s_ref[0, h] = state
        return carry

    lax.fori_loop(0, heads // group, group_body, 0)

    for s, ref in enumerate(srcs):
        for j in range(heads):
            xbuf_ref[s * heads + j, 0:CONV_PAD, :] = ref[j, tile - CONV_PAD:tile, :]


def _delta_prompt(z3, ab, conv_w3, a_log, dt_bias, norm_w, batch, seq, tile=128, group=16):
    m = z3.shape[1]
    heads = C_HEADS
    nt = seq // tile

    def spec(cb0):
        return pl.BlockSpec((heads, tile, LANE), lambda b, t: (cb0 // heads, b * nt + t, 0))

    def full(a):
        return pl.BlockSpec(a.shape, lambda b, t: (0,) * a.ndim)

    small = (conv_w3, a_log.reshape(1, heads), dt_bias.reshape(1, heads), norm_w.reshape(1, LANE))
    return pl.pallas_call(
        functools.partial(_delta_prompt_body, group=group),
        out_shape=(jax.ShapeDtypeStruct((heads, m, LANE), BF16),
                   jax.ShapeDtypeStruct((batch, heads, LANE, LANE), F32)),
        grid=(batch, nt),
        in_specs=[spec(CB_QC), spec(CB_KC), spec(CB_VC), spec(CB_GC),
                  pl.BlockSpec((tile, 2 * heads), lambda b, t: (b * nt + t, 0))] + [full(a) for a in small],
        out_specs=(pl.BlockSpec((heads, tile, LANE), lambda b, t: (0, b * nt + t, 0)),
                   pl.BlockSpec((1, heads, LANE, LANE), lambda b, t: (b, 0, 0, 0))),
        scratch_shapes=[pltpu.VMEM((3 * heads, 2 * CONV_PAD, LANE), F32),
                        pltpu.VMEM((heads, tile), F32)],
        compiler_params=_params("parallel", "arbitrary"),
        name="delta_prompt",
    )(z3, z3, z3, z3, ab, *small)


def _attn_sample_body(z_ref, *refs, npat):
    kv_refs, o_ref = refs[:2 * npat], refs[2 * npat]
    b = pl.program_id(0)
    heads = A_HEADS
    n = A_BLOCK

    @pl.when(b == 0)
    def _():
        o_ref[...] = jnp.zeros_like(o_ref)

    row = pl.ds(b, 1)

    def head_rows(cb0):
        return jnp.concatenate([z_ref[cb0 + h, row, :] for h in range(heads)], axis=0)

    q = head_rows(CB_QA).astype(BF16)
    kn = head_rows(CB_KA).astype(BF16).astype(F32)
    vn = head_rows(CB_VA).astype(BF16).astype(F32)
    gate = head_rows(CB_GA)
    scale = LANE ** -0.5
    qpad = jnp.concatenate([q, jnp.zeros((MXU_ROWS - heads, LANE), BF16)], axis=0)
    s_new = jnp.sum(q.astype(F32) * kn, axis=1, keepdims=True) * scale
    col = lax.broadcasted_iota(jnp.int32, (heads, n * heads), 1)
    own = (col & (heads - 1)) == lax.broadcasted_iota(jnp.int32, (heads, n * heads), 0)
    parts = []
    for i in range(npat):
        kmat = kv_refs[2 * i][...].reshape(n * heads, LANE).astype(BF16)
        vmat = kv_refs[2 * i + 1][...].reshape(n * heads, LANE).astype(BF16)
        s = jnp.where(own, _dot_nt(qpad, kmat)[:heads] * scale, -jnp.inf)
        m = jnp.maximum(jnp.max(s, axis=1, keepdims=True), s_new)
        p = jnp.exp(s - m)
        pn = jnp.exp(s_new - m)
        l = jnp.sum(p, axis=1, keepdims=True) + pn
        ppad = jnp.concatenate([p, jnp.zeros((MXU_ROWS - heads, n * heads), F32)], axis=0).astype(BF16)
        o = _dot(ppad, vmat)[:heads] + pn.astype(BF16).astype(F32) * vn
        parts.append((o, m, l))
    m_all = functools.reduce(jnp.maximum, [m for _, m, _ in parts])
    num = jnp.zeros((heads, LANE), F32)
    den = jnp.zeros((heads, 1), F32)
    for o, m, l in parts:
        wgt = jnp.exp(m - m_all)
        num = num + o * wgt
        den = den + l * wgt
    out = num / den * _silu(gate)
    for h in range(heads):
        o_ref[h, row, :] = out[h:h + 1]


def _attn_sample(z3s, cache_k, cache_v, layer):
    depth, nb, nbuf, heads, hd = cache_k.shape
    rows = z3s.shape[1]
    n = A_BLOCK
    assert heads == A_HEADS and hd == LANE
    assert all(nbuf % window == 0 for window, _ in A_PATTERNS), "cached window shorter than a pattern"
    views, specs = [], []
    for window, d in A_PATTERNS:
        last = nbuf // d // n - 1
        spec = pl.BlockSpec((None, None, n, None, heads, hd), lambda b, last=last: (layer, b, last, 0, 0, 0))
        for cache in (cache_k, cache_v):
            views.append(cache.reshape(depth, nb, nbuf // d, d, heads, hd))
            specs.append(spec)
    return pl.pallas_call(
        functools.partial(_attn_sample_body, npat=len(A_PATTERNS)),
        out_shape=jax.ShapeDtypeStruct((heads, rows, LANE), F32),
        grid=(nb,),
        in_specs=[pl.BlockSpec(z3s.shape, lambda b: (0, 0, 0))] + specs,
        out_specs=pl.BlockSpec((heads, rows, LANE), lambda b: (0, 0, 0)),
        compiler_params=_params("arbitrary"),
        name="attn_sample",
    )(z3s, *views)


def _state_sample_body(z_ref, ab_ref, pool_ref, conv_ref, st_ref, pw_ref, psc_ref, cw_ref, alog_ref, dtb_ref,
                       nw_ref, mb_ref, mc_ref, so_ref):
    b = pl.program_id(0)
    heads = C_HEADS
    hd = LANE

    @pl.when(b == 0)
    def _():
        mb_ref[...] = jnp.zeros_like(mb_ref)
        mc_ref[...] = jnp.zeros_like(mc_ref)

    row = pl.ds(b, 1)

    gw = pw_ref.shape[1]
    cpg = gw // LANE
    for gi, width in enumerate(B_POOLS):
        ys = []
        for j in range(cpg):
            cb = gi * cpg + j
            un = z_ref[CB_UB + cb, row, :]
            prev = pool_ref[0, B_BUF - (width - 1):B_BUF, cb * LANE:(cb + 1) * LANE]
            mean = (jnp.sum(prev, axis=0, keepdims=True) + un) / float(width)
            ys.append(mean - un)
        y = jnp.broadcast_to(jnp.concatenate(ys, axis=1), (MXU_ROWS, gw)).astype(BF16)
        out = _dot(y, pw_ref[gi])[0:1] * psc_ref[:, gi * gw:(gi + 1) * gw]
        for j in range(cpg):
            cb = gi * cpg + j
            mb_ref[cb, row, :] = out[:, j * LANE:(j + 1) * LANE] * _silu(z_ref[CB_GB + cb, row, :])

    arow = ab_ref[row, :]
    lane_ab = lax.broadcasted_iota(jnp.int32, (1, 2 * heads), 1)
    lane_h = lax.broadcasted_iota(jnp.int32, (1, heads), 1)
    eye = lax.broadcasted_iota(jnp.int32, (hd, hd), 0) == lax.broadcasted_iota(jnp.int32, (hd, hd), 1)

    def pick(vec, lanes, idx):
        return jnp.sum(jnp.where(lanes == idx, vec, 0.0), axis=1, keepdims=True)

    def column(x):
        return jnp.sum(jnp.where(eye, jnp.broadcast_to(x, (hd, hd)), 0.0), axis=1, keepdims=True)

    def conv(j):
        st = conv_ref[0, j]
        w = cw_ref[j]
        y = st[0:1] * w[0:1]
        y = y + st[1:2] * w[1:2]
        y = y + st[2:3] * w[2:3]
        y = y + z_ref[CB_QC + j, row, :] * w[3:4]
        return _silu(y)

    def l2n(x):
        return x * lax.rsqrt(jnp.sum(x * x, axis=-1, keepdims=True) + RMS_EPS)

    def head_body(h, carry):
        q = l2n(conv(h)) * (hd ** -0.5)
        k = l2n(conv(heads + h))
        v = conv(2 * heads + h)
        g = -jnp.exp(pick(alog_ref[...], lane_h, h)) * _softplus(pick(arow, lane_ab, h) + pick(dtb_ref[...], lane_h, h))
        beta = _sigmoid(pick(arow, lane_ab, heads + h))
        state = st_ref[0, h] * jnp.exp(g)
        kcol = column(k)
        delta = (v - jnp.sum(kcol * state, axis=0, keepdims=True)) * beta
        state = state + kcol * delta
        o = jnp.sum(column(q) * state, axis=0, keepdims=True)
        on = o * lax.rsqrt(jnp.mean(o * o, axis=-1, keepdims=True) + RMS_EPS) * nw_ref[...]
        mc_ref[h, row, :] = on * _silu(z_ref[CB_GC + h, row, :])
        so_ref[0, h] = state
        return carry

    lax.fori_loop(0, heads, head_body, 0)


def _state_sample(z3s, ab_s, state_pool, state_conv3, state_delta, pool_w, pool_scale, conv_w3, a_log, dt_bias,
                  norm_w):
    nb = state_pool.shape[0]
    rows = z3s.shape[1]
    heads = C_HEADS
    bw = state_pool.shape[2]

    def full(a):
        return pl.BlockSpec(a.shape, lambda b: (0,) * a.ndim)

    def per_b(a):
        return pl.BlockSpec((1,) + a.shape[1:], lambda b: (b,) + (0,) * (a.ndim - 1))

    small = (pool_w.astype(BF16), pool_scale.reshape(1, bw), conv_w3, a_log.reshape(1, heads),
             dt_bias.reshape(1, heads), norm_w.reshape(1, LANE))
    return pl.pallas_call(
        _state_sample_body,
        out_shape=(jax.ShapeDtypeStruct((bw // LANE, rows, LANE), F32),
                   jax.ShapeDtypeStruct((heads, rows, LANE), F32),
                   jax.ShapeDtypeStruct(state_delta.shape, F32)),
        grid=(nb,),
        in_specs=[full(z3s), full(ab_s), per_b(state_pool), per_b(state_conv3), per_b(state_delta)]
        + [full(a) for a in small],
        out_specs=(pl.BlockSpec((bw // LANE, rows, LANE), lambda b: (0, 0, 0)),
                   pl.BlockSpec((heads, rows, LANE), lambda b: (0, 0, 0)),
                   per_b(state_delta)),
        compiler_params=_params("arbitrary"),
        name="state_sample",
    )(z3s, ab_s, state_pool, state_conv3, state_delta, *small)


def _cache_shift_body(c_ref, n_ref, o_ref):
    nbuf = c_ref.shape[0]
    for i in range(0, nbuf - 1, CACHE_COPY_ROWS):
        rows = min(CACHE_COPY_ROWS, nbuf - 1 - i)
        o_ref[pl.ds(i, rows)] = c_ref[pl.ds(i + 1, rows)]
    o_ref[pl.ds(nbuf - 1, 1)] = n_ref[...]


def _cache_shift(cache, new_rows):
    depth, nb, nbuf, heads, hd = cache.shape
    return pl.pallas_call(
        _cache_shift_body,
        out_shape=jax.ShapeDtypeStruct(cache.shape, cache.dtype),
        grid=(depth, nb),
        in_specs=[pl.BlockSpec((None, None, nbuf, heads, hd), lambda l, b: (l, b, 0, 0, 0)),
                  pl.BlockSpec((None, None, 1, heads, hd), lambda l, b: (l, b, 0, 0, 0))],
        out_specs=pl.BlockSpec((None, None, nbuf, heads, hd), lambda l, b: (l, b, 0, 0, 0)),
        compiler_params=_params("parallel", "parallel"),
        name="cache_shift",
    )(cache, new_rows)


def _cols(z4, cb0, ncb, r0, r1):
    blk = z4[cb0:cb0 + ncb, :, r0:r1]
    return jnp.transpose(blk, (1, 2, 0, 3)).reshape(blk.shape[1], r1 - r0, ncb * LANE)


def kernel(x_prompt, x_sample, cache_win_k, cache_win_v, state_pool, state_conv, state_delta, norm_w, w_in,
           conv_w, a_log, dt_bias, delta_norm_w, pool_w, pool_scale, w_out, final_norm_w):
    batch, seq, d_model = x_prompt.shape
    nb = x_sample.shape[0]
    depth = w_in.shape[0]
    heads = C_HEADS
    m = batch * seq
    rows_s = 16
    nbuf = cache_win_k.shape[2]
    pool_cb = state_pool.shape[-1] // LANE

    hp = x_prompt.reshape(m, d_model)
    hs = jnp.zeros((rows_s, d_model), F32).at[:nb].set(x_sample.reshape(nb, d_model))

    assert nbuf == A_PATTERNS[-1][0], "the window cache is expected full: one row in, one row out"

    w_in_t = jnp.swapaxes(w_in, 1, 2)
    outs_p = [[] for _ in range(5)]
    outs_s = [[] for _ in range(5)]
    new_k, new_v = [], []
    for l in range(depth):
        conv_w3 = jnp.transpose(conv_w[l].reshape(C_CONV, 3 * heads, LANE), (1, 0, 2))

        h = _rmsnorm(hp, norm_w[l], BF16, 256)
        h_s = _rmsnorm(hs, norm_w[l], BF16, rows_s)
        z3, ab, z3s, ab_s = _inproj(h, h_s, w_in_t, l, 2048, 512)
        mix_a, k_nat, v_nat = _attn_prompt(z3, batch, seq)
        mix_b = _pool_prompt(z3, pool_w[l], pool_scale[l], batch, seq)
        mix_c, s_new = _delta_prompt(z3, ab, conv_w3, a_log[l], dt_bias[l], delta_norm_w[l], batch, seq)
        hp = _outproj(mix_a, mix_b, mix_c, w_out, l, hp, 2048, 512)

        z4 = z3.reshape(CB_TOTAL, batch, seq, LANE)
        keep_p = min(seq, A_PATTERNS[-1][0])
        outs_p[0].append(k_nat.reshape(batch, seq, A_HEADS, LANE)[:, seq - keep_p:])
        outs_p[1].append(v_nat.reshape(batch, seq, A_HEADS, LANE)[:, seq - keep_p:])
        outs_p[2].append(_cols(z4, CB_UB, pool_cb, seq - B_BUF, seq))
        outs_p[3].append(_cols(z4, CB_QC, 3 * heads, seq - (C_CONV - 1), seq))
        outs_p[4].append(s_new)

        mix_as = _attn_sample(z3s, cache_win_k, cache_win_v, l)
        conv3 = jnp.transpose(state_conv[l].reshape(nb, C_CONV - 1, 3 * heads, LANE), (0, 2, 1, 3))
        mix_bs, mix_cs, st_new = _state_sample(z3s, ab_s, state_pool[l], conv3, state_delta[l], pool_w[l],
                                               pool_scale[l], conv_w3, a_log[l], dt_bias[l], delta_norm_w[l])
        hs = _outproj(mix_as, mix_bs, mix_cs, w_out, l, hs, rows_s, 512)

        z4s = z3s[:, :nb].reshape(CB_TOTAL, nb, 1, LANE)
        new_k.append(_cols(z4s, CB_KA, A_HEADS, 0, 1).reshape(nb, 1, A_HEADS, LANE))
        new_v.append(_cols(z4s, CB_VA, A_HEADS, 0, 1).reshape(nb, 1, A_HEADS, LANE))
        outs_s[2].append(jnp.concatenate([state_pool[l], _cols(z4s, CB_UB, pool_cb, 0, 1)], axis=1)[:, 1:])
        outs_s[3].append(jnp.concatenate([state_conv[l], _cols(z4s, CB_QC, 3 * heads, 0, 1)], axis=1)[:, 1:])
        outs_s[4].append(st_new)

    y_prompt = _rmsnorm(hp, final_norm_w, F32, 256).reshape(batch, seq, d_model)
    y_sample = _rmsnorm(hs, final_norm_w, F32, rows_s)[:nb].reshape(nb, 1, d_model)
    stack = lambda xs: jnp.stack(xs, axis=0)
    win_k = _cache_shift(cache_win_k, stack(new_k))
    win_v = _cache_shift(cache_win_v, stack(new_v))
    return ((y_prompt, y_sample) + tuple(stack(o) for o in outs_p) + (win_k, win_v)
            + tuple(stack(o) for o in outs_s[2:]))
```

```python
import functools

import jax
import jax.numpy as jnp
from jax import lax
from jax.experimental import pallas as pl
from jax.experimental.pallas import tpu as pltpu

F32 = jnp.float32
BF16 = jnp.bfloat16

LANE = 128
MXU_DEPTH = 256
ROW_CHUNK = 1024
MXU_ROWS = 16
VMEM_LIMIT = 56 * 2**20
RMS_EPS = 1e-6

A_HEADS = 8
A_PATTERNS = ((128, 1), (512, 4), (2048, 16))
A_BLOCK = 128
A_UNROLL = 4
B_POOLS = (2, 4, 8, 16)
B_BUF = 15
C_HEADS = 16
C_CONV = 4
CACHE_COPY_ROWS = 256
CONV_PAD = 8

CB_QA, CB_KA, CB_VA, CB_GA = 0, 8, 16, 24
CB_UB, CB_GB = 32, 40
CB_QC, CB_KC, CB_VC, CB_GC = 48, 64, 80, 96
CB_TOTAL = 112
N_MAIN = CB_TOTAL * LANE

HIGHEST = lax.Precision.HIGHEST
ROW_RESIDENT = pl.Buffered(1)


def _params(*sem):
    return pltpu.CompilerParams(dimension_semantics=sem, vmem_limit_bytes=VMEM_LIMIT)


def _sigmoid(x):
    return 1.0 / (1.0 + jnp.exp(-x))


def _silu(x):
    half = 0.5 * x
    return half + half * jnp.tanh(half)


def _softplus(x):
    return jnp.maximum(x, 0.0) + jnp.log(1.0 + jnp.exp(-jnp.abs(x)))


def _dot(a, b, precision=None):
    return jnp.dot(a, b, preferred_element_type=F32, precision=precision)


def _dot_nt(a, b, precision=None):
    return lax.dot_general(a, b, (((1,), (1,)), ((), ())), preferred_element_type=F32, precision=precision)


def _dot_tn(a, b, precision=None):
    return lax.dot_general(a, b, (((0,), (0,)), ((), ())), preferred_element_type=F32, precision=precision)


def _rms_body(x_ref, g_ref, o_ref):
    x = x_ref[...]
    ms = jnp.mean(x * x, axis=-1, keepdims=True)
    o_ref[...] = (x * lax.rsqrt(ms + RMS_EPS) * g_ref[...]).astype(o_ref.dtype)


def _rmsnorm(x, gain, out_dtype, tm):
    m, d = x.shape
    return pl.pallas_call(
        _rms_body,
        out_shape=jax.ShapeDtypeStruct((m, d), out_dtype),
        grid=(m // tm,),
        in_specs=[pl.BlockSpec((tm, d), lambda i: (i, 0)), pl.BlockSpec((1, d), lambda i: (0, 0))],
        out_specs=pl.BlockSpec((tm, d), lambda i: (i, 0)),
        compiler_params=_params("parallel"),
        name="rmsnorm",
    )(x, gain.reshape(1, d))


def _inproj_body(h_ref, hs_ref, w_ref, wab_ref, z_ref, ab_ref, zs_ref, abs_ref):
    i, j = pl.program_id(0), pl.program_id(1)
    tm = h_ref.shape[0]
    w = w_ref[...].astype(BF16)
    for r0 in range(0, tm, ROW_CHUNK):
        rows = pl.ds(r0, min(ROW_CHUNK, tm - r0))
        h = h_ref[rows, :]
        acc = _dot_nt(h, w)
        for c in range(acc.shape[1] // LANE):
            z_ref[c, rows, :] = acc[:, c * LANE:(c + 1) * LANE]

        @pl.when(j == 0)
        def _():
            ab_ref[rows, :] = _dot_nt(h, wab_ref[...].astype(BF16))

    @pl.when(i == 0)
    def _():
        hs = hs_ref[...]
        acc_s = _dot_nt(hs, w)
        for c in range(acc_s.shape[1] // LANE):
            zs_ref[c] = acc_s[:, c * LANE:(c + 1) * LANE]

        @pl.when(j == 0)
        def _():
            abs_ref[...] = _dot_nt(hs, wab_ref[...].astype(BF16))


def _inproj(h, hs, w_t, layer, tm, tn):
    m, d = h.shape
    ms = hs.shape[0]
    nab = w_t.shape[1] - N_MAIN
    nj = N_MAIN // tn
    assert N_MAIN % nab == 0
    return pl.pallas_call(
        _inproj_body,
        out_shape=(jax.ShapeDtypeStruct((CB_TOTAL, m, LANE), F32),
                   jax.ShapeDtypeStruct((m, nab), F32),
                   jax.ShapeDtypeStruct((CB_TOTAL, ms, LANE), F32),
                   jax.ShapeDtypeStruct((ms, nab), F32)),
        grid=(m // tm, nj),
        in_specs=[pl.BlockSpec((tm, d), lambda i, j: (i, 0), pipeline_mode=ROW_RESIDENT),
                  pl.BlockSpec((ms, d), lambda i, j: (0, 0)),
                  pl.BlockSpec((None, tn, d), lambda i, j: (layer, j, 0)),
                  pl.BlockSpec((None, nab, d), lambda i, j: (layer, N_MAIN // nab, 0))],
        out_specs=(pl.BlockSpec((tn // LANE, tm, LANE), lambda i, j: (j, i, 0)),
                   pl.BlockSpec((tm, nab), lambda i, j: (i, 0)),
                   pl.BlockSpec((tn // LANE, ms, LANE), lambda i, j: (jnp.where(i == 0, j, nj - 1), 0, 0)),
                   pl.BlockSpec((ms, nab), lambda i, j: (0, 0))),
        compiler_params=_params("arbitrary", "arbitrary"),
        name="inproj",
    )(h, hs, w_t, w_t)


def _outproj_body(ma_ref, mb_ref, mc_ref, w_ref, x_ref, o_ref):
    w = w_ref[...].astype(BF16)
    tm = o_ref.shape[0]
    blocks = [(ref, c) for ref in (ma_ref, mb_ref, mc_ref) for c in range(ref.shape[0])]
    for r0 in range(0, tm, ROW_CHUNK):
        rows = pl.ds(r0, min(ROW_CHUNK, tm - r0))
        acc = x_ref[rows, :]
        for p in range(0, len(blocks), MXU_DEPTH // LANE):
            lhs = jnp.concatenate([ref[c, rows, :].astype(BF16) for ref, c in blocks[p:p + MXU_DEPTH // LANE]], axis=1)
            acc = acc + _dot(lhs, w[p * LANE:p * LANE + MXU_DEPTH, :])
        o_ref[rows, :] = acc


def _outproj(mix_a, mix_b, mix_c, w_all, layer, x, tm, tn):
    m, d = x.shape
    k = w_all.shape[1]

    def mix_spec(a):
        return pl.BlockSpec((a.shape[0], tm, LANE), lambda i, j: (0, i, 0), pipeline_mode=ROW_RESIDENT)

    return pl.pallas_call(
        _outproj_body,
        out_shape=jax.ShapeDtypeStruct((m, d), F32),
        grid=(m // tm, d // tn),
        in_specs=[mix_spec(mix_a), mix_spec(mix_b), mix_spec(mix_c),
                  pl.BlockSpec((None, k, tn), lambda i, j: (layer, 0, j)),
                  pl.BlockSpec((tm, tn), lambda i, j: (i, j))],
        out_specs=pl.BlockSpec((tm, tn), lambda i, j: (i, j)),
        compiler_params=_params("parallel", "arbitrary"),
        name="outproj",
    )(mix_a, mix_b, mix_c, w_all, x)


def _attn_prompt_body(q_ref, k_ref, v_ref, g_ref, o_ref, kn_ref, vn_ref, acc_ref, m_ref, l_ref, *, seq):
    n = A_BLOCK
    scale = LANE ** -0.5
    kn_ref[...] = k_ref[0]
    vn_ref[...] = v_ref[0]
    qi = lax.broadcasted_iota(jnp.int32, (n, n), 0)
    kj = lax.broadcasted_iota(jnp.int32, (n, n), 1)
    cur_mask = kj <= qi
    prev_mask = kj >= qi

    def rows(t0, d):
        return pl.ds(t0, n) if d == 1 else pl.ds(t0, n, stride=d)

    blocks = []
    for pi, (window, d) in enumerate(sorted(A_PATTERNS, key=lambda p: -p[1])):
        for r in range(d):
            for blk in range(seq // window):
                t0 = blk * window + r
                blocks.append((pi, rows(t0, d), rows(t0 - window, d) if blk > 0 else None))
    waves = [blocks[i:i + A_UNROLL] for i in range(0, len(blocks), A_UNROLL)]

    def scores(block):
        _, cur, prev = block
        q = q_ref[0, cur, :].astype(BF16)
        sc = [_dot_nt(q, k_ref[0, cur, :].astype(BF16)), v_ref[0, cur, :].astype(BF16)]
        if prev is not None:
            sc += [_dot_nt(q, k_ref[0, prev, :].astype(BF16)), v_ref[0, prev, :].astype(BF16)]
        return sc

    def softmax(sc):
        s_c = jnp.where(cur_mask, sc[0] * scale, -jnp.inf)
        if len(sc) == 2:
            m = jnp.max(s_c, axis=1, keepdims=True)
            p_c = jnp.exp(s_c - m)
            return m, jnp.sum(p_c, axis=1, keepdims=True), p_c.astype(BF16)
        s_p = jnp.where(prev_mask, sc[2] * scale, -jnp.inf)
        m = jnp.max(jnp.maximum(s_c, s_p), axis=1, keepdims=True)
        p_c = jnp.exp(s_c - m)
        p_p = jnp.exp(s_p - m)
        return m, jnp.sum(p_c + p_p, axis=1, keepdims=True), p_c.astype(BF16), p_p.astype(BF16)

    def values(sc, pr):
        o = _dot(pr[2], sc[1])
        if len(sc) > 2:
            o = o + _dot(pr[3], sc[3])
        return o

    def accumulate(block, pr, o):
        pi, cur, _ = block
        mb = jnp.broadcast_to(pr[0], (n, LANE))
        lb = jnp.broadcast_to(pr[1], (n, LANE))
        if pi == 0:
            acc_ref[cur, :] = o
            m_ref[cur, :] = mb
            l_ref[cur, :] = lb
        else:
            m_old = m_ref[cur, :]
            m_new = jnp.maximum(m_old, mb)
            w_old = jnp.exp(m_old - m_new)
            w_cur = jnp.exp(mb - m_new)
            acc_ref[cur, :] = acc_ref[cur, :] * w_old + o * w_cur
            l_ref[cur, :] = l_ref[cur, :] * w_old + lb * w_cur
            m_ref[cur, :] = m_new

    sc_of, pr_of, o_of = {}, {}, {}
    for step in range(len(waves) + 3):
        if step < len(waves):
            sc_of[step] = [scores(blk) for blk in waves[step]]
        w = step - 1
        if 0 <= w < len(waves):
            pr_of[w] = [softmax(sc) for sc in sc_of[w]]
        w = step - 2
        if 0 <= w < len(waves):
            o_of[w] = [values(sc, pr) for sc, pr in zip(sc_of.pop(w), pr_of[w])]
        w = step - 3
        if 0 <= w < len(waves):
            for blk, pr, o in zip(waves[w], pr_of.pop(w), o_of.pop(w)):
                accumulate(blk, pr, o)

    rows_out = 256

    def finish(i, carry):
        sl = pl.ds(pl.multiple_of(i * rows_out, rows_out), rows_out)
        out = acc_ref[sl, :] / l_ref[sl, :] * _silu(g_ref[0, sl, :])
        o_ref[0, sl, :] = out.astype(o_ref.dtype)
        return carry

    lax.fori_loop(0, seq // rows_out, finish, 0)


def _attn_prompt(z3, batch, seq):
    m = z3.shape[1]

    def spec(cb0):
        return pl.BlockSpec((1, seq, LANE), lambda b, h: (cb0 + h, b, 0))

    nat = pl.BlockSpec((seq, LANE), lambda b, h: (b, h))
    nat_shape = jax.ShapeDtypeStruct((m, A_HEADS * LANE), F32)
    return pl.pallas_call(
        functools.partial(_attn_prompt_body, seq=seq),
        out_shape=(jax.ShapeDtypeStruct((A_HEADS, m, LANE), BF16), nat_shape, nat_shape),
        grid=(batch, A_HEADS),
        in_specs=[spec(CB_QA), spec(CB_KA), spec(CB_VA), spec(CB_GA)],
        out_specs=(pl.BlockSpec((1, seq, LANE), lambda b, h: (h, b, 0)), nat, nat),
        scratch_shapes=[pltpu.VMEM((seq, LANE), F32)] * 3,
        compiler_params=_params("parallel", "parallel"),
        name="attn_prompt",
    )(z3, z3, z3, z3)


def _pool_prompt_body(u_ref, g_ref, w_ref, sc_ref, o_ref, *, seq):
    grp = pl.program_id(1)
    row = lax.broadcasted_iota(jnp.int32, (seq, 1), 0)
    for gi, width in enumerate(B_POOLS):
        @pl.when(grp == gi)
        def _(width=width):
            u = jnp.concatenate([u_ref[0], u_ref[1]], axis=1)
            s = u
            step = 1
            while step < width:
                s = s + jnp.where(row >= step, pltpu.roll(s, step, axis=0), 0.0)
                step *= 2
            cnt = jnp.minimum(row + 1, width).astype(F32)
            y = (s / cnt - u).astype(BF16)
            out = _dot(y, w_ref[0]) * sc_ref[0]
            gate = jnp.concatenate([g_ref[0], g_ref[1]], axis=1)
            out = (out * _silu(gate)).astype(o_ref.dtype)
            o_ref[0] = out[:, :LANE]
            o_ref[1] = out[:, LANE:]


def _pool_prompt(z3, pool_w, pool_scale, batch, seq):
    m = z3.shape[1]
    ngrp = len(B_POOLS)
    gw = pool_w.shape[1]
    cpg = gw // LANE

    def spec(cb0):
        return pl.BlockSpec((cpg, seq, LANE), lambda b, g: (cb0 // cpg + g, b, 0))

    return pl.pallas_call(
        functools.partial(_pool_prompt_body, seq=seq),
        out_shape=jax.ShapeDtypeStruct((ngrp * cpg, m, LANE), BF16),
        grid=(batch, ngrp),
        in_specs=[spec(CB_UB), spec(CB_GB),
                  pl.BlockSpec((1, gw, gw), lambda b, g: (g, 0, 0)),
                  pl.BlockSpec((1, 1, gw), lambda b, g: (g, 0, 0))],
        out_specs=pl.BlockSpec((cpg, seq, LANE), lambda b, g: (g, b, 0)),
        compiler_params=_params("parallel", "parallel"),
        name="pool_prompt",
    )(z3, z3, pool_w.astype(BF16), pool_scale.reshape(ngrp, 1, gw))


def _unit_lower_inverses(mats):
    c = mats[0].shape[0]
    eye = (lax.broadcasted_iota(jnp.int32, (c, c), 0) == lax.broadcasted_iota(jnp.int32, (c, c), 1)).astype(F32)
    xs = [eye - a for a in mats]
    pbs = [(-a).astype(BF16) for a in mats]
    k = 2
    while k < c:
        pbs = [_dot(pb, pb).astype(BF16) for pb in pbs]
        xs = [x + _dot(x.astype(BF16), pb) for x, pb in zip(xs, pbs)]
        k *= 2
    return xs


def _shift_half_window(c_ref, o_ref, second_half):
    half = o_ref.shape[0]

    @pl.when(jnp.logical_not(second_half))
    def _():
        for i in range(0, half, CACHE_COPY_ROWS):
            o_ref[pl.ds(i, CACHE_COPY_ROWS)] = c_ref[pl.ds(i, CACHE_COPY_ROWS)]

    @pl.when(second_half)
    def _():
        for i in range(0, half, CACHE_COPY_ROWS):
            rows = min(CACHE_COPY_ROWS, half - 1 - i)
            o_ref[pl.ds(i, rows)] = c_ref[pl.ds(i + 1, rows)]
        o_ref[pl.ds(half - 1, 1)] = c_ref[pl.ds(half - 1, 1)]


def _delta_prompt_body(q_ref, k_ref, v_ref, gate_ref, ab_ref, cw_ref, alog_ref, dtb_ref, nw_ref, cache_ref,
                       o_ref, s_ref, shifted_ref, xbuf_ref, gct_ref, *, group, steps_per_copy):
    heads = C_HEADS
    c = q_ref.shape[1]
    tile = c
    hd = LANE

    step = pl.program_id(0) * pl.num_programs(1) + pl.program_id(1)

    @pl.when(step % steps_per_copy == 0)
    def _():
        _shift_half_window(cache_ref.at[0, 0], shifted_ref, (step // steps_per_copy) % 2 == 1)

    @pl.when(pl.program_id(1) == 0)
    def _():
        s_ref[...] = jnp.zeros_like(s_ref)
        xbuf_ref[:, 0:CONV_PAD, :] = jnp.zeros((3 * heads, CONV_PAD, hd), F32)

    srcs = (q_ref, k_ref, v_ref)
    for s, ref in enumerate(srcs):
        for j in range(heads):
            xbuf_ref[s * heads + j, CONV_PAD:2 * CONV_PAD, :] = ref[j, 0:CONV_PAD, :]

    g = -jnp.exp(alog_ref[...]) * _softplus(ab_ref[:, 0:heads] + dtb_ref[...])
    beta = _sigmoid(ab_ref[:, heads:2 * heads])

    ri = lax.broadcasted_iota(jnp.int32, (c, c), 0)
    ci = lax.broadcasted_iota(jnp.int32, (c, c), 1)
    causal = ri >= ci
    strict = ri > ci
    gcum = _dot(causal.astype(F32), g, HIGHEST)
    eye_h = (lax.broadcasted_iota(jnp.int32, (heads, heads), 0)
             == lax.broadcasted_iota(jnp.int32, (heads, heads), 1)).astype(F32)
    gct_ref[...] = _dot_nt(eye_h, gcum, HIGHEST)
    lane_h = lax.broadcasted_iota(jnp.int32, (1, heads), 1)

    def conv(s, h):
        j = s * heads + h
        w = cw_ref[j]
        first = CONV_PAD - (C_CONV - 1)
        y_edge = xbuf_ref[j, pl.ds(first, CONV_PAD), :] * w[0:1]
        y_rest = srcs[s][h, pl.ds(first, tile - CONV_PAD), :] * w[0:1]
        for i in range(1, C_CONV):
            y_edge = y_edge + xbuf_ref[j, pl.ds(first + i, CONV_PAD), :] * w[i:i + 1]
            y_rest = y_rest + srcs[s][h, pl.ds(first + i, tile - CONV_PAD), :] * w[i:i + 1]
        return _silu(jnp.concatenate([y_edge, y_rest], axis=0))

    def l2n(x):
        return x * lax.rsqrt(jnp.sum(x * x, axis=-1, keepdims=True) + RMS_EPS)

    def group_body(hg, carry):
        hs = [hg * group + i for i in range(group)]
        states = [s_ref[0, h] for h in hs]
        pre = []
        for h in hs:
            onehot = lane_h == h
            q = l2n(conv(0, h)) * (hd ** -0.5)
            k = l2n(conv(1, h))
            v = conv(2, h)
            gcol = jnp.sum(jnp.where(onehot, gcum, 0.0), axis=1, keepdims=True)
            bcol = jnp.sum(jnp.where(onehot, beta, 0.0), axis=1, keepdims=True)
            grow = gct_ref[pl.ds(h, 1), :]
            dec = jnp.exp(jnp.where(causal, gcol - grow, -jnp.inf))
            kb = k * bcol
            kq = _dot_nt(jnp.concatenate([kb, q], axis=0).astype(BF16), k.astype(BF16))
            a_mat = jnp.where(strict, kq[:c] * dec, 0.0)
            qk = (kq[c:] * dec).astype(BF16)
            egc = jnp.exp(gcol)
            rhs = jnp.concatenate([v * bcol, kb * egc], axis=1).astype(BF16)
            glast = gcol[c - 1:c, :]
            kd = (k * jnp.exp(glast - gcol)).astype(BF16)
            pre.append((a_mat, qk, rhs, (q * egc).astype(BF16), kd, jnp.exp(glast)))
        invs = _unit_lower_inverses([p[0] for p in pre])
        sols = [_dot(t.astype(BF16), p[2]) for t, p in zip(invs, pre)]
        ws_qs = [_dot(jnp.concatenate([sol[:, hd:].astype(BF16), p[3]], axis=0), st.astype(BF16))
                 for sol, p, st in zip(sols, pre, states)]
        v_news = [(sol[:, :hd] - wq[:c]).astype(BF16) for sol, wq in zip(sols, ws_qs)]
        outs = [wq[c:] + _dot(p[1], vn) for wq, p, vn in zip(ws_qs, pre, v_news)]
        new_states = [st * p[5] + _dot_tn(p[4], vn) for st, p, vn in zip(states, pre, v_news)]
        for h, o in zip(hs, outs):
            on = o * lax.rsqrt(jnp.mean(o * o, axis=-1, keepdims=True) + RMS_EPS) * nw_ref[...]
            o_ref[h] = (on * _silu(gate_ref[h])).astype(o_ref.dtype)
        for h, state in zip(hs, new_states):
            s_ref[0, h] = state
        return carry

    lax.fori_loop(0, heads // group, group_body, 0)

    for s, ref in enumerate(srcs):
        for j in range(heads):
            xbuf_ref[s * heads + j, 0:CONV_PAD, :] = ref[j, tile - CONV_PAD:tile, :]


def _delta_prompt(z3, ab, conv_w3, a_log, dt_bias, norm_w, cache, batch, seq, tile=128, group=16):
    m = z3.shape[1]
    heads = C_HEADS
    nt = seq // tile
    depth, nb, nbuf = cache.shape[:3]
    half = nbuf // 2
    ncopy = depth * nb * 2
    assert (batch * nt) % ncopy == 0 and half % CACHE_COPY_ROWS == 0
    steps_per_copy = batch * nt // ncopy

    def spec(cb0):
        return pl.BlockSpec((heads, tile, LANE), lambda b, t: (cb0 // heads, b * nt + t, 0))

    def full(a):
        return pl.BlockSpec(a.shape, lambda b, t: (0,) * a.ndim)

    def window(b, t):
        cidx = (b * nt + t) // steps_per_copy
        return cidx // (2 * nb), (cidx // 2) % nb, cidx % 2

    def cache_in(b, t):
        l, w, hf = window(b, t)
        return l, w, jnp.where(hf == 0, 1, half), 0, 0

    def cache_out(b, t):
        l, w, hf = window(b, t)
        return l, w, hf, 0, 0

    small = (conv_w3, a_log.reshape(1, heads), dt_bias.reshape(1, heads), norm_w.reshape(1, LANE))
    return pl.pallas_call(
        functools.partial(_delta_prompt_body, group=group, steps_per_copy=steps_per_copy),
        out_shape=(jax.ShapeDtypeStruct((heads, m, LANE), BF16),
                   jax.ShapeDtypeStruct((batch, heads, LANE, LANE), F32),
                   jax.ShapeDtypeStruct(cache.shape, cache.dtype)),
        grid=(batch, nt),
        in_specs=[spec(CB_QC), spec(CB_KC), spec(CB_VC), spec(CB_GC),
                  pl.BlockSpec((tile, 2 * heads), lambda b, t: (b * nt + t, 0))] + [full(a) for a in small]
        + [pl.BlockSpec(tuple(pl.Element(n) for n in (1, 1, half) + cache.shape[3:]), cache_in)],
        out_specs=(pl.BlockSpec((heads, tile, LANE), lambda b, t: (0, b * nt + t, 0)),
                   pl.BlockSpec((1, heads, LANE, LANE), lambda b, t: (b, 0, 0, 0)),
                   pl.BlockSpec((None, None, half) + cache.shape[3:], cache_out)),
        scratch_shapes=[pltpu.VMEM((3 * heads, 2 * CONV_PAD, LANE), F32),
                        pltpu.VMEM((heads, tile), F32)],
        compiler_params=_params("arbitrary", "arbitrary"),
        name="delta_prompt",
    )(z3, z3, z3, z3, ab, *small, cache)


def _attn_sample_body(z_ref, *refs, npat):
    kv_refs, o_ref = refs[:2 * npat], refs[2 * npat]
    b = pl.program_id(0)
    heads = A_HEADS
    n = A_BLOCK

    @pl.when(b == 0)
    def _():
        o_ref[...] = jnp.zeros_like(o_ref)

    row = pl.ds(b, 1)

    def head_rows(cb0):
        return jnp.concatenate([z_ref[cb0 + h, row, :] for h in range(heads)], axis=0)

    q = head_rows(CB_QA).astype(BF16)
    kn = head_rows(CB_KA).astype(BF16).astype(F32)
    vn = head_rows(CB_VA).astype(BF16).astype(F32)
    gate = head_rows(CB_GA)
    scale = LANE ** -0.5
    qpad = jnp.concatenate([q, jnp.zeros((MXU_ROWS - heads, LANE), BF16)], axis=0)
    s_new = jnp.sum(q.astype(F32) * kn, axis=1, keepdims=True) * scale
    col = lax.broadcasted_iota(jnp.int32, (heads, n * heads), 1)
    own = (col & (heads - 1)) == lax.broadcasted_iota(jnp.int32, (heads, n * heads), 0)
    parts = []
    for i in range(npat):
        kmat = kv_refs[2 * i][...].reshape(n * heads, LANE).astype(BF16)
        vmat = kv_refs[2 * i + 1][...].reshape(n * heads, LANE).astype(BF16)
        s = jnp.where(own, _dot_nt(qpad, kmat)[:heads] * scale, -jnp.inf)
        m = jnp.maximum(jnp.max(s, axis=1, keepdims=True), s_new)
        p = jnp.exp(s - m)
        pn = jnp.exp(s_new - m)
        l = jnp.sum(p, axis=1, keepdims=True) + pn
        ppad = jnp.concatenate([p, jnp.zeros((MXU_ROWS - heads, n * heads), F32)], axis=0).astype(BF16)
        o = _dot(ppad, vmat)[:heads] + pn.astype(BF16).astype(F32) * vn
        parts.append((o, m, l))
    m_all = functools.reduce(jnp.maximum, [m for _, m, _ in parts])
    num = jnp.zeros((heads, LANE), F32)
    den = jnp.zeros((heads, 1), F32)
    for o, m, l in parts:
        wgt = jnp.exp(m - m_all)
        num = num + o * wgt
        den = den + l * wgt
    out = num / den * _silu(gate)
    for h in range(heads):
        o_ref[h, row, :] = out[h:h + 1]


def _attn_sample(z3s, cache_k, cache_v, layer):
    depth, nb, nbuf, heads, hd = cache_k.shape
    rows = z3s.shape[1]
    n = A_BLOCK
    assert heads == A_HEADS and hd == LANE
    assert all(nbuf % window == 0 for window, _ in A_PATTERNS), "cached window shorter than a pattern"
    views, specs = [], []
    for window, d in A_PATTERNS:
        last = nbuf // d // n - 1
        spec = pl.BlockSpec((None, None, n, None, heads, hd), lambda b, last=last: (layer, b, last, 0, 0, 0))
        for cache in (cache_k, cache_v):
            views.append(cache.reshape(depth, nb, nbuf // d, d, heads, hd))
            specs.append(spec)
    return pl.pallas_call(
        functools.partial(_attn_sample_body, npat=len(A_PATTERNS)),
        out_shape=jax.ShapeDtypeStruct((heads, rows, LANE), F32),
        grid=(nb,),
        in_specs=[pl.BlockSpec(z3s.shape, lambda b: (0, 0, 0))] + specs,
        out_specs=pl.BlockSpec((heads, rows, LANE), lambda b: (0, 0, 0)),
        compiler_params=_params("arbitrary"),
        name="attn_sample",
    )(z3s, *views)


def _state_sample_body(z_ref, ab_ref, pool_ref, conv_ref, st_ref, pw_ref, psc_ref, cw_ref, alog_ref, dtb_ref,
                       nw_ref, mb_ref, mc_ref, so_ref):
    b = pl.program_id(0)
    heads = C_HEADS
    hd = LANE

    @pl.when(b == 0)
    def _():
        mb_ref[...] = jnp.zeros_like(mb_ref)
        mc_ref[...] = jnp.zeros_like(mc_ref)

    row = pl.ds(b, 1)

    gw = pw_ref.shape[1]
    cpg = gw // LANE
    for gi, width in enumerate(B_POOLS):
        ys = []
        for j in range(cpg):
            cb = gi * cpg + j
            un = z_ref[CB_UB + cb, row, :]
            prev = pool_ref[0, B_BUF - (width - 1):B_BUF, cb * LANE:(cb + 1) * LANE]
            mean = (jnp.sum(prev, axis=0, keepdims=True) + un) / float(width)
            ys.append(mean - un)
        y = jnp.broadcast_to(jnp.concatenate(ys, axis=1), (MXU_ROWS, gw)).astype(BF16)
        out = _dot(y, pw_ref[gi])[0:1] * psc_ref[:, gi * gw:(gi + 1) * gw]
        for j in range(cpg):
            cb = gi * cpg + j
            mb_ref[cb, row, :] = out[:, j * LANE:(j + 1) * LANE] * _silu(z_ref[CB_GB + cb, row, :])

    arow = ab_ref[row, :]
    lane_ab = lax.broadcasted_iota(jnp.int32, (1, 2 * heads), 1)
    lane_h = lax.broadcasted_iota(jnp.int32, (1, heads), 1)
    eye = lax.broadcasted_iota(jnp.int32, (hd, hd), 0) == lax.broadcasted_iota(jnp.int32, (hd, hd), 1)

    def pick(vec, lanes, idx):
        return jnp.sum(jnp.where(lanes == idx, vec, 0.0), axis=1, keepdims=True)

    def column(x):
        return jnp.sum(jnp.where(eye, jnp.broadcast_to(x, (hd, hd)), 0.0), axis=1, keepdims=True)

    def conv(j):
        st = conv_ref[0, j]
        w = cw_ref[j]
        y = st[0:1] * w[0:1]
        y = y + st[1:2] * w[1:2]
        y = y + st[2:3] * w[2:3]
        y = y + z_ref[CB_QC + j, row, :] * w[3:4]
        return _silu(y)

    def l2n(x):
        return x * lax.rsqrt(jnp.sum(x * x, axis=-1, keepdims=True) + RMS_EPS)

    def head_body(h, carry):
        q = l2n(conv(h)) * (hd ** -0.5)
        k = l2n(conv(heads + h))
        v = conv(2 * heads + h)
        g = -jnp.exp(pick(alog_ref[...], lane_h, h)) * _softplus(pick(arow, lane_ab, h) + pick(dtb_ref[...], lane_h, h))
        beta = _sigmoid(pick(arow, lane_ab, heads + h))
        state = st_ref[0, h] * jnp.exp(g)
        kcol = column(k)
        delta = (v - jnp.sum(kcol * state, axis=0, keepdims=True)) * beta
        state = state + kcol * delta
        o = jnp.sum(column(q) * state, axis=0, keepdims=True)
        on = o * lax.rsqrt(jnp.mean(o * o, axis=-1, keepdims=True) + RMS_EPS) * nw_ref[...]
        mc_ref[h, row, :] = on * _silu(z_ref[CB_GC + h, row, :])
        so_ref[0, h] = state
        return carry

    lax.fori_loop(0, heads, head_body, 0)


def _state_sample(z3s, ab_s, state_pool, state_conv3, state_delta, pool_w, pool_scale, conv_w3, a_log, dt_bias,
                  norm_w):
    nb = state_pool.shape[0]
    rows = z3s.shape[1]
    heads = C_HEADS
    bw = state_pool.shape[2]

    def full(a):
        return pl.BlockSpec(a.shape, lambda b: (0,) * a.ndim)

    def per_b(a):
        return pl.BlockSpec((1,) + a.shape[1:], lambda b: (b,) + (0,) * (a.ndim - 1))

    small = (pool_w.astype(BF16), pool_scale.reshape(1, bw), conv_w3, a_log.reshape(1, heads),
             dt_bias.reshape(1, heads), norm_w.reshape(1, LANE))
    return pl.pallas_call(
        _state_sample_body,
        out_shape=(jax.ShapeDtypeStruct((bw // LANE, rows, LANE), F32),
                   jax.ShapeDtypeStruct((heads, rows, LANE), F32),
                   jax.ShapeDtypeStruct(state_delta.shape, F32)),
        grid=(nb,),
        in_specs=[full(z3s), full(ab_s), per_b(state_pool), per_b(state_conv3), per_b(state_delta)]
        + [full(a) for a in small],
        out_specs=(pl.BlockSpec((bw // LANE, rows, LANE), lambda b: (0, 0, 0)),
                   pl.BlockSpec((heads, rows, LANE), lambda b: (0, 0, 0)),
                   per_b(state_delta)),
        compiler_params=_params("arbitrary"),
        name="state_sample",
    )(z3s, ab_s, state_pool, state_conv3, state_delta, *small)


def _cache_insert_body(c_ref, n_ref, o_ref):
    del c_ref
    o_ref[...] = n_ref[...]


def _cache_insert(shifted, new_rows):
    depth, nb, nbuf, heads, hd = shifted.shape
    return pl.pallas_call(
        _cache_insert_body,
        out_shape=jax.ShapeDtypeStruct(shifted.shape, shifted.dtype),
        grid=(depth, nb),
        in_specs=[pl.BlockSpec(memory_space=pl.ANY),
                  pl.BlockSpec((None, None, 1, heads, hd), lambda l, b: (l, b, 0, 0, 0))],
        out_specs=pl.BlockSpec((None, None, 1, heads, hd), lambda l, b: (l, b, nbuf - 1, 0, 0)),
        input_output_aliases={0: 0},
        compiler_params=_params("arbitrary", "arbitrary"),
        name="cache_insert",
    )(shifted, new_rows)


def _cols(z4, cb0, ncb, r0, r1):
    blk = z4[cb0:cb0 + ncb, :, r0:r1]
    return jnp.transpose(blk, (1, 2, 0, 3)).reshape(blk.shape[1], r1 - r0, ncb * LANE)


def kernel(x_prompt, x_sample, cache_win_k, cache_win_v, state_pool, state_conv, state_delta, norm_w, w_in,
           conv_w, a_log, dt_bias, delta_norm_w, pool_w, pool_scale, w_out, final_norm_w):
    batch, seq, d_model = x_prompt.shape
    nb = x_sample.shape[0]
    depth = w_in.shape[0]
    heads = C_HEADS
    m = batch * seq
    rows_s = 16
    nbuf = cache_win_k.shape[2]
    pool_cb = state_pool.shape[-1] // LANE

    hp = x_prompt.reshape(m, d_model)
    hs = jnp.zeros((rows_s, d_model), F32).at[:nb].set(x_sample.reshape(nb, d_model))

    assert nbuf == A_PATTERNS[-1][0], "the window cache is expected full: one row in, one row out"

    w_in_t = jnp.swapaxes(w_in, 1, 2)
    outs_p = [[] for _ in range(5)]
    outs_s = [[] for _ in range(5)]
    new_k, new_v = [], []
    assert depth == 2, "each of the two layers' delta-rule calls also moves one of the two window caches"
    shifted = [None, None]
    for l in range(depth):
        conv_w3 = jnp.transpose(conv_w[l].reshape(C_CONV, 3 * heads, LANE), (1, 0, 2))

        h = _rmsnorm(hp, norm_w[l], BF16, 256)
        h_s = _rmsnorm(hs, norm_w[l], BF16, rows_s)
        z3, ab, z3s, ab_s = _inproj(h, h_s, w_in_t, l, 2048, 512)
        mix_a, k_nat, v_nat = _attn_prompt(z3, batch, seq)
        mix_b = _pool_prompt(z3, pool_w[l], pool_scale[l], batch, seq)
        mix_c, s_new, shifted[l] = _delta_prompt(z3, ab, conv_w3, a_log[l], dt_bias[l], delta_norm_w[l],
                                                 (cache_win_k, cache_win_v)[l], batch, seq)
        hp = _outproj(mix_a, mix_b, mix_c, w_out, l, hp, 2048, 512)

        z4 = z3.reshape(CB_TOTAL, batch, seq, LANE)
        keep_p = min(seq, A_PATTERNS[-1][0])
        outs_p[0].append(k_nat.reshape(batch, seq, A_HEADS, LANE)[:, seq - keep_p:])
        outs_p[1].append(v_nat.reshape(batch, seq, A_HEADS, LANE)[:, seq - keep_p:])
        outs_p[2].append(_cols(z4, CB_UB, pool_cb, seq - B_BUF, seq))
        outs_p[3].append(_cols(z4, CB_QC, 3 * heads, seq - (C_CONV - 1), seq))
        outs_p[4].append(s_new)

        mix_as = _attn_sample(z3s, cache_win_k, cache_win_v, l)
        conv3 = jnp.transpose(state_conv[l].reshape(nb, C_CONV - 1, 3 * heads, LANE), (0, 2, 1, 3))
        mix_bs, mix_cs, st_new = _state_sample(z3s, ab_s, state_pool[l], conv3, state_delta[l], pool_w[l],
                                               pool_scale[l], conv_w3, a_log[l], dt_bias[l], delta_norm_w[l])
        hs = _outproj(mix_as, mix_bs, mix_cs, w_out, l, hs, rows_s, 512)

        z4s = z3s[:, :nb].reshape(CB_TOTAL, nb, 1, LANE)
        new_k.append(_cols(z4s, CB_KA, A_HEADS, 0, 1).reshape(nb, 1, A_HEADS, LANE))
        new_v.append(_cols(z4s, CB_VA, A_HEADS, 0, 1).reshape(nb, 1, A_HEADS, LANE))
        outs_s[2].append(jnp.concatenate([state_pool[l], _cols(z4s, CB_UB, pool_cb, 0, 1)], axis=1)[:, 1:])
        outs_s[3].append(jnp.concatenate([state_conv[l], _cols(z4s, CB_QC, 3 * heads, 0, 1)], axis=1)[:, 1:])
        outs_s[4].append(st_new)

    y_prompt = _rmsnorm(hp, final_norm_w, F32, 256).reshape(batch, seq, d_model)
    y_sample = _rmsnorm(hs, final_norm_w, F32, rows_s)[:nb].reshape(nb, 1, d_model)
    stack = lambda xs: jnp.stack(xs, axis=0)
    win_k = _cache_insert(shifted[0], stack(new_k))
    win_v = _cache_insert(shifted[1], stack(new_v))
    return ((y_prompt, y_sample) + tuple(stack(o) for o in outs_p) + (win_k, win_v)
            + tuple(stack(o) for o in outs_s[2:]))
```

```python
import functools

import jax
import jax.numpy as jnp
from jax import lax
from jax.experimental import pallas as pl
from jax.experimental.pallas import tpu as pltpu

F32 = jnp.float32
BF16 = jnp.bfloat16

LANE = 128
MXU_DEPTH = 256
ROW_CHUNK = 1024
MXU_ROWS = 16
VMEM_LIMIT = 56 * 2**20
RMS_EPS = 1e-6

A_HEADS = 8
A_PATTERNS = ((128, 1), (512, 4), (2048, 16))
A_BLOCK = 128
A_UNROLL = 4
B_POOLS = (2, 4, 8, 16)
B_BUF = 15
C_HEADS = 16
C_CONV = 4
CACHE_COPY_ROWS = 256
CONV_PAD = 8

CB_QA, CB_KA, CB_VA, CB_GA = 0, 8, 16, 24
CB_UB, CB_GB = 32, 40
CB_QC, CB_KC, CB_VC, CB_GC = 48, 64, 80, 96
CB_TOTAL = 112
N_MAIN = CB_TOTAL * LANE

HIGHEST = lax.Precision.HIGHEST
ROW_RESIDENT = pl.Buffered(1)


def _params(*sem):
    return pltpu.CompilerParams(dimension_semantics=sem, vmem_limit_bytes=VMEM_LIMIT)


def _sigmoid(x):
    return 1.0 / (1.0 + jnp.exp(-x))


def _silu(x):
    half = 0.5 * x
    return half + half * jnp.tanh(half)


def _softplus(x):
    return jnp.maximum(x, 0.0) + jnp.log(1.0 + jnp.exp(-jnp.abs(x)))


def _dot(a, b, precision=None):
    return jnp.dot(a, b, preferred_element_type=F32, precision=precision)


def _dot_nt(a, b, precision=None):
    return lax.dot_general(a, b, (((1,), (1,)), ((), ())), preferred_element_type=F32, precision=precision)


def _dot_tn(a, b, precision=None):
    return lax.dot_general(a, b, (((0,), (0,)), ((), ())), preferred_element_type=F32, precision=precision)


def _rms_body(x_ref, g_ref, o_ref):
    x = x_ref[...]
    ms = jnp.mean(x * x, axis=-1, keepdims=True)
    o_ref[...] = (x * lax.rsqrt(ms + RMS_EPS) * g_ref[...]).astype(o_ref.dtype)


def _rmsnorm(x, gain, out_dtype, tm):
    m, d = x.shape
    return pl.pallas_call(
        _rms_body,
        out_shape=jax.ShapeDtypeStruct((m, d), out_dtype),
        grid=(m // tm,),
        in_specs=[pl.BlockSpec((tm, d), lambda i: (i, 0)), pl.BlockSpec((1, d), lambda i: (0, 0))],
        out_specs=pl.BlockSpec((tm, d), lambda i: (i, 0)),
        compiler_params=_params("parallel"),
        name="rmsnorm",
    )(x, gain.reshape(1, d))


def _inproj_body(h_ref, hs_ref, w_ref, wab_ref, z_ref, ab_ref, zs_ref, abs_ref):
    i, j = pl.program_id(0), pl.program_id(1)
    tm = h_ref.shape[0]
    w = w_ref[...].astype(BF16)
    for r0 in range(0, tm, ROW_CHUNK):
        rows = pl.ds(r0, min(ROW_CHUNK, tm - r0))
        h = h_ref[rows, :]
        acc = _dot_nt(h, w)
        for c in range(acc.shape[1] // LANE):
            z_ref[c, rows, :] = acc[:, c * LANE:(c + 1) * LANE]

        @pl.when(j == 0)
        def _():
            ab_ref[rows, :] = _dot_nt(h, wab_ref[...].astype(BF16))

    @pl.when(i == 0)
    def _():
        hs = hs_ref[...]
        acc_s = _dot_nt(hs, w)
        for c in range(acc_s.shape[1] // LANE):
            zs_ref[c] = acc_s[:, c * LANE:(c + 1) * LANE]

        @pl.when(j == 0)
        def _():
            abs_ref[...] = _dot_nt(hs, wab_ref[...].astype(BF16))


def _inproj(h, hs, w_t, layer, tm, tn):
    m, d = h.shape
    ms = hs.shape[0]
    nab = w_t.shape[1] - N_MAIN
    nj = N_MAIN // tn
    assert N_MAIN % nab == 0
    return pl.pallas_call(
        _inproj_body,
        out_shape=(jax.ShapeDtypeStruct((CB_TOTAL, m, LANE), F32),
                   jax.ShapeDtypeStruct((m, nab), F32),
                   jax.ShapeDtypeStruct((CB_TOTAL, ms, LANE), F32),
                   jax.ShapeDtypeStruct((ms, nab), F32)),
        grid=(m // tm, nj),
        in_specs=[pl.BlockSpec((tm, d), lambda i, j: (i, 0), pipeline_mode=ROW_RESIDENT),
                  pl.BlockSpec((ms, d), lambda i, j: (0, 0)),
                  pl.BlockSpec((None, tn, d), lambda i, j: (layer, j, 0)),
                  pl.BlockSpec((None, nab, d), lambda i, j: (layer, N_MAIN // nab, 0))],
        out_specs=(pl.BlockSpec((tn // LANE, tm, LANE), lambda i, j: (j, i, 0)),
                   pl.BlockSpec((tm, nab), lambda i, j: (i, 0)),
                   pl.BlockSpec((tn // LANE, ms, LANE), lambda i, j: (jnp.where(i == 0, j, nj - 1), 0, 0)),
                   pl.BlockSpec((ms, nab), lambda i, j: (0, 0))),
        compiler_params=_params("arbitrary", "arbitrary"),
        name="inproj",
    )(h, hs, w_t, w_t)


def _outproj_body(ma_ref, mb_ref, mc_ref, sa_ref, sb_ref, sc_ref, w_ref, x_ref, xs_ref, o_ref, os_ref):
    w = w_ref[...].astype(BF16)
    pair = MXU_DEPTH // LANE

    def project(residual, mix_refs, rows):
        blocks = [(ref, c) for ref in mix_refs for c in range(ref.shape[0])]
        acc = residual
        for p in range(0, len(blocks), pair):
            lhs = jnp.concatenate([ref[c, rows, :].astype(BF16) for ref, c in blocks[p:p + pair]], axis=1)
            acc = acc + _dot(lhs, w[p * LANE:p * LANE + MXU_DEPTH, :])
        return acc

    tm = o_ref.shape[0]
    for r0 in range(0, tm, ROW_CHUNK):
        rows = pl.ds(r0, min(ROW_CHUNK, tm - r0))
        o_ref[rows, :] = project(x_ref[rows, :], (ma_ref, mb_ref, mc_ref), rows)

    @pl.when(pl.program_id(0) == 0)
    def _():
        os_ref[...] = project(xs_ref[...], (sa_ref, sb_ref, sc_ref), slice(None))


def _outproj(mix, mix_s, w_all, layer, x, xs, tm, tn):
    m, d = x.shape
    ms = xs.shape[0]
    k = w_all.shape[1]
    nj = d // tn

    def mix_spec(a):
        return pl.BlockSpec((a.shape[0], tm, LANE), lambda i, j: (0, i, 0), pipeline_mode=ROW_RESIDENT)

    def whole(a):
        return pl.BlockSpec(a.shape, lambda i, j: (0,) * a.ndim)

    def sample_cols(i, j):
        return 0, jnp.where(i == 0, j, nj - 1)

    return pl.pallas_call(
        _outproj_body,
        out_shape=(jax.ShapeDtypeStruct((m, d), F32), jax.ShapeDtypeStruct((ms, d), F32)),
        grid=(m // tm, nj),
        in_specs=[mix_spec(a) for a in mix] + [whole(a) for a in mix_s]
        + [pl.BlockSpec((None, k, tn), lambda i, j: (layer, 0, j)),
           pl.BlockSpec((tm, tn), lambda i, j: (i, j)),
           pl.BlockSpec((ms, tn), sample_cols)],
        out_specs=(pl.BlockSpec((tm, tn), lambda i, j: (i, j)), pl.BlockSpec((ms, tn), sample_cols)),
        compiler_params=_params("arbitrary", "arbitrary"),
        name="outproj",
    )(*mix, *mix_s, w_all, x, xs)


def _attn_prompt_body(q_ref, k_ref, v_ref, g_ref, *refs, seq):
    n = A_BLOCK
    scale = LANE ** -0.5
    earlier, (o_ref, kn_ref, vn_ref, acc_ref, m_ref, l_ref) = refs[:-6], refs[-6:]
    if earlier:
        kn_ref[0] = earlier[0][...]
        vn_ref[0] = earlier[1][...]
        kn_ref[1] = k_ref[0]
        vn_ref[1] = v_ref[0]
    else:
        kn_ref[...] = k_ref[0]
        vn_ref[...] = v_ref[0]
    qi = lax.broadcasted_iota(jnp.int32, (n, n), 0)
    kj = lax.broadcasted_iota(jnp.int32, (n, n), 1)
    cur_mask = kj <= qi
    prev_mask = kj >= qi

    def rows(t0, d):
        return pl.ds(t0, n) if d == 1 else pl.ds(t0, n, stride=d)

    blocks = []
    for pi, (window, d) in enumerate(sorted(A_PATTERNS, key=lambda p: -p[1])):
        for r in range(d):
            for blk in range(seq // window):
                t0 = blk * window + r
                blocks.append((pi, rows(t0, d), rows(t0 - window, d) if blk > 0 else None))
    waves = [blocks[i:i + A_UNROLL] for i in range(0, len(blocks), A_UNROLL)]

    def scores(block):
        _, cur, prev = block
        q = q_ref[0, cur, :].astype(BF16)
        sc = [_dot_nt(q, k_ref[0, cur, :].astype(BF16)), v_ref[0, cur, :].astype(BF16)]
        if prev is not None:
            sc += [_dot_nt(q, k_ref[0, prev, :].astype(BF16)), v_ref[0, prev, :].astype(BF16)]
        return sc

    def softmax(sc):
        s_c = jnp.where(cur_mask, sc[0] * scale, -jnp.inf)
        if len(sc) == 2:
            m = jnp.max(s_c, axis=1, keepdims=True)
            p_c = jnp.exp(s_c - m)
            return m, jnp.sum(p_c, axis=1, keepdims=True), p_c.astype(BF16)
        s_p = jnp.where(prev_mask, sc[2] * scale, -jnp.inf)
        m = jnp.max(jnp.maximum(s_c, s_p), axis=1, keepdims=True)
        p_c = jnp.exp(s_c - m)
        p_p = jnp.exp(s_p - m)
        return m, jnp.sum(p_c + p_p, axis=1, keepdims=True), p_c.astype(BF16), p_p.astype(BF16)

    def values(sc, pr):
        o = _dot(pr[2], sc[1])
        if len(sc) > 2:
            o = o + _dot(pr[3], sc[3])
        return o

    def accumulate(block, pr, o):
        pi, cur, _ = block
        mb = jnp.broadcast_to(pr[0], (n, LANE))
        lb = jnp.broadcast_to(pr[1], (n, LANE))
        if pi == 0:
            acc_ref[cur, :] = o
            m_ref[cur, :] = mb
            l_ref[cur, :] = lb
        else:
            m_old = m_ref[cur, :]
            m_new = jnp.maximum(m_old, mb)
            w_old = jnp.exp(m_old - m_new)
            w_cur = jnp.exp(mb - m_new)
            acc_ref[cur, :] = acc_ref[cur, :] * w_old + o * w_cur
            l_ref[cur, :] = l_ref[cur, :] * w_old + lb * w_cur
            m_ref[cur, :] = m_new

    sc_of, pr_of, o_of = {}, {}, {}
    for step in range(len(waves) + 3):
        if step < len(waves):
            sc_of[step] = [scores(blk) for blk in waves[step]]
        w = step - 1
        if 0 <= w < len(waves):
            pr_of[w] = [softmax(sc) for sc in sc_of[w]]
        w = step - 2
        if 0 <= w < len(waves):
            o_of[w] = [values(sc, pr) for sc, pr in zip(sc_of.pop(w), pr_of[w])]
        w = step - 3
        if 0 <= w < len(waves):
            for blk, pr, o in zip(waves[w], pr_of.pop(w), o_of.pop(w)):
                accumulate(blk, pr, o)

    rows_out = 256

    def finish(i, carry):
        sl = pl.ds(pl.multiple_of(i * rows_out, rows_out), rows_out)
        out = acc_ref[sl, :] / l_ref[sl, :] * _silu(g_ref[0, sl, :])
        o_ref[0, sl, :] = out.astype(o_ref.dtype)
        return carry

    lax.fori_loop(0, seq // rows_out, finish, 0)


def _attn_prompt(z3, batch, seq, earlier=()):
    m = z3.shape[1]

    def spec(cb0):
        return pl.BlockSpec((1, seq, LANE), lambda b, h: (cb0 + h, b, 0))

    nat = pl.BlockSpec((seq, LANE), lambda b, h: (b, h))
    if earlier:
        nat_out = pl.BlockSpec((2, seq, LANE), lambda b, h: (0, b, h))
        nat_shape = jax.ShapeDtypeStruct((2, m, A_HEADS * LANE), F32)
    else:
        nat_out, nat_shape = nat, jax.ShapeDtypeStruct((m, A_HEADS * LANE), F32)
    return pl.pallas_call(
        functools.partial(_attn_prompt_body, seq=seq),
        out_shape=(jax.ShapeDtypeStruct((A_HEADS, m, LANE), BF16), nat_shape, nat_shape),
        grid=(batch, A_HEADS),
        in_specs=[spec(CB_QA), spec(CB_KA), spec(CB_VA), spec(CB_GA)] + [nat] * len(earlier),
        out_specs=(pl.BlockSpec((1, seq, LANE), lambda b, h: (h, b, 0)), nat_out, nat_out),
        scratch_shapes=[pltpu.VMEM((seq, LANE), F32)] * 3,
        compiler_params=_params("parallel", "parallel"),
        name="attn_prompt",
    )(z3, z3, z3, z3, *earlier)


def _pool_prompt_body(u_ref, g_ref, w_ref, sc_ref, o_ref, *, seq):
    grp = pl.program_id(1)
    row = lax.broadcasted_iota(jnp.int32, (seq, 1), 0)
    for gi, width in enumerate(B_POOLS):
        @pl.when(grp == gi)
        def _(width=width):
            u = jnp.concatenate([u_ref[0], u_ref[1]], axis=1)
            s = u
            step = 1
            while step < width:
                s = s + jnp.where(row >= step, pltpu.roll(s, step, axis=0), 0.0)
                step *= 2
            cnt = jnp.minimum(row + 1, width).astype(F32)
            y = (s / cnt - u).astype(BF16)
            out = _dot(y, w_ref[0]) * sc_ref[0]
            gate = jnp.concatenate([g_ref[0], g_ref[1]], axis=1)
            out = (out * _silu(gate)).astype(o_ref.dtype)
            o_ref[0] = out[:, :LANE]
            o_ref[1] = out[:, LANE:]


def _pool_prompt(z3, pool_w, pool_scale, batch, seq):
    m = z3.shape[1]
    ngrp = len(B_POOLS)
    gw = pool_w.shape[1]
    cpg = gw // LANE

    def spec(cb0):
        return pl.BlockSpec((cpg, seq, LANE), lambda b, g: (cb0 // cpg + g, b, 0))

    return pl.pallas_call(
        functools.partial(_pool_prompt_body, seq=seq),
        out_shape=jax.ShapeDtypeStruct((ngrp * cpg, m, LANE), BF16),
        grid=(batch, ngrp),
        in_specs=[spec(CB_UB), spec(CB_GB),
                  pl.BlockSpec((1, gw, gw), lambda b, g: (g, 0, 0)),
                  pl.BlockSpec((1, 1, gw), lambda b, g: (g, 0, 0))],
        out_specs=pl.BlockSpec((cpg, seq, LANE), lambda b, g: (g, b, 0)),
        compiler_params=_params("parallel", "parallel"),
        name="pool_prompt",
    )(z3, z3, pool_w.astype(BF16), pool_scale.reshape(ngrp, 1, gw))


def _unit_lower_inverses(mats):
    c = mats[0].shape[0]
    eye = (lax.broadcasted_iota(jnp.int32, (c, c), 0) == lax.broadcasted_iota(jnp.int32, (c, c), 1)).astype(F32)
    xs = [eye - a for a in mats]
    pbs = [(-a).astype(BF16) for a in mats]
    k = 2
    while k < c:
        pbs = [_dot(pb, pb).astype(BF16) for pb in pbs]
        xs = [x + _dot(x.astype(BF16), pb) for x, pb in zip(xs, pbs)]
        k *= 2
    return xs


def _shift_half_window(c_ref, o_ref, second_half):
    half = o_ref.shape[0]

    @pl.when(jnp.logical_not(second_half))
    def _():
        for i in range(0, half, CACHE_COPY_ROWS):
            o_ref[pl.ds(i, CACHE_COPY_ROWS)] = c_ref[pl.ds(i, CACHE_COPY_ROWS)]

    @pl.when(second_half)
    def _():
        for i in range(0, half, CACHE_COPY_ROWS):
            rows = min(CACHE_COPY_ROWS, half - 1 - i)
            o_ref[pl.ds(i, rows)] = c_ref[pl.ds(i + 1, rows)]
        o_ref[pl.ds(half - 1, 1)] = c_ref[pl.ds(half - 1, 1)]


def _delta_prompt_body(q_ref, k_ref, v_ref, gate_ref, ab_ref, cw_ref, alog_ref, dtb_ref, nw_ref, cache_ref,
                       o_ref, s_ref, shifted_ref, xbuf_ref, gct_ref, *, group, steps_per_copy):
    heads = C_HEADS
    c = q_ref.shape[1]
    tile = c
    hd = LANE

    step = pl.program_id(0) * pl.num_programs(1) + pl.program_id(1)

    @pl.when(step % steps_per_copy == 0)
    def _():
        _shift_half_window(cache_ref.at[0, 0], shifted_ref, (step // steps_per_copy) % 2 == 1)

    @pl.when(pl.program_id(1) == 0)
    def _():
        s_ref[...] = jnp.zeros_like(s_ref)
        xbuf_ref[:, 0:CONV_PAD, :] = jnp.zeros((3 * heads, CONV_PAD, hd), F32)

    srcs = (q_ref, k_ref, v_ref)
    for s, ref in enumerate(srcs):
        for j in range(heads):
            xbuf_ref[s * heads + j, CONV_PAD:2 * CONV_PAD, :] = ref[j, 0:CONV_PAD, :]

    g = -jnp.exp(alog_ref[...]) * _softplus(ab_ref[:, 0:heads] + dtb_ref[...])
    beta = _sigmoid(ab_ref[:, heads:2 * heads])

    ri = lax.broadcasted_iota(jnp.int32, (c, c), 0)
    ci = lax.broadcasted_iota(jnp.int32, (c, c), 1)
    causal = ri >= ci
    strict = ri > ci
    gcum = _dot(causal.astype(F32), g, HIGHEST)
    eye_h = (lax.broadcasted_iota(jnp.int32, (heads, heads), 0)
             == lax.broadcasted_iota(jnp.int32, (heads, heads), 1)).astype(F32)
    gct_ref[...] = _dot_nt(eye_h, gcum, HIGHEST)
    lane_h = lax.broadcasted_iota(jnp.int32, (1, heads), 1)

    def conv(s, h):
        j = s * heads + h
        w = cw_ref[j]
        first = CONV_PAD - (C_CONV - 1)
        y_edge = xbuf_ref[j, pl.ds(first, CONV_PAD), :] * w[0:1]
        y_rest = srcs[s][h, pl.ds(first, tile - CONV_PAD), :] * w[0:1]
        for i in range(1, C_CONV):
            y_edge = y_edge + xbuf_ref[j, pl.ds(first + i, CONV_PAD), :] * w[i:i + 1]
            y_rest = y_rest + srcs[s][h, pl.ds(first + i, tile - CONV_PAD), :] * w[i:i + 1]
        return _silu(jnp.concatenate([y_edge, y_rest], axis=0))

    def l2n(x):
        return x * lax.rsqrt(jnp.sum(x * x, axis=-1, keepdims=True) + RMS_EPS)

    def group_body(hg, carry):
        hs = [hg * group + i for i in range(group)]
        states = [s_ref[0, h] for h in hs]
        pre = []
        for h in hs:
            onehot = lane_h == h
            q = l2n(conv(0, h)) * (hd ** -0.5)
            k = l2n(conv(1, h))
            v = conv(2, h)
            gcol = jnp.sum(jnp.where(onehot, gcum, 0.0), axis=1, keepdims=True)
            bcol = jnp.sum(jnp.where(onehot, beta, 0.0), axis=1, keepdims=True)
            grow = gct_ref[pl.ds(h, 1), :]
            dec = jnp.exp(jnp.where(causal, gcol - grow, -jnp.inf))
            kb = k * bcol
            kq = _dot_nt(jnp.concatenate([kb, q], axis=0).astype(BF16), k.astype(BF16))
            a_mat = jnp.where(strict, kq[:c] * dec, 0.0)
            qk = (kq[c:] * dec).astype(BF16)
            egc = jnp.exp(gcol)
            rhs = jnp.concatenate([v * bcol, kb * egc], axis=1).astype(BF16)
            glast = gcol[c - 1:c, :]
            kd = (k * jnp.exp(glast - gcol)).astype(BF16)
            pre.append((a_mat, qk, rhs, (q * egc).astype(BF16), kd, jnp.exp(glast)))
        invs = _unit_lower_inverses([p[0] for p in pre])
        sols = [_dot(t.astype(BF16), p[2]) for t, p in zip(invs, pre)]
        ws_qs = [_dot(jnp.concatenate([sol[:, hd:].astype(BF16), p[3]], axis=0), st.astype(BF16))
                 for sol, p, st in zip(sols, pre, states)]
        v_news = [(sol[:, :hd] - wq[:c]).astype(BF16) for sol, wq in zip(sols, ws_qs)]
        outs = [wq[c:] + _dot(p[1], vn) for wq, p, vn in zip(ws_qs, pre, v_news)]
        new_states = [st * p[5] + _dot_tn(p[4], vn) for st, p, vn in zip(states, pre, v_news)]
        for h, o in zip(hs, outs):
            on = o * lax.rsqrt(jnp.mean(o * o, axis=-1, keepdims=True) + RMS_EPS) * nw_ref[...]
            o_ref[h] = (on * _silu(gate_ref[h])).astype(o_ref.dtype)
        for h, state in zip(hs, new_states):
            s_ref[0, h] = state
        return carry

    lax.fori_loop(0, heads // group, group_body, 0)

    for s, ref in enumerate(srcs):
        for j in range(heads):
            xbuf_ref[s * heads + j, 0:CONV_PAD, :] = ref[j, tile - CONV_PAD:tile, :]


def _delta_prompt(z3, ab, conv_w3, a_log, dt_bias, norm_w, cache, batch, seq, tile=128, group=16):
    m = z3.shape[1]
    heads = C_HEADS
    nt = seq // tile
    depth, nb, nbuf = cache.shape[:3]
    half = nbuf // 2
    ncopy = depth * nb * 2
    assert (batch * nt) % ncopy == 0 and half % CACHE_COPY_ROWS == 0
    steps_per_copy = batch * nt // ncopy

    def spec(cb0):
        return pl.BlockSpec((heads, tile, LANE), lambda b, t: (cb0 // heads, b * nt + t, 0))

    def full(a):
        return pl.BlockSpec(a.shape, lambda b, t: (0,) * a.ndim)

    def window(b, t):
        cidx = (b * nt + t) // steps_per_copy
        return cidx // (2 * nb), (cidx // 2) % nb, cidx % 2

    def cache_in(b, t):
        l, w, hf = window(b, t)
        return l, w, jnp.where(hf == 0, 1, half), 0, 0

    def cache_out(b, t):
        l, w, hf = window(b, t)
        return l, w, hf, 0, 0

    small = (conv_w3, a_log.reshape(1, heads), dt_bias.reshape(1, heads), norm_w.reshape(1, LANE))
    return pl.pallas_call(
        functools.partial(_delta_prompt_body, group=group, steps_per_copy=steps_per_copy),
        out_shape=(jax.ShapeDtypeStruct((heads, m, LANE), BF16),
                   jax.ShapeDtypeStruct((batch, heads, LANE, LANE), F32),
                   jax.ShapeDtypeStruct(cache.shape, cache.dtype)),
        grid=(batch, nt),
        in_specs=[spec(CB_QC), spec(CB_KC), spec(CB_VC), spec(CB_GC),
                  pl.BlockSpec((tile, 2 * heads), lambda b, t: (b * nt + t, 0))] + [full(a) for a in small]
        + [pl.BlockSpec(tuple(pl.Element(n) for n in (1, 1, half) + cache.shape[3:]), cache_in)],
        out_specs=(pl.BlockSpec((heads, tile, LANE), lambda b, t: (0, b * nt + t, 0)),
                   pl.BlockSpec((1, heads, LANE, LANE), lambda b, t: (b, 0, 0, 0)),
                   pl.BlockSpec((None, None, half) + cache.shape[3:], cache_out)),
        scratch_shapes=[pltpu.VMEM((3 * heads, 2 * CONV_PAD, LANE), F32),
                        pltpu.VMEM((heads, tile), F32)],
        compiler_params=_params("arbitrary", "arbitrary"),
        name="delta_prompt",
    )(z3, z3, z3, z3, ab, *small, cache)


def _attn_sample_body(z_ref, *refs, npat):
    kv_refs, o_ref = refs[:2 * npat], refs[2 * npat]
    b = pl.program_id(0)
    heads = A_HEADS
    n = A_BLOCK

    @pl.when(b == 0)
    def _():
        o_ref[...] = jnp.zeros_like(o_ref)

    row = pl.ds(b, 1)

    def head_rows(cb0):
        return jnp.concatenate([z_ref[cb0 + h, row, :] for h in range(heads)], axis=0)

    q = head_rows(CB_QA).astype(BF16)
    kn = head_rows(CB_KA).astype(BF16).astype(F32)
    vn = head_rows(CB_VA).astype(BF16).astype(F32)
    gate = head_rows(CB_GA)
    scale = LANE ** -0.5
    qpad = jnp.concatenate([q, jnp.zeros((MXU_ROWS - heads, LANE), BF16)], axis=0)
    s_new = jnp.sum(q.astype(F32) * kn, axis=1, keepdims=True) * scale
    col = lax.broadcasted_iota(jnp.int32, (heads, n * heads), 1)
    own = (col & (heads - 1)) == lax.broadcasted_iota(jnp.int32, (heads, n * heads), 0)
    parts = []
    for i in range(npat):
        kmat = kv_refs[2 * i][...].reshape(n * heads, LANE).astype(BF16)
        vmat = kv_refs[2 * i + 1][...].reshape(n * heads, LANE).astype(BF16)
        s = jnp.where(own, _dot_nt(qpad, kmat)[:heads] * scale, -jnp.inf)
        m = jnp.maximum(jnp.max(s, axis=1, keepdims=True), s_new)
        p = jnp.exp(s - m)
        pn = jnp.exp(s_new - m)
        l = jnp.sum(p, axis=1, keepdims=True) + pn
        ppad = jnp.concatenate([p, jnp.zeros((MXU_ROWS - heads, n * heads), F32)], axis=0).astype(BF16)
        o = _dot(ppad, vmat)[:heads] + pn.astype(BF16).astype(F32) * vn
        parts.append((o, m, l))
    m_all = functools.reduce(jnp.maximum, [m for _, m, _ in parts])
    num = jnp.zeros((heads, LANE), F32)
    den = jnp.zeros((heads, 1), F32)
    for o, m, l in parts:
        wgt = jnp.exp(m - m_all)
        num = num + o * wgt
        den = den + l * wgt
    out = num / den * _silu(gate)
    for h in range(heads):
        o_ref[h, row, :] = out[h:h + 1]


def _attn_sample(z3s, cache_k, cache_v, layer):
    depth, nb, nbuf, heads, hd = cache_k.shape
    rows = z3s.shape[1]
    n = A_BLOCK
    assert heads == A_HEADS and hd == LANE
    assert all(nbuf % window == 0 for window, _ in A_PATTERNS), "cached window shorter than a pattern"
    views, specs = [], []
    for window, d in A_PATTERNS:
        last = nbuf // d // n - 1
        spec = pl.BlockSpec((None, None, n, None, heads, hd), lambda b, last=last: (layer, b, last, 0, 0, 0))
        for cache in (cache_k, cache_v):
            views.append(cache.reshape(depth, nb, nbuf // d, d, heads, hd))
            specs.append(spec)
    return pl.pallas_call(
        functools.partial(_attn_sample_body, npat=len(A_PATTERNS)),
        out_shape=jax.ShapeDtypeStruct((heads, rows, LANE), F32),
        grid=(nb,),
        in_specs=[pl.BlockSpec(z3s.shape, lambda b: (0, 0, 0))] + specs,
        out_specs=pl.BlockSpec((heads, rows, LANE), lambda b: (0, 0, 0)),
        compiler_params=_params("arbitrary"),
        name="attn_sample",
    )(z3s, *views)


def _state_sample_body(z_ref, ab_ref, pool_ref, conv_ref, st_ref, pw_ref, psc_ref, cw_ref, alog_ref, dtb_ref,
                       nw_ref, mb_ref, mc_ref, so_ref):
    b = pl.program_id(0)
    heads = C_HEADS
    hd = LANE

    @pl.when(b == 0)
    def _():
        mb_ref[...] = jnp.zeros_like(mb_ref)
        mc_ref[...] = jnp.zeros_like(mc_ref)

    row = pl.ds(b, 1)

    gw = pw_ref.shape[1]
    cpg = gw // LANE
    for gi, width in enumerate(B_POOLS):
        ys = []
        for j in range(cpg):
            cb = gi * cpg + j
            un = z_ref[CB_UB + cb, row, :]
            prev = pool_ref[0, B_BUF - (width - 1):B_BUF, cb * LANE:(cb + 1) * LANE]
            mean = (jnp.sum(prev, axis=0, keepdims=True) + un) / float(width)
            ys.append(mean - un)
        y = jnp.broadcast_to(jnp.concatenate(ys, axis=1), (MXU_ROWS, gw)).astype(BF16)
        out = _dot(y, pw_ref[gi])[0:1] * psc_ref[:, gi * gw:(gi + 1) * gw]
        for j in range(cpg):
            cb = gi * cpg + j
            mb_ref[cb, row, :] = out[:, j * LANE:(j + 1) * LANE] * _silu(z_ref[CB_GB + cb, row, :])

    arow = ab_ref[row, :]
    lane_ab = lax.broadcasted_iota(jnp.int32, (1, 2 * heads), 1)
    lane_h = lax.broadcasted_iota(jnp.int32, (1, heads), 1)
    eye = lax.broadcasted_iota(jnp.int32, (hd, hd), 0) == lax.broadcasted_iota(jnp.int32, (hd, hd), 1)

    def pick(vec, lanes, idx):
        return jnp.sum(jnp.where(lanes == idx, vec, 0.0), axis=1, keepdims=True)

    def column(x):
        return jnp.sum(jnp.where(eye, jnp.broadcast_to(x, (hd, hd)), 0.0), axis=1, keepdims=True)

    def conv(j):
        st = conv_ref[0, j]
        w = cw_ref[j]
        y = st[0:1] * w[0:1]
        y = y + st[1:2] * w[1:2]
        y = y + st[2:3] * w[2:3]
        y = y + z_ref[CB_QC + j, row, :] * w[3:4]
        return _silu(y)

    def l2n(x):
        return x * lax.rsqrt(jnp.sum(x * x, axis=-1, keepdims=True) + RMS_EPS)

    def head_body(h, carry):
        q = l2n(conv(h)) * (hd ** -0.5)
        k = l2n(conv(heads + h))
        v = conv(2 * heads + h)
        g = -jnp.exp(pick(alog_ref[...], lane_h, h)) * _softplus(pick(arow, lane_ab, h) + pick(dtb_ref[...], lane_h, h))
        beta = _sigmoid(pick(arow, lane_ab, heads + h))
        state = st_ref[0, h] * jnp.exp(g)
        kcol = column(k)
        delta = (v - jnp.sum(kcol * state, axis=0, keepdims=True)) * beta
        state = state + kcol * delta
        o = jnp.sum(column(q) * state, axis=0, keepdims=True)
        on = o * lax.rsqrt(jnp.mean(o * o, axis=-1, keepdims=True) + RMS_EPS) * nw_ref[...]
        mc_ref[h, row, :] = on * _silu(z_ref[CB_GC + h, row, :])
        so_ref[0, h] = state
        return carry

    lax.fori_loop(0, heads, head_body, 0)


def _state_sample(z3s, ab_s, state_pool, state_conv3, state_delta, layer, pool_w, pool_scale, conv_w3, a_log,
                  dt_bias, norm_w):
    nb = state_pool.shape[1]
    rows = z3s.shape[1]
    heads = C_HEADS
    bw = state_pool.shape[-1]

    def full(a):
        return pl.BlockSpec(a.shape, lambda b: (0,) * a.ndim)

    def per_b(a):
        return pl.BlockSpec((1,) + a.shape[1:], lambda b: (b,) + (0,) * (a.ndim - 1))

    def layer_b(a):
        return pl.BlockSpec((None, 1) + a.shape[2:], lambda b: (layer, b) + (0,) * (a.ndim - 2))

    small = (pool_w.astype(BF16), pool_scale.reshape(1, bw), conv_w3, a_log.reshape(1, heads),
             dt_bias.reshape(1, heads), norm_w.reshape(1, LANE))
    state_shape = jax.ShapeDtypeStruct(state_delta.shape[1:], F32)
    return pl.pallas_call(
        _state_sample_body,
        out_shape=(jax.ShapeDtypeStruct((bw // LANE, rows, LANE), F32),
                   jax.ShapeDtypeStruct((heads, rows, LANE), F32),
                   state_shape),
        grid=(nb,),
        in_specs=[full(z3s), full(ab_s), layer_b(state_pool), per_b(state_conv3), layer_b(state_delta)]
        + [full(a) for a in small],
        out_specs=(pl.BlockSpec((bw // LANE, rows, LANE), lambda b: (0, 0, 0)),
                   pl.BlockSpec((heads, rows, LANE), lambda b: (0, 0, 0)),
                   per_b(state_shape)),
        compiler_params=_params("arbitrary"),
        name="state_sample",
    )(z3s, ab_s, state_pool, state_conv3, state_delta, *small)


def _cache_insert_body(c_ref, n_ref, o_ref):
    del c_ref
    o_ref[...] = n_ref[...]


def _cache_insert(shifted, new_rows):
    depth, nb, nbuf, heads, hd = shifted.shape
    return pl.pallas_call(
        _cache_insert_body,
        out_shape=jax.ShapeDtypeStruct(shifted.shape, shifted.dtype),
        grid=(depth, nb),
        in_specs=[pl.BlockSpec(memory_space=pl.ANY),
                  pl.BlockSpec((None, None, 1, heads, hd), lambda l, b: (l, b, 0, 0, 0))],
        out_specs=pl.BlockSpec((None, None, 1, heads, hd), lambda l, b: (l, b, nbuf - 1, 0, 0)),
        input_output_aliases={0: 0},
        compiler_params=_params("arbitrary", "arbitrary"),
        name="cache_insert",
    )(shifted, new_rows)


def _cols(z4, cb0, ncb, r0, r1):
    blk = z4[cb0:cb0 + ncb, :, r0:r1]
    return jnp.transpose(blk, (1, 2, 0, 3)).reshape(blk.shape[1], r1 - r0, ncb * LANE)


def kernel(x_prompt, x_sample, cache_win_k, cache_win_v, state_pool, state_conv, state_delta, norm_w, w_in,
           conv_w, a_log, dt_bias, delta_norm_w, pool_w, pool_scale, w_out, final_norm_w):
    batch, seq, d_model = x_prompt.shape
    nb = x_sample.shape[0]
    depth = w_in.shape[0]
    heads = C_HEADS
    m = batch * seq
    rows_s = 16
    nbuf = cache_win_k.shape[2]
    pool_cb = state_pool.shape[-1] // LANE

    hp = x_prompt.reshape(m, d_model)
    hs = jnp.zeros((rows_s, d_model), F32).at[:nb].set(x_sample.reshape(nb, d_model))

    assert nbuf == A_PATTERNS[-1][0], "the window cache is expected full: one row in, one row out"

    w_in_t = jnp.swapaxes(w_in, 1, 2)
    outs_p = [[] for _ in range(5)]
    outs_s = [[] for _ in range(5)]
    new_k, new_v = [], []
    assert depth == 2, "each of the two layers' delta-rule calls also moves one of the two window caches"
    shifted = [None, None]
    kv_nat = ()
    for l in range(depth):
        conv_w3 = jnp.transpose(conv_w[l].reshape(C_CONV, 3 * heads, LANE), (1, 0, 2))

        h = _rmsnorm(hp, norm_w[l], BF16, 256)
        h_s = _rmsnorm(hs, norm_w[l], BF16, rows_s)
        z3, ab, z3s, ab_s = _inproj(h, h_s, w_in_t, l, 2048, 512)
        mix_a, *kv_nat = _attn_prompt(z3, batch, seq, earlier=kv_nat)
        mix_b = _pool_prompt(z3, pool_w[l], pool_scale[l], batch, seq)
        mix_c, s_new, shifted[l] = _delta_prompt(z3, ab, conv_w3, a_log[l], dt_bias[l], delta_norm_w[l],
                                                 (cache_win_k, cache_win_v)[l], batch, seq)
        z4 = z3.reshape(CB_TOTAL, batch, seq, LANE)
        outs_p[2].append(_cols(z4, CB_UB, pool_cb, seq - B_BUF, seq))
        outs_p[3].append(_cols(z4, CB_QC, 3 * heads, seq - (C_CONV - 1), seq))
        outs_p[4].append(s_new)

        mix_as = _attn_sample(z3s, cache_win_k, cache_win_v, l)
        conv3 = jnp.transpose(state_conv[l].reshape(nb, C_CONV - 1, 3 * heads, LANE), (0, 2, 1, 3))
        mix_bs, mix_cs, st_new = _state_sample(z3s, ab_s, state_pool, conv3, state_delta, l, pool_w[l],
                                               pool_scale[l], conv_w3, a_log[l], dt_bias[l], delta_norm_w[l])
        hp, hs = _outproj((mix_a, mix_b, mix_c), (mix_as, mix_bs, mix_cs), w_out, l, hp, hs, 2048, 512)

        z4s = z3s[:, :nb].reshape(CB_TOTAL, nb, 1, LANE)
        new_k.append(_cols(z4s, CB_KA, A_HEADS, 0, 1).reshape(nb, 1, A_HEADS, LANE))
        new_v.append(_cols(z4s, CB_VA, A_HEADS, 0, 1).reshape(nb, 1, A_HEADS, LANE))
        outs_s[2].append(jnp.concatenate([state_pool[l], _cols(z4s, CB_UB, pool_cb, 0, 1)], axis=1)[:, 1:])
        outs_s[3].append(jnp.concatenate([state_conv[l], _cols(z4s, CB_QC, 3 * heads, 0, 1)], axis=1)[:, 1:])
        outs_s[4].append(st_new)

    y_prompt = _rmsnorm(hp, final_norm_w, F32, 256).reshape(batch, seq, d_model)
    y_sample = _rmsnorm(hs, final_norm_w, F32, rows_s)[:nb].reshape(nb, 1, d_model)
    stack = lambda xs: jnp.stack(xs, axis=0)
    win_k = _cache_insert(shifted[0], stack(new_k))
    win_v = _cache_insert(shifted[1], stack(new_v))
    keep_p = min(seq, A_PATTERNS[-1][0])
    win_p = tuple(a.reshape(depth, batch, seq, A_HEADS, LANE)[:, :, seq - keep_p:] for a in kv_nat)
    return ((y_prompt, y_sample) + win_p + tuple(stack(o) for o in outs_p[2:]) + (win_k, win_v)
            + tuple(stack(o) for o in outs_s[2:]))
```

```python
import functools

import jax
import jax.numpy as jnp
from jax import lax
from jax.experimental import pallas as pl
from jax.experimental.pallas import tpu as pltpu

F32 = jnp.float32
BF16 = jnp.bfloat16

LANE = 128
MXU_DEPTH = 256
ROW_CHUNK = 1024
MXU_ROWS = 16
VMEM_LIMIT = 56 * 2**20
RMS_EPS = 1e-6

A_HEADS = 8
A_PATTERNS = ((128, 1), (512, 4), (2048, 16))
A_BLOCK = 128
A_UNROLL = 4
B_POOLS = (2, 4, 8, 16)
B_BUF = 15
C_HEADS = 16
C_CONV = 4
CACHE_COPY_ROWS = 256
CONV_PAD = 8

CB_QA, CB_KA, CB_VA, CB_GA = 0, 8, 16, 24
CB_UB, CB_GB = 32, 40
CB_QC, CB_KC, CB_VC, CB_GC = 48, 64, 80, 96
CB_TOTAL = 112
N_MAIN = CB_TOTAL * LANE

HIGHEST = lax.Precision.HIGHEST
ROW_RESIDENT = pl.Buffered(1)


def _params(*sem):
    return pltpu.CompilerParams(dimension_semantics=sem, vmem_limit_bytes=VMEM_LIMIT)


def _sigmoid(x):
    return 1.0 / (1.0 + jnp.exp(-x))


def _silu(x):
    half = 0.5 * x
    return half + half * jnp.tanh(half)


def _softplus(x):
    return jnp.maximum(x, 0.0) + jnp.log(1.0 + jnp.exp(-jnp.abs(x)))


def _dot(a, b, precision=None):
    return jnp.dot(a, b, preferred_element_type=F32, precision=precision)


def _dot_nt(a, b, precision=None):
    return lax.dot_general(a, b, (((1,), (1,)), ((), ())), preferred_element_type=F32, precision=precision)


def _dot_tn(a, b, precision=None):
    return lax.dot_general(a, b, (((0,), (0,)), ((), ())), preferred_element_type=F32, precision=precision)


def _rms_body(x_ref, g_ref, o_ref):
    x = x_ref[...]
    ms = jnp.mean(x * x, axis=-1, keepdims=True)
    o_ref[...] = (x * lax.rsqrt(ms + RMS_EPS) * g_ref[...]).astype(o_ref.dtype)


def _rmsnorm(x, gain, out_dtype, tm):
    m, d = x.shape
    return pl.pallas_call(
        _rms_body,
        out_shape=jax.ShapeDtypeStruct((m, d), out_dtype),
        grid=(m // tm,),
        in_specs=[pl.BlockSpec((tm, d), lambda i: (i, 0)), pl.BlockSpec((1, d), lambda i: (0, 0))],
        out_specs=pl.BlockSpec((tm, d), lambda i: (i, 0)),
        compiler_params=_params("parallel"),
        name="rmsnorm",
    )(x, gain.reshape(1, d))


def _inproj_body(h_ref, hs_ref, w_ref, wab_ref, z_ref, ab_ref, zs_ref, abs_ref):
    i, j = pl.program_id(0), pl.program_id(1)
    tm = h_ref.shape[0]
    w = w_ref[...].astype(BF16)
    for r0 in range(0, tm, ROW_CHUNK):
        rows = pl.ds(r0, min(ROW_CHUNK, tm - r0))
        h = h_ref[rows, :]
        acc = _dot_nt(h, w)
        for c in range(acc.shape[1] // LANE):
            z_ref[c, rows, :] = acc[:, c * LANE:(c + 1) * LANE]

        @pl.when(j == 0)
        def _():
            ab_ref[rows, :] = _dot_nt(h, wab_ref[...].astype(BF16))

    @pl.when(i == 0)
    def _():
        hs = hs_ref[...]
        acc_s = _dot_nt(hs, w)
        for c in range(acc_s.shape[1] // LANE):
            zs_ref[c] = acc_s[:, c * LANE:(c + 1) * LANE]

        @pl.when(j == 0)
        def _():
            abs_ref[...] = _dot_nt(hs, wab_ref[...].astype(BF16))


def _inproj(h, hs, w_t, layer, tm, tn):
    m, d = h.shape
    ms = hs.shape[0]
    nab = w_t.shape[1] - N_MAIN
    nj = N_MAIN // tn
    assert N_MAIN % nab == 0
    return pl.pallas_call(
        _inproj_body,
        out_shape=(jax.ShapeDtypeStruct((CB_TOTAL, m, LANE), F32),
                   jax.ShapeDtypeStruct((m, nab), F32),
                   jax.ShapeDtypeStruct((CB_TOTAL, ms, LANE), F32),
                   jax.ShapeDtypeStruct((ms, nab), F32)),
        grid=(m // tm, nj),
        in_specs=[pl.BlockSpec((tm, d), lambda i, j: (i, 0), pipeline_mode=ROW_RESIDENT),
                  pl.BlockSpec((ms, d), lambda i, j: (0, 0)),
                  pl.BlockSpec((None, tn, d), lambda i, j: (layer, j, 0)),
                  pl.BlockSpec((None, nab, d), lambda i, j: (layer, N_MAIN // nab, 0))],
        out_specs=(pl.BlockSpec((tn // LANE, tm, LANE), lambda i, j: (j, i, 0)),
                   pl.BlockSpec((tm, nab), lambda i, j: (i, 0)),
                   pl.BlockSpec((tn // LANE, ms, LANE), lambda i, j: (jnp.where(i == 0, j, nj - 1), 0, 0)),
                   pl.BlockSpec((ms, nab), lambda i, j: (0, 0))),
        compiler_params=_params("arbitrary", "arbitrary"),
        name="inproj",
    )(h, hs, w_t, w_t)


def _outproj_body(ma_ref, mb_ref, mc_ref, sa_ref, sb_ref, sc_ref, w_ref, x_ref, xs_ref, o_ref, os_ref):
    w = w_ref[...].astype(BF16)
    pair = MXU_DEPTH // LANE

    def project(residual, mix_refs, rows):
        blocks = [(ref, c) for ref in mix_refs for c in range(ref.shape[0])]
        acc = residual
        for p in range(0, len(blocks), pair):
            lhs = jnp.concatenate([ref[c, rows, :].astype(BF16) for ref, c in blocks[p:p + pair]], axis=1)
            acc = acc + _dot(lhs, w[p * LANE:p * LANE + MXU_DEPTH, :])
        return acc

    tm = o_ref.shape[0]
    for r0 in range(0, tm, ROW_CHUNK):
        rows = pl.ds(r0, min(ROW_CHUNK, tm - r0))
        o_ref[rows, :] = project(x_ref[rows, :], (ma_ref, mb_ref, mc_ref), rows)

    @pl.when(pl.program_id(0) == 0)
    def _():
        os_ref[...] = project(xs_ref[...], (sa_ref, sb_ref, sc_ref), slice(None))


def _outproj(mix, mix_s, w_all, layer, x, xs, tm, tn):
    m, d = x.shape
    ms = xs.shape[0]
    k = w_all.shape[1]
    nj = d // tn

    def mix_spec(a):
        return pl.BlockSpec((a.shape[0], tm, LANE), lambda i, j: (0, i, 0), pipeline_mode=ROW_RESIDENT)

    def whole(a):
        return pl.BlockSpec(a.shape, lambda i, j: (0,) * a.ndim)

    def sample_cols(i, j):
        return 0, jnp.where(i == 0, j, nj - 1)

    return pl.pallas_call(
        _outproj_body,
        out_shape=(jax.ShapeDtypeStruct((m, d), F32), jax.ShapeDtypeStruct((ms, d), F32)),
        grid=(m // tm, nj),
        in_specs=[mix_spec(a) for a in mix] + [whole(a) for a in mix_s]
        + [pl.BlockSpec((None, k, tn), lambda i, j: (layer, 0, j)),
           pl.BlockSpec((tm, tn), lambda i, j: (i, j)),
           pl.BlockSpec((ms, tn), sample_cols)],
        out_specs=(pl.BlockSpec((tm, tn), lambda i, j: (i, j)), pl.BlockSpec((ms, tn), sample_cols)),
        compiler_params=_params("arbitrary", "arbitrary"),
        name="outproj",
    )(*mix, *mix_s, w_all, x, xs)


def _attn_prompt_body(q_ref, k_ref, v_ref, g_ref, *refs, seq):
    n = A_BLOCK
    scale = LANE ** -0.5
    earlier, (o_ref, kn_ref, vn_ref, acc_ref, m_ref, l_ref) = refs[:-6], refs[-6:]
    if earlier:
        kn_ref[0] = earlier[0][...]
        vn_ref[0] = earlier[1][...]
        kn_ref[1] = k_ref[0]
        vn_ref[1] = v_ref[0]
    else:
        kn_ref[...] = k_ref[0]
        vn_ref[...] = v_ref[0]
    qi = lax.broadcasted_iota(jnp.int32, (n, n), 0)
    kj = lax.broadcasted_iota(jnp.int32, (n, n), 1)
    cur_mask = kj <= qi
    prev_mask = kj >= qi

    def rows(t0, d):
        return pl.ds(t0, n) if d == 1 else pl.ds(t0, n, stride=d)

    blocks = []
    for pi, (window, d) in enumerate(sorted(A_PATTERNS, key=lambda p: -p[1])):
        for r in range(d):
            for blk in range(seq // window):
                t0 = blk * window + r
                blocks.append((pi, rows(t0, d), rows(t0 - window, d) if blk > 0 else None))
    waves = [blocks[i:i + A_UNROLL] for i in range(0, len(blocks), A_UNROLL)]

    def scores(block):
        _, cur, prev = block
        q = q_ref[0, cur, :].astype(BF16)
        sc = [_dot_nt(q, k_ref[0, cur, :].astype(BF16)), v_ref[0, cur, :].astype(BF16)]
        if prev is not None:
            sc += [_dot_nt(q, k_ref[0, prev, :].astype(BF16)), v_ref[0, prev, :].astype(BF16)]
        return sc

    def softmax(sc):
        s_c = jnp.where(cur_mask, sc[0] * scale, -jnp.inf)
        if len(sc) == 2:
            m = jnp.max(s_c, axis=1, keepdims=True)
            p_c = jnp.exp(s_c - m)
            return m, jnp.sum(p_c, axis=1, keepdims=True), p_c.astype(BF16)
        s_p = jnp.where(prev_mask, sc[2] * scale, -jnp.inf)
        m = jnp.max(jnp.maximum(s_c, s_p), axis=1, keepdims=True)
        p_c = jnp.exp(s_c - m)
        p_p = jnp.exp(s_p - m)
        return m, jnp.sum(p_c + p_p, axis=1, keepdims=True), p_c.astype(BF16), p_p.astype(BF16)

    def values(sc, pr):
        o = _dot(pr[2], sc[1])
        if len(sc) > 2:
            o = o + _dot(pr[3], sc[3])
        return o

    def accumulate(block, pr, o):
        pi, cur, _ = block
        mb = jnp.broadcast_to(pr[0], (n, LANE))
        lb = jnp.broadcast_to(pr[1], (n, LANE))
        if pi == 0:
            acc_ref[cur, :] = o
            m_ref[cur, :] = mb
            l_ref[cur, :] = lb
        else:
            m_old = m_ref[cur, :]
            m_new = jnp.maximum(m_old, mb)
            w_old = jnp.exp(m_old - m_new)
            w_cur = jnp.exp(mb - m_new)
            acc_ref[cur, :] = acc_ref[cur, :] * w_old + o * w_cur
            l_ref[cur, :] = l_ref[cur, :] * w_old + lb * w_cur
            m_ref[cur, :] = m_new

    sc_of, pr_of, o_of = {}, {}, {}
    for step in range(len(waves) + 3):
        if step < len(waves):
            sc_of[step] = [scores(blk) for blk in waves[step]]
        w = step - 1
        if 0 <= w < len(waves):
            pr_of[w] = [softmax(sc) for sc in sc_of[w]]
        w = step - 2
        if 0 <= w < len(waves):
            o_of[w] = [values(sc, pr) for sc, pr in zip(sc_of.pop(w), pr_of[w])]
        w = step - 3
        if 0 <= w < len(waves):
            for blk, pr, o in zip(waves[w], pr_of.pop(w), o_of.pop(w)):
                accumulate(blk, pr, o)

    rows_out = 256

    def finish(i, carry):
        sl = pl.ds(pl.multiple_of(i * rows_out, rows_out), rows_out)
        out = acc_ref[sl, :] / l_ref[sl, :] * _silu(g_ref[0, sl, :])
        o_ref[0, sl, :] = out.astype(o_ref.dtype)
        return carry

    lax.fori_loop(0, seq // rows_out, finish, 0)


def _attn_prompt(z3, batch, seq, earlier=()):
    m = z3.shape[1]

    def spec(cb0):
        return pl.BlockSpec((1, seq, LANE), lambda b, h: (cb0 + h, b, 0))

    nat = pl.BlockSpec((seq, LANE), lambda b, h: (b, h))
    if earlier:
        nat_out = pl.BlockSpec((2, seq, LANE), lambda b, h: (0, b, h))
        nat_shape = jax.ShapeDtypeStruct((2, m, A_HEADS * LANE), F32)
    else:
        nat_out, nat_shape = nat, jax.ShapeDtypeStruct((m, A_HEADS * LANE), F32)
    return pl.pallas_call(
        functools.partial(_attn_prompt_body, seq=seq),
        out_shape=(jax.ShapeDtypeStruct((A_HEADS, m, LANE), BF16), nat_shape, nat_shape),
        grid=(batch, A_HEADS),
        in_specs=[spec(CB_QA), spec(CB_KA), spec(CB_VA), spec(CB_GA)] + [nat] * len(earlier),
        out_specs=(pl.BlockSpec((1, seq, LANE), lambda b, h: (h, b, 0)), nat_out, nat_out),
        scratch_shapes=[pltpu.VMEM((seq, LANE), F32)] * 3,
        compiler_params=_params("parallel", "parallel"),
        name="attn_prompt",
    )(z3, z3, z3, z3, *earlier)


def _pool_prompt_body(u_ref, g_ref, w_ref, sc_ref, o_ref, *, seq):
    grp = pl.program_id(1)
    row = lax.broadcasted_iota(jnp.int32, (seq, 1), 0)
    for gi, width in enumerate(B_POOLS):
        @pl.when(grp == gi)
        def _(width=width):
            u = jnp.concatenate([u_ref[0], u_ref[1]], axis=1)
            s = u
            step = 1
            while step < width:
                s = s + jnp.where(row >= step, pltpu.roll(s, step, axis=0), 0.0)
                step *= 2
            cnt = jnp.minimum(row + 1, width).astype(F32)
            y = (s / cnt - u).astype(BF16)
            out = _dot(y, w_ref[0]) * sc_ref[0]
            gate = jnp.concatenate([g_ref[0], g_ref[1]], axis=1)
            out = (out * _silu(gate)).astype(o_ref.dtype)
            o_ref[0] = out[:, :LANE]
            o_ref[1] = out[:, LANE:]


def _pool_prompt(z3, pool_w, pool_scale, batch, seq):
    m = z3.shape[1]
    ngrp = len(B_POOLS)
    gw = pool_w.shape[1]
    cpg = gw // LANE

    def spec(cb0):
        return pl.BlockSpec((cpg, seq, LANE), lambda b, g: (cb0 // cpg + g, b, 0))

    return pl.pallas_call(
        functools.partial(_pool_prompt_body, seq=seq),
        out_shape=jax.ShapeDtypeStruct((ngrp * cpg, m, LANE), BF16),
        grid=(batch, ngrp),
        in_specs=[spec(CB_UB), spec(CB_GB),
                  pl.BlockSpec((1, gw, gw), lambda b, g: (g, 0, 0)),
                  pl.BlockSpec((1, 1, gw), lambda b, g: (g, 0, 0))],
        out_specs=pl.BlockSpec((cpg, seq, LANE), lambda b, g: (g, b, 0)),
        compiler_params=_params("parallel", "parallel"),
        name="pool_prompt",
    )(z3, z3, pool_w.astype(BF16), pool_scale.reshape(ngrp, 1, gw))


def _unit_lower_inverses(mats):
    c = mats[0].shape[0]
    eye = (lax.broadcasted_iota(jnp.int32, (c, c), 0) == lax.broadcasted_iota(jnp.int32, (c, c), 1)).astype(F32)
    xs = [eye - a for a in mats]
    pbs = [(-a).astype(BF16) for a in mats]
    k = 2
    while k < c:
        pbs = [_dot(pb, pb).astype(BF16) for pb in pbs]
        xs = [x + _dot(x.astype(BF16), pb) for x, pb in zip(xs, pbs)]
        k *= 2
    return xs


def _shift_half_window(c_ref, o_ref, second_half):
    half = o_ref.shape[0]

    @pl.when(jnp.logical_not(second_half))
    def _():
        for i in range(0, half, CACHE_COPY_ROWS):
            o_ref[pl.ds(i, CACHE_COPY_ROWS)] = c_ref[pl.ds(i, CACHE_COPY_ROWS)]

    @pl.when(second_half)
    def _():
        for i in range(0, half, CACHE_COPY_ROWS):
            rows = min(CACHE_COPY_ROWS, half - 1 - i)
            o_ref[pl.ds(i, rows)] = c_ref[pl.ds(i + 1, rows)]
        o_ref[pl.ds(half - 1, 1)] = c_ref[pl.ds(half - 1, 1)]


def _delta_prompt_body(q_ref, k_ref, v_ref, gate_ref, ab_ref, cw_ref, alog_ref, dtb_ref, nw_ref, cache_ref,
                       o_ref, s_ref, shifted_ref, xbuf_ref, gct_ref, *, group, steps_per_copy):
    heads = C_HEADS
    c = q_ref.shape[1]
    tile = c
    hd = LANE

    step = pl.program_id(0) * pl.num_programs(1) + pl.program_id(1)

    @pl.when(step % steps_per_copy == 0)
    def _():
        _shift_half_window(cache_ref.at[0, 0], shifted_ref, (step // steps_per_copy) % 2 == 1)

    @pl.when(pl.program_id(1) == 0)
    def _():
        s_ref[...] = jnp.zeros_like(s_ref)
        xbuf_ref[:, 0:CONV_PAD, :] = jnp.zeros((3 * heads, CONV_PAD, hd), F32)

    srcs = (q_ref, k_ref, v_ref)
    for s, ref in enumerate(srcs):
        for j in range(heads):
            xbuf_ref[s * heads + j, CONV_PAD:2 * CONV_PAD, :] = ref[j, 0:CONV_PAD, :]

    g = -jnp.exp(alog_ref[...]) * _softplus(ab_ref[:, 0:heads] + dtb_ref[...])
    beta = _sigmoid(ab_ref[:, heads:2 * heads])

    ri = lax.broadcasted_iota(jnp.int32, (c, c), 0)
    ci = lax.broadcasted_iota(jnp.int32, (c, c), 1)
    causal = ri >= ci
    strict = ri > ci
    gcum = _dot(causal.astype(F32), g, HIGHEST)
    eye_h = (lax.broadcasted_iota(jnp.int32, (heads, heads), 0)
             == lax.broadcasted_iota(jnp.int32, (heads, heads), 1)).astype(F32)
    gct_ref[...] = _dot_nt(eye_h, gcum, HIGHEST)
    lane_h = lax.broadcasted_iota(jnp.int32, (1, heads), 1)

    def conv(s, h):
        j = s * heads + h
        w = cw_ref[j]
        first = CONV_PAD - (C_CONV - 1)
        y_edge = xbuf_ref[j, pl.ds(first, CONV_PAD), :] * w[0:1]
        y_rest = srcs[s][h, pl.ds(first, tile - CONV_PAD), :] * w[0:1]
        for i in range(1, C_CONV):
            y_edge = y_edge + xbuf_ref[j, pl.ds(first + i, CONV_PAD), :] * w[i:i + 1]
            y_rest = y_rest + srcs[s][h, pl.ds(first + i, tile - CONV_PAD), :] * w[i:i + 1]
        return _silu(jnp.concatenate([y_edge, y_rest], axis=0))

    def l2n(x):
        return x * lax.rsqrt(jnp.sum(x * x, axis=-1, keepdims=True) + RMS_EPS)

    def group_body(hg, carry):
        hs = [hg * group + i for i in range(group)]
        states = [s_ref[0, h] for h in hs]
        pre = []
        for h in hs:
            onehot = lane_h == h
            q = l2n(conv(0, h)) * (hd ** -0.5)
            k = l2n(conv(1, h))
            v = conv(2, h)
            gcol = jnp.sum(jnp.where(onehot, gcum, 0.0), axis=1, keepdims=True)
            bcol = jnp.sum(jnp.where(onehot, beta, 0.0), axis=1, keepdims=True)
            grow = gct_ref[pl.ds(h, 1), :]
            dec = jnp.exp(jnp.where(causal, gcol - grow, -jnp.inf))
            kb = k * bcol
            kq = _dot_nt(jnp.concatenate([kb, q], axis=0).astype(BF16), k.astype(BF16))
            a_mat = jnp.where(strict, kq[:c] * dec, 0.0)
            qk = (kq[c:] * dec).astype(BF16)
            egc = jnp.exp(gcol)
            rhs = jnp.concatenate([v * bcol, kb * egc], axis=1).astype(BF16)
            glast = gcol[c - 1:c, :]
            kd = (k * jnp.exp(glast - gcol)).astype(BF16)
            pre.append((a_mat, qk, rhs, (q * egc).astype(BF16), kd, jnp.exp(glast)))
        invs = _unit_lower_inverses([p[0] for p in pre])
        sols = [_dot(t.astype(BF16), p[2]) for t, p in zip(invs, pre)]
        ws_qs = [_dot(jnp.concatenate([sol[:, hd:].astype(BF16), p[3]], axis=0), st.astype(BF16))
                 for sol, p, st in zip(sols, pre, states)]
        v_news = [(sol[:, :hd] - wq[:c]).astype(BF16) for sol, wq in zip(sols, ws_qs)]
        outs = [wq[c:] + _dot(p[1], vn) for wq, p, vn in zip(ws_qs, pre, v_news)]
        new_states = [st * p[5] + _dot_tn(p[4], vn) for st, p, vn in zip(states, pre, v_news)]
        for h, o in zip(hs, outs):
            on = o * lax.rsqrt(jnp.mean(o * o, axis=-1, keepdims=True) + RMS_EPS) * nw_ref[...]
            o_ref[h] = (on * _silu(gate_ref[h])).astype(o_ref.dtype)
        for h, state in zip(hs, new_states):
            s_ref[0, h] = state
        return carry

    lax.fori_loop(0, heads // group, group_body, 0)

    for s, ref in enumerate(srcs):
        for j in range(heads):
            xbuf_ref[s * heads + j, 0:CONV_PAD, :] = ref[j, tile - CONV_PAD:tile, :]


def _delta_prompt(z3, ab, conv_w3, a_log, dt_bias, norm_w, cache, batch, seq, tile=128, group=16):
    m = z3.shape[1]
    heads = C_HEADS
    nt = seq // tile
    depth, nb, nbuf = cache.shape[:3]
    half = nbuf // 2
    ncopy = depth * nb * 2
    assert (batch * nt) % ncopy == 0 and half % CACHE_COPY_ROWS == 0
    steps_per_copy = batch * nt // ncopy

    def spec(cb0):
        return pl.BlockSpec((heads, tile, LANE), lambda b, t: (cb0 // heads, b * nt + t, 0))

    def full(a):
        return pl.BlockSpec(a.shape, lambda b, t: (0,) * a.ndim)

    def window(b, t):
        cidx = (b * nt + t) // steps_per_copy
        return cidx // (2 * nb), (cidx // 2) % nb, cidx % 2

    def cache_in(b, t):
        l, w, hf = window(b, t)
        return l, w, jnp.where(hf == 0, 1, half), 0, 0

    def cache_out(b, t):
        l, w, hf = window(b, t)
        return l, w, hf, 0, 0

    small = (conv_w3, a_log.reshape(1, heads), dt_bias.reshape(1, heads), norm_w.reshape(1, LANE))
    return pl.pallas_call(
        functools.partial(_delta_prompt_body, group=group, steps_per_copy=steps_per_copy),
        out_shape=(jax.ShapeDtypeStruct((heads, m, LANE), BF16),
                   jax.ShapeDtypeStruct((batch, heads, LANE, LANE), F32),
                   jax.ShapeDtypeStruct(cache.shape, cache.dtype)),
        grid=(batch, nt),
        in_specs=[spec(CB_QC), spec(CB_KC), spec(CB_VC), spec(CB_GC),
                  pl.BlockSpec((tile, 2 * heads), lambda b, t: (b * nt + t, 0))] + [full(a) for a in small]
        + [pl.BlockSpec(tuple(pl.Element(n) for n in (1, 1, half) + cache.shape[3:]), cache_in)],
        out_specs=(pl.BlockSpec((heads, tile, LANE), lambda b, t: (0, b * nt + t, 0)),
                   pl.BlockSpec((1, heads, LANE, LANE), lambda b, t: (b, 0, 0, 0)),
                   pl.BlockSpec((None, None, half) + cache.shape[3:], cache_out)),
        scratch_shapes=[pltpu.VMEM((3 * heads, 2 * CONV_PAD, LANE), F32),
                        pltpu.VMEM((heads, tile), F32)],
        compiler_params=_params("arbitrary", "arbitrary"),
        name="delta_prompt",
    )(z3, z3, z3, z3, ab, *small, cache)


def _attn_sample_body(z_ref, *refs, npat):
    kv_refs, o_ref = refs[:2 * npat], refs[2 * npat]
    b = pl.program_id(0)
    heads = A_HEADS
    n = A_BLOCK

    @pl.when(b == 0)
    def _():
        o_ref[...] = jnp.zeros_like(o_ref)

    row = pl.ds(b, 1)

    def head_rows(cb0):
        return jnp.concatenate([z_ref[cb0 + h, row, :] for h in range(heads)], axis=0)

    q = head_rows(CB_QA).astype(BF16)
    kn = head_rows(CB_KA).astype(BF16).astype(F32)
    vn = head_rows(CB_VA).astype(BF16).astype(F32)
    gate = head_rows(CB_GA)
    scale = LANE ** -0.5
    qpad = jnp.concatenate([q, jnp.zeros((MXU_ROWS - heads, LANE), BF16)], axis=0)
    s_new = jnp.sum(q.astype(F32) * kn, axis=1, keepdims=True) * scale
    col = lax.broadcasted_iota(jnp.int32, (heads, n * heads), 1)
    own = (col & (heads - 1)) == lax.broadcasted_iota(jnp.int32, (heads, n * heads), 0)
    parts = []
    for i in range(npat):
        kmat = kv_refs[2 * i][...].reshape(n * heads, LANE).astype(BF16)
        vmat = kv_refs[2 * i + 1][...].reshape(n * heads, LANE).astype(BF16)
        s = jnp.where(own, _dot_nt(qpad, kmat)[:heads] * scale, -jnp.inf)
        m = jnp.maximum(jnp.max(s, axis=1, keepdims=True), s_new)
        p = jnp.exp(s - m)
        pn = jnp.exp(s_new - m)
        l = jnp.sum(p, axis=1, keepdims=True) + pn
        ppad = jnp.concatenate([p, jnp.zeros((MXU_ROWS - heads, n * heads), F32)], axis=0).astype(BF16)
        o = _dot(ppad, vmat)[:heads] + pn.astype(BF16).astype(F32) * vn
        parts.append((o, m, l))
    m_all = functools.reduce(jnp.maximum, [m for _, m, _ in parts])
    num = jnp.zeros((heads, LANE), F32)
    den = jnp.zeros((heads, 1), F32)
    for o, m, l in parts:
        wgt = jnp.exp(m - m_all)
        num = num + o * wgt
        den = den + l * wgt
    out = num / den * _silu(gate)
    for h in range(heads):
        o_ref[h, row, :] = out[h:h + 1]


def _attn_sample(z3s, cache_k, cache_v, layer):
    depth, nb, nbuf, heads, hd = cache_k.shape
    rows = z3s.shape[1]
    n = A_BLOCK
    assert heads == A_HEADS and hd == LANE
    assert all(nbuf % window == 0 for window, _ in A_PATTERNS), "cached window shorter than a pattern"
    views, specs = [], []
    for window, d in A_PATTERNS:
        last = nbuf // d // n - 1
        spec = pl.BlockSpec((None, None, n, None, heads, hd), lambda b, last=last: (layer, b, last, 0, 0, 0))
        for cache in (cache_k, cache_v):
            views.append(cache.reshape(depth, nb, nbuf // d, d, heads, hd))
            specs.append(spec)
    return pl.pallas_call(
        functools.partial(_attn_sample_body, npat=len(A_PATTERNS)),
        out_shape=jax.ShapeDtypeStruct((heads, rows, LANE), F32),
        grid=(nb,),
        in_specs=[pl.BlockSpec(z3s.shape, lambda b: (0, 0, 0))] + specs,
        out_specs=pl.BlockSpec((heads, rows, LANE), lambda b: (0, 0, 0)),
        compiler_params=_params("arbitrary"),
        name="attn_sample",
    )(z3s, *views)


def _state_sample_body(z_ref, ab_ref, pool_ref, conv_ref, st_ref, pw_ref, psc_ref, cw_ref, alog_ref, dtb_ref,
                       nw_ref, mb_ref, mc_ref, so_ref):
    b = pl.program_id(0)
    heads = C_HEADS
    hd = LANE

    @pl.when(b == 0)
    def _():
        mb_ref[...] = jnp.zeros_like(mb_ref)
        mc_ref[...] = jnp.zeros_like(mc_ref)

    row = pl.ds(b, 1)

    gw = pw_ref.shape[1]
    cpg = gw // LANE
    for gi, width in enumerate(B_POOLS):
        ys = []
        for j in range(cpg):
            cb = gi * cpg + j
            un = z_ref[CB_UB + cb, row, :]
            prev = pool_ref[0, B_BUF - (width - 1):B_BUF, cb * LANE:(cb + 1) * LANE]
            mean = (jnp.sum(prev, axis=0, keepdims=True) + un) / float(width)
            ys.append(mean - un)
        y = jnp.broadcast_to(jnp.concatenate(ys, axis=1), (MXU_ROWS, gw)).astype(BF16)
        out = _dot(y, pw_ref[gi])[0:1] * psc_ref[:, gi * gw:(gi + 1) * gw]
        for j in range(cpg):
            cb = gi * cpg + j
            mb_ref[cb, row, :] = out[:, j * LANE:(j + 1) * LANE] * _silu(z_ref[CB_GB + cb, row, :])

    arow = ab_ref[row, :]
    eye = lax.broadcasted_iota(jnp.int32, (hd, hd), 0) == lax.broadcasted_iota(jnp.int32, (hd, hd), 1)

    def column(x):
        return jnp.sum(jnp.where(eye, jnp.broadcast_to(x, (hd, hd)), 0.0), axis=1, keepdims=True)

    def conv(j):
        st = conv_ref[0, j]
        w = cw_ref[j]
        y = st[0:1] * w[0:1]
        for i in range(1, C_CONV - 1):
            y = y + st[i:i + 1] * w[i:i + 1]
        y = y + z_ref[CB_QC + j, row, :] * w[C_CONV - 1:C_CONV]
        return _silu(y)

    def l2n(x):
        return x * lax.rsqrt(jnp.sum(x * x, axis=-1, keepdims=True) + RMS_EPS)

    for h in range(heads):
        q = l2n(conv(h)) * (hd ** -0.5)
        k = l2n(conv(heads + h))
        v = conv(2 * heads + h)
        g = -jnp.exp(alog_ref[:, h:h + 1]) * _softplus(arow[:, h:h + 1] + dtb_ref[:, h:h + 1])
        beta = _sigmoid(arow[:, heads + h:heads + h + 1])
        state = st_ref[0, h] * jnp.exp(g)
        kcol = column(k)
        delta = (v - jnp.sum(kcol * state, axis=0, keepdims=True)) * beta
        state = state + kcol * delta
        o = jnp.sum(column(q) * state, axis=0, keepdims=True)
        on = o * lax.rsqrt(jnp.mean(o * o, axis=-1, keepdims=True) + RMS_EPS) * nw_ref[...]
        mc_ref[h, row, :] = on * _silu(z_ref[CB_GC + h, row, :])
        so_ref[0, h] = state


def _state_sample(z3s, ab_s, state_pool, state_conv3, state_delta, layer, pool_w, pool_scale, conv_w3, a_log,
                  dt_bias, norm_w):
    nb = state_pool.shape[1]
    rows = z3s.shape[1]
    heads = C_HEADS
    bw = state_pool.shape[-1]

    def full(a):
        return pl.BlockSpec(a.shape, lambda b: (0,) * a.ndim)

    def per_b(a):
        return pl.BlockSpec((1,) + a.shape[1:], lambda b: (b,) + (0,) * (a.ndim - 1))

    def layer_b(a):
        return pl.BlockSpec((None, 1) + a.shape[2:], lambda b: (layer, b) + (0,) * (a.ndim - 2))

    small = (pool_w.astype(BF16), pool_scale.reshape(1, bw), conv_w3, a_log.reshape(1, heads),
             dt_bias.reshape(1, heads), norm_w.reshape(1, LANE))
    state_shape = jax.ShapeDtypeStruct(state_delta.shape[1:], F32)
    return pl.pallas_call(
        _state_sample_body,
        out_shape=(jax.ShapeDtypeStruct((bw // LANE, rows, LANE), F32),
                   jax.ShapeDtypeStruct((heads, rows, LANE), F32),
                   state_shape),
        grid=(nb,),
        in_specs=[full(z3s), full(ab_s), layer_b(state_pool), per_b(state_conv3), layer_b(state_delta)]
        + [full(a) for a in small],
        out_specs=(pl.BlockSpec((bw // LANE, rows, LANE), lambda b: (0, 0, 0)),
                   pl.BlockSpec((heads, rows, LANE), lambda b: (0, 0, 0)),
                   per_b(state_shape)),
        compiler_params=_params("arbitrary"),
        name="state_sample",
    )(z3s, ab_s, state_pool, state_conv3, state_delta, *small)


def _cache_insert_body(c_ref, n_ref, o_ref):
    del c_ref
    o_ref[...] = n_ref[...]


def _cache_insert(shifted, new_rows):
    depth, nb, nbuf, heads, hd = shifted.shape
    return pl.pallas_call(
        _cache_insert_body,
        out_shape=jax.ShapeDtypeStruct(shifted.shape, shifted.dtype),
        grid=(depth, nb),
        in_specs=[pl.BlockSpec(memory_space=pl.ANY),
                  pl.BlockSpec((None, None, 1, heads, hd), lambda l, b: (l, b, 0, 0, 0))],
        out_specs=pl.BlockSpec((None, None, 1, heads, hd), lambda l, b: (l, b, nbuf - 1, 0, 0)),
        input_output_aliases={0: 0},
        compiler_params=_params("arbitrary", "arbitrary"),
        name="cache_insert",
    )(shifted, new_rows)


def _cols(z4, cb0, ncb, r0, r1):
    blk = z4[cb0:cb0 + ncb, :, r0:r1]
    return jnp.transpose(blk, (1, 2, 0, 3)).reshape(blk.shape[1], r1 - r0, ncb * LANE)


def kernel(x_prompt, x_sample, cache_win_k, cache_win_v, state_pool, state_conv, state_delta, norm_w, w_in,
           conv_w, a_log, dt_bias, delta_norm_w, pool_w, pool_scale, w_out, final_norm_w):
    batch, seq, d_model = x_prompt.shape
    nb = x_sample.shape[0]
    depth = w_in.shape[0]
    heads = C_HEADS
    m = batch * seq
    rows_s = 16
    nbuf = cache_win_k.shape[2]
    pool_cb = state_pool.shape[-1] // LANE

    hp = x_prompt.reshape(m, d_model)
    hs = jnp.zeros((rows_s, d_model), F32).at[:nb].set(x_sample.reshape(nb, d_model))

    assert nbuf == A_PATTERNS[-1][0], "the window cache is expected full: one row in, one row out"

    w_in_t = jnp.swapaxes(w_in, 1, 2)
    outs_p = [[] for _ in range(5)]
    outs_s = [[] for _ in range(5)]
    new_k, new_v = [], []
    assert depth == 2, "each of the two layers' delta-rule calls also moves one of the two window caches"
    shifted = [None, None]
    kv_nat = ()
    for l in range(depth):
        conv_w3 = jnp.transpose(conv_w[l].reshape(C_CONV, 3 * heads, LANE), (1, 0, 2))

        h = _rmsnorm(hp, norm_w[l], BF16, 512)
        h_s = _rmsnorm(hs, norm_w[l], BF16, rows_s)
        z3, ab, z3s, ab_s = _inproj(h, h_s, w_in_t, l, 2048, 512)
        mix_a, *kv_nat = _attn_prompt(z3, batch, seq, earlier=kv_nat)
        mix_b = _pool_prompt(z3, pool_w[l], pool_scale[l], batch, seq)
        mix_c, s_new, shifted[l] = _delta_prompt(z3, ab, conv_w3, a_log[l], dt_bias[l], delta_norm_w[l],
                                                 (cache_win_k, cache_win_v)[l], batch, seq)
        z4 = z3.reshape(CB_TOTAL, batch, seq, LANE)
        outs_p[2].append(_cols(z4, CB_UB, pool_cb, seq - B_BUF, seq))
        outs_p[3].append(_cols(z4, CB_QC, 3 * heads, seq - (C_CONV - 1), seq))
        outs_p[4].append(s_new)

        mix_as = _attn_sample(z3s, cache_win_k, cache_win_v, l)
        conv3 = jnp.transpose(state_conv[l].reshape(nb, C_CONV - 1, 3 * heads, LANE), (0, 2, 1, 3))
        mix_bs, mix_cs, st_new = _state_sample(z3s, ab_s, state_pool, conv3, state_delta, l, pool_w[l],
                                               pool_scale[l], conv_w3, a_log[l], dt_bias[l], delta_norm_w[l])
        hp, hs = _outproj((mix_a, mix_b, mix_c), (mix_as, mix_bs, mix_cs), w_out, l, hp, hs, 2048, 512)

        z4s = z3s[:, :nb].reshape(CB_TOTAL, nb, 1, LANE)
        new_k.append(_cols(z4s, CB_KA, A_HEADS, 0, 1).reshape(nb, 1, A_HEADS, LANE))
        new_v.append(_cols(z4s, CB_VA, A_HEADS, 0, 1).reshape(nb, 1, A_HEADS, LANE))
        outs_s[2].append(jnp.concatenate([state_pool[l], _cols(z4s, CB_UB, pool_cb, 0, 1)], axis=1)[:, 1:])
        outs_s[3].append(jnp.concatenate([state_conv[l], _cols(z4s, CB_QC, 3 * heads, 0, 1)], axis=1)[:, 1:])
        outs_s[4].append(st_new)

    y_prompt = _rmsnorm(hp, final_norm_w, F32, 256).reshape(batch, seq, d_model)
    y_sample = _rmsnorm(hs, final_norm_w, F32, rows_s)[:nb].reshape(nb, 1, d_model)
    stack = lambda xs: jnp.stack(xs, axis=0)
    win_k = _cache_insert(shifted[0], stack(new_k))
    win_v = _cache_insert(shifted[1], stack(new_v))
    keep_p = min(seq, A_PATTERNS[-1][0])
    win_p = tuple(a.reshape(depth, batch, seq, A_HEADS, LANE)[:, :, seq - keep_p:] for a in kv_nat)
    return ((y_prompt, y_sample) + win_p + tuple(stack(o) for o in outs_p[2:]) + (win_k, win_v)
            + tuple(stack(o) for o in outs_s[2:]))
```

```python
import functools

import jax
import jax.numpy as jnp
from jax import lax
from jax.experimental import pallas as pl
from jax.experimental.pallas import tpu as pltpu

F32 = jnp.float32
BF16 = jnp.bfloat16

LANE = 128
MXU_DEPTH = 256
MXU_ROWS = 16
VMEM_LIMIT = 56 * 2**20
RMS_EPS = 1e-6

PROJ_ROWS = 2048
PROJ_COLS = 512
ROW_CHUNK = 1024
NORM_ROWS_BF16 = 512
NORM_ROWS_F32 = 256
SAMPLE_ROWS = 16
DELTA_TILE = 128

A_HEADS = 8
A_PATTERNS = ((128, 1), (512, 4), (2048, 16))
A_BLOCK = 128
A_UNROLL = 4
B_POOLS = (2, 4, 8, 16)
B_BUF = 15
C_HEADS = 16
C_CONV = 4
CACHE_COPY_ROWS = 256
CONV_PAD = 8

CB_QA, CB_KA, CB_VA, CB_GA = 0, 8, 16, 24
CB_UB, CB_GB = 32, 40
CB_QC, CB_KC, CB_VC, CB_GC = 48, 64, 80, 96
CB_TOTAL = 112
N_MAIN = CB_TOTAL * LANE

HIGHEST = lax.Precision.HIGHEST
ROW_RESIDENT = pl.Buffered(1)


def _params(*sem):
    return pltpu.CompilerParams(dimension_semantics=sem, vmem_limit_bytes=VMEM_LIMIT)


def _sigmoid(x):
    return 1.0 / (1.0 + jnp.exp(-x))


def _silu(x):
    half = 0.5 * x
    return half + half * jnp.tanh(half)


def _softplus(x):
    return jnp.maximum(x, 0.0) + jnp.log(1.0 + jnp.exp(-jnp.abs(x)))


def _dot(a, b, precision=None):
    return jnp.dot(a, b, preferred_element_type=F32, precision=precision)


def _dot_nt(a, b, precision=None):
    return lax.dot_general(a, b, (((1,), (1,)), ((), ())), preferred_element_type=F32, precision=precision)


def _dot_tn(a, b, precision=None):
    return lax.dot_general(a, b, (((0,), (0,)), ((), ())), preferred_element_type=F32, precision=precision)


def _rms_body(x_ref, g_ref, o_ref):
    x = x_ref[...]
    ms = jnp.mean(x * x, axis=-1, keepdims=True)
    o_ref[...] = (x * lax.rsqrt(ms + RMS_EPS) * g_ref[...]).astype(o_ref.dtype)


def _rmsnorm(x, gain, out_dtype, tm):
    m, d = x.shape
    return pl.pallas_call(
        _rms_body,
        out_shape=jax.ShapeDtypeStruct((m, d), out_dtype),
        grid=(m // tm,),
        in_specs=[pl.BlockSpec((tm, d), lambda i: (i, 0)), pl.BlockSpec((1, d), lambda i: (0, 0))],
        out_specs=pl.BlockSpec((tm, d), lambda i: (i, 0)),
        compiler_params=_params("parallel"),
        name="rmsnorm",
    )(x, gain.reshape(1, d))


def _inproj_body(h_ref, hs_ref, w_ref, wab_ref, z_ref, ab_ref, zs_ref, abs_ref):
    i, j = pl.program_id(0), pl.program_id(1)
    tm = h_ref.shape[0]
    w = w_ref[...].astype(BF16)
    for r0 in range(0, tm, ROW_CHUNK):
        rows = pl.ds(r0, min(ROW_CHUNK, tm - r0))
        h = h_ref[rows, :]
        acc = _dot_nt(h, w)
        for c in range(acc.shape[1] // LANE):
            z_ref[c, rows, :] = acc[:, c * LANE:(c + 1) * LANE]

        @pl.when(j == 0)
        def _():
            ab_ref[rows, :] = _dot_nt(h, wab_ref[...].astype(BF16))

    @pl.when(i == 0)
    def _():
        hs = hs_ref[...]
        acc_s = _dot_nt(hs, w)
        for c in range(acc_s.shape[1] // LANE):
            zs_ref[c] = acc_s[:, c * LANE:(c + 1) * LANE]

        @pl.when(j == 0)
        def _():
            abs_ref[...] = _dot_nt(hs, wab_ref[...].astype(BF16))


def _inproj(h, hs, w_t, layer, tm, tn):
    m, d = h.shape
    ms = hs.shape[0]
    nab = w_t.shape[1] - N_MAIN
    nj = N_MAIN // tn
    assert N_MAIN % nab == 0
    return pl.pallas_call(
        _inproj_body,
        out_shape=(jax.ShapeDtypeStruct((CB_TOTAL, m, LANE), F32),
                   jax.ShapeDtypeStruct((m, nab), F32),
                   jax.ShapeDtypeStruct((CB_TOTAL, ms, LANE), F32),
                   jax.ShapeDtypeStruct((ms, nab), F32)),
        grid=(m // tm, nj),
        in_specs=[pl.BlockSpec((tm, d), lambda i, j: (i, 0), pipeline_mode=ROW_RESIDENT),
                  pl.BlockSpec((ms, d), lambda i, j: (0, 0)),
                  pl.BlockSpec((None, tn, d), lambda i, j: (layer, j, 0)),
                  pl.BlockSpec((None, nab, d), lambda i, j: (layer, N_MAIN // nab, 0))],
        out_specs=(pl.BlockSpec((tn // LANE, tm, LANE), lambda i, j: (j, i, 0)),
                   pl.BlockSpec((tm, nab), lambda i, j: (i, 0)),
                   pl.BlockSpec((tn // LANE, ms, LANE), lambda i, j: (jnp.where(i == 0, j, nj - 1), 0, 0)),
                   pl.BlockSpec((ms, nab), lambda i, j: (0, 0))),
        compiler_params=_params("arbitrary", "arbitrary"),
        name="inproj",
    )(h, hs, w_t, w_t)


def _outproj_body(ma_ref, mb_ref, mc_ref, sa_ref, sb_ref, sc_ref, w_ref, x_ref, xs_ref, o_ref, os_ref):
    w = w_ref[...].astype(BF16)
    pair = MXU_DEPTH // LANE

    def project(residual, mix_refs, rows):
        blocks = [(ref, c) for ref in mix_refs for c in range(ref.shape[0])]
        acc = residual
        for p in range(0, len(blocks), pair):
            lhs = jnp.concatenate([ref[c, rows, :].astype(BF16) for ref, c in blocks[p:p + pair]], axis=1)
            acc = acc + _dot(lhs, w[p * LANE:p * LANE + MXU_DEPTH, :])
        return acc

    tm = o_ref.shape[0]
    for r0 in range(0, tm, ROW_CHUNK):
        rows = pl.ds(r0, min(ROW_CHUNK, tm - r0))
        o_ref[rows, :] = project(x_ref[rows, :], (ma_ref, mb_ref, mc_ref), rows)

    @pl.when(pl.program_id(0) == 0)
    def _():
        os_ref[...] = project(xs_ref[...], (sa_ref, sb_ref, sc_ref), slice(None))


def _outproj(mix, mix_s, w_all, layer, x, xs, tm, tn):
    m, d = x.shape
    ms = xs.shape[0]
    k = w_all.shape[1]
    nj = d // tn

    def mix_spec(a):
        return pl.BlockSpec((a.shape[0], tm, LANE), lambda i, j: (0, i, 0), pipeline_mode=ROW_RESIDENT)

    def whole(a):
        return pl.BlockSpec(a.shape, lambda i, j: (0,) * a.ndim)

    def sample_cols(i, j):
        return 0, jnp.where(i == 0, j, nj - 1)

    return pl.pallas_call(
        _outproj_body,
        out_shape=(jax.ShapeDtypeStruct((m, d), F32), jax.ShapeDtypeStruct((ms, d), F32)),
        grid=(m // tm, nj),
        in_specs=[mix_spec(a) for a in mix] + [whole(a) for a in mix_s]
        + [pl.BlockSpec((None, k, tn), lambda i, j: (layer, 0, j)),
           pl.BlockSpec((tm, tn), lambda i, j: (i, j)),
           pl.BlockSpec((ms, tn), sample_cols)],
        out_specs=(pl.BlockSpec((tm, tn), lambda i, j: (i, j)), pl.BlockSpec((ms, tn), sample_cols)),
        compiler_params=_params("arbitrary", "arbitrary"),
        name="outproj",
    )(*mix, *mix_s, w_all, x, xs)


def _attn_prompt_body(q_ref, k_ref, v_ref, g_ref, *refs, seq):
    n = A_BLOCK
    scale = LANE ** -0.5
    earlier, (o_ref, kn_ref, vn_ref, acc_ref, m_ref, l_ref) = refs[:-6], refs[-6:]
    if earlier:
        kn_ref[0] = earlier[0][...]
        vn_ref[0] = earlier[1][...]
        kn_ref[1] = k_ref[0]
        vn_ref[1] = v_ref[0]
    else:
        kn_ref[...] = k_ref[0]
        vn_ref[...] = v_ref[0]
    qi = lax.broadcasted_iota(jnp.int32, (n, n), 0)
    kj = lax.broadcasted_iota(jnp.int32, (n, n), 1)
    cur_mask = kj <= qi
    prev_mask = kj >= qi

    def rows(t0, d):
        return pl.ds(t0, n) if d == 1 else pl.ds(t0, n, stride=d)

    blocks = []
    for pi, (window, d) in enumerate(sorted(A_PATTERNS, key=lambda p: -p[1])):
        for r in range(d):
            for blk in range(seq // window):
                t0 = blk * window + r
                blocks.append((pi, rows(t0, d), rows(t0 - window, d) if blk > 0 else None))
    waves = [blocks[i:i + A_UNROLL] for i in range(0, len(blocks), A_UNROLL)]

    def scores(block):
        _, cur, prev = block
        q = q_ref[0, cur, :].astype(BF16)
        sc = [_dot_nt(q, k_ref[0, cur, :].astype(BF16)), v_ref[0, cur, :].astype(BF16)]
        if prev is not None:
            sc += [_dot_nt(q, k_ref[0, prev, :].astype(BF16)), v_ref[0, prev, :].astype(BF16)]
        return sc

    def softmax(sc):
        s_c = jnp.where(cur_mask, sc[0] * scale, -jnp.inf)
        if len(sc) == 2:
            m = jnp.max(s_c, axis=1, keepdims=True)
            p_c = jnp.exp(s_c - m)
            return m, jnp.sum(p_c, axis=1, keepdims=True), p_c.astype(BF16)
        s_p = jnp.where(prev_mask, sc[2] * scale, -jnp.inf)
        m = jnp.max(jnp.maximum(s_c, s_p), axis=1, keepdims=True)
        p_c = jnp.exp(s_c - m)
        p_p = jnp.exp(s_p - m)
        return m, jnp.sum(p_c + p_p, axis=1, keepdims=True), p_c.astype(BF16), p_p.astype(BF16)

    def values(sc, pr):
        o = _dot(pr[2], sc[1])
        if len(sc) > 2:
            o = o + _dot(pr[3], sc[3])
        return o

    def accumulate(block, pr, o):
        pi, cur, _ = block
        mb = jnp.broadcast_to(pr[0], (n, LANE))
        lb = jnp.broadcast_to(pr[1], (n, LANE))
        if pi == 0:
            acc_ref[cur, :] = o
            m_ref[cur, :] = mb
            l_ref[cur, :] = lb
        else:
            m_old = m_ref[cur, :]
            m_new = jnp.maximum(m_old, mb)
            w_old = jnp.exp(m_old - m_new)
            w_cur = jnp.exp(mb - m_new)
            acc_ref[cur, :] = acc_ref[cur, :] * w_old + o * w_cur
            l_ref[cur, :] = l_ref[cur, :] * w_old + lb * w_cur
            m_ref[cur, :] = m_new

    sc_of, pr_of, o_of = {}, {}, {}
    for step in range(len(waves) + 3):
        if step < len(waves):
            sc_of[step] = [scores(blk) for blk in waves[step]]
        w = step - 1
        if 0 <= w < len(waves):
            pr_of[w] = [softmax(sc) for sc in sc_of[w]]
        w = step - 2
        if 0 <= w < len(waves):
            o_of[w] = [values(sc, pr) for sc, pr in zip(sc_of.pop(w), pr_of[w])]
        w = step - 3
        if 0 <= w < len(waves):
            for blk, pr, o in zip(waves[w], pr_of.pop(w), o_of.pop(w)):
                accumulate(blk, pr, o)

    rows_out = 256

    def finish(i, carry):
        sl = pl.ds(pl.multiple_of(i * rows_out, rows_out), rows_out)
        out = acc_ref[sl, :] / l_ref[sl, :] * _silu(g_ref[0, sl, :])
        o_ref[0, sl, :] = out.astype(o_ref.dtype)
        return carry

    lax.fori_loop(0, seq // rows_out, finish, 0)


def _attn_prompt(z3, batch, seq, earlier=()):
    m = z3.shape[1]

    def spec(cb0):
        return pl.BlockSpec((1, seq, LANE), lambda b, h: (cb0 + h, b, 0))

    nat = pl.BlockSpec((seq, LANE), lambda b, h: (b, h))
    if earlier:
        nat_out = pl.BlockSpec((2, seq, LANE), lambda b, h: (0, b, h))
        nat_shape = jax.ShapeDtypeStruct((2, m, A_HEADS * LANE), F32)
    else:
        nat_out, nat_shape = nat, jax.ShapeDtypeStruct((m, A_HEADS * LANE), F32)
    return pl.pallas_call(
        functools.partial(_attn_prompt_body, seq=seq),
        out_shape=(jax.ShapeDtypeStruct((A_HEADS, m, LANE), BF16), nat_shape, nat_shape),
        grid=(batch, A_HEADS),
        in_specs=[spec(CB_QA), spec(CB_KA), spec(CB_VA), spec(CB_GA)] + [nat] * len(earlier),
        out_specs=(pl.BlockSpec((1, seq, LANE), lambda b, h: (h, b, 0)), nat_out, nat_out),
        scratch_shapes=[pltpu.VMEM((seq, LANE), F32)] * 3,
        compiler_params=_params("parallel", "parallel"),
        name="attn_prompt",
    )(z3, z3, z3, z3, *earlier)


def _pool_prompt_body(u_ref, g_ref, w_ref, sc_ref, o_ref, *, seq):
    grp = pl.program_id(1)
    row = lax.broadcasted_iota(jnp.int32, (seq, 1), 0)
    for gi, width in enumerate(B_POOLS):
        @pl.when(grp == gi)
        def _(width=width):
            u = jnp.concatenate([u_ref[0], u_ref[1]], axis=1)
            s = u
            step = 1
            while step < width:
                s = s + jnp.where(row >= step, pltpu.roll(s, step, axis=0), 0.0)
                step *= 2
            cnt = jnp.minimum(row + 1, width).astype(F32)
            y = (s / cnt - u).astype(BF16)
            out = _dot(y, w_ref[0]) * sc_ref[0]
            gate = jnp.concatenate([g_ref[0], g_ref[1]], axis=1)
            out = (out * _silu(gate)).astype(o_ref.dtype)
            o_ref[0] = out[:, :LANE]
            o_ref[1] = out[:, LANE:]


def _pool_prompt(z3, pool_w, pool_scale, batch, seq):
    m = z3.shape[1]
    ngrp = len(B_POOLS)
    gw = pool_w.shape[1]
    cpg = gw // LANE

    def spec(cb0):
        return pl.BlockSpec((cpg, seq, LANE), lambda b, g: (cb0 // cpg + g, b, 0))

    return pl.pallas_call(
        functools.partial(_pool_prompt_body, seq=seq),
        out_shape=jax.ShapeDtypeStruct((ngrp * cpg, m, LANE), BF16),
        grid=(batch, ngrp),
        in_specs=[spec(CB_UB), spec(CB_GB),
                  pl.BlockSpec((1, gw, gw), lambda b, g: (g, 0, 0)),
                  pl.BlockSpec((1, 1, gw), lambda b, g: (g, 0, 0))],
        out_specs=pl.BlockSpec((cpg, seq, LANE), lambda b, g: (g, b, 0)),
        compiler_params=_params("parallel", "parallel"),
        name="pool_prompt",
    )(z3, z3, pool_w.astype(BF16), pool_scale.reshape(ngrp, 1, gw))


def _unit_lower_inverses(mats):
    c = mats[0].shape[0]
    eye = (lax.broadcasted_iota(jnp.int32, (c, c), 0) == lax.broadcasted_iota(jnp.int32, (c, c), 1)).astype(F32)
    xs = [eye - a for a in mats]
    pbs = [(-a).astype(BF16) for a in mats]
    k = 2
    while k < c:
        pbs = [_dot(pb, pb).astype(BF16) for pb in pbs]
        xs = [x + _dot(x.astype(BF16), pb) for x, pb in zip(xs, pbs)]
        k *= 2
    return xs


def _shift_half_window(c_ref, o_ref, second_half):
    half = o_ref.shape[0]

    @pl.when(jnp.logical_not(second_half))
    def _():
        for i in range(0, half, CACHE_COPY_ROWS):
            o_ref[pl.ds(i, CACHE_COPY_ROWS)] = c_ref[pl.ds(i, CACHE_COPY_ROWS)]

    @pl.when(second_half)
    def _():
        for i in range(0, half, CACHE_COPY_ROWS):
            rows = min(CACHE_COPY_ROWS, half - 1 - i)
            o_ref[pl.ds(i, rows)] = c_ref[pl.ds(i + 1, rows)]
        o_ref[pl.ds(half - 1, 1)] = c_ref[pl.ds(half - 1, 1)]


def _delta_prompt_body(q_ref, k_ref, v_ref, gate_ref, ab_ref, cw_ref, alog_ref, dtb_ref, nw_ref, cache_ref,
                       o_ref, s_ref, shifted_ref, xbuf_ref, gct_ref, *, group, steps_per_copy):
    heads = C_HEADS
    c = q_ref.shape[1]
    tile = c
    hd = LANE

    step = pl.program_id(0) * pl.num_programs(1) + pl.program_id(1)

    @pl.when(step % steps_per_copy == 0)
    def _():
        _shift_half_window(cache_ref.at[0, 0], shifted_ref, (step // steps_per_copy) % 2 == 1)

    @pl.when(pl.program_id(1) == 0)
    def _():
        s_ref[...] = jnp.zeros_like(s_ref)
        xbuf_ref[:, 0:CONV_PAD, :] = jnp.zeros((3 * heads, CONV_PAD, hd), F32)

    srcs = (q_ref, k_ref, v_ref)
    for s, ref in enumerate(srcs):
        for j in range(heads):
            xbuf_ref[s * heads + j, CONV_PAD:2 * CONV_PAD, :] = ref[j, 0:CONV_PAD, :]

    g = -jnp.exp(alog_ref[...]) * _softplus(ab_ref[:, 0:heads] + dtb_ref[...])
    beta = _sigmoid(ab_ref[:, heads:2 * heads])

    ri = lax.broadcasted_iota(jnp.int32, (c, c), 0)
    ci = lax.broadcasted_iota(jnp.int32, (c, c), 1)
    causal = ri >= ci
    strict = ri > ci
    gcum = _dot(causal.astype(F32), g, HIGHEST)
    eye_h = (lax.broadcasted_iota(jnp.int32, (heads, heads), 0)
             == lax.broadcasted_iota(jnp.int32, (heads, heads), 1)).astype(F32)
    gct_ref[...] = _dot_nt(eye_h, gcum, HIGHEST)
    lane_h = lax.broadcasted_iota(jnp.int32, (1, heads), 1)

    def conv(s, h):
        j = s * heads + h
        w = cw_ref[j]
        first = CONV_PAD - (C_CONV - 1)
        y_edge = xbuf_ref[j, pl.ds(first, CONV_PAD), :] * w[0:1]
        y_rest = srcs[s][h, pl.ds(first, tile - CONV_PAD), :] * w[0:1]
        for i in range(1, C_CONV):
            y_edge = y_edge + xbuf_ref[j, pl.ds(first + i, CONV_PAD), :] * w[i:i + 1]
            y_rest = y_rest + srcs[s][h, pl.ds(first + i, tile - CONV_PAD), :] * w[i:i + 1]
        return _silu(jnp.concatenate([y_edge, y_rest], axis=0))

    def l2n(x):
        return x * lax.rsqrt(jnp.sum(x * x, axis=-1, keepdims=True) + RMS_EPS)

    def group_body(hg):
        hs = [hg * group + i for i in range(group)]
        states = [s_ref[0, h] for h in hs]
        pre = []
        for h in hs:
            onehot = lane_h == h
            q = l2n(conv(0, h)) * (hd ** -0.5)
            k = l2n(conv(1, h))
            v = conv(2, h)
            gcol = jnp.sum(jnp.where(onehot, gcum, 0.0), axis=1, keepdims=True)
            bcol = jnp.sum(jnp.where(onehot, beta, 0.0), axis=1, keepdims=True)
            grow = gct_ref[pl.ds(h, 1), :]
            dec = jnp.exp(jnp.where(causal, gcol - grow, -jnp.inf))
            kb = k * bcol
            kq = _dot_nt(jnp.concatenate([kb, q], axis=0).astype(BF16), k.astype(BF16))
            a_mat = jnp.where(strict, kq[:c] * dec, 0.0)
            qk = (kq[c:] * dec).astype(BF16)
            egc = jnp.exp(gcol)
            rhs = jnp.concatenate([v * bcol, kb * egc], axis=1).astype(BF16)
            glast = gcol[c - 1:c, :]
            kd = (k * jnp.exp(glast - gcol)).astype(BF16)
            pre.append((a_mat, qk, rhs, (q * egc).astype(BF16), kd, jnp.exp(glast)))
        invs = _unit_lower_inverses([p[0] for p in pre])
        sols = [_dot(t.astype(BF16), p[2]) for t, p in zip(invs, pre)]
        ws_qs = [_dot(jnp.concatenate([sol[:, hd:].astype(BF16), p[3]], axis=0), st.astype(BF16))
                 for sol, p, st in zip(sols, pre, states)]
        v_news = [(sol[:, :hd] - wq[:c]).astype(BF16) for sol, wq in zip(sols, ws_qs)]
        outs = [wq[c:] + _dot(p[1], vn) for wq, p, vn in zip(ws_qs, pre, v_news)]
        new_states = [st * p[5] + _dot_tn(p[4], vn) for st, p, vn in zip(states, pre, v_news)]
        for h, o in zip(hs, outs):
            on = o * lax.rsqrt(jnp.mean(o * o, axis=-1, keepdims=True) + RMS_EPS) * nw_ref[...]
            o_ref[h] = (on * _silu(gate_ref[h])).astype(o_ref.dtype)
        for h, state in zip(hs, new_states):
            s_ref[0, h] = state

    for hg in range(heads // group):
        group_body(hg)

    for s, ref in enumerate(srcs):
        for j in range(heads):
            xbuf_ref[s * heads + j, 0:CONV_PAD, :] = ref[j, tile - CONV_PAD:tile, :]


def _delta_prompt(z3, ab, conv_w3, a_log, dt_bias, norm_w, cache, batch, seq, tile=DELTA_TILE, group=C_HEADS):
    m = z3.shape[1]
    heads = C_HEADS
    nt = seq // tile
    depth, nb, nbuf = cache.shape[:3]
    half = nbuf // 2
    ncopy = depth * nb * 2
    assert (batch * nt) % ncopy == 0 and half % CACHE_COPY_ROWS == 0
    steps_per_copy = batch * nt // ncopy

    def spec(cb0):
        return pl.BlockSpec((heads, tile, LANE), lambda b, t: (cb0 // heads, b * nt + t, 0))

    def full(a):
        return pl.BlockSpec(a.shape, lambda b, t: (0,) * a.ndim)

    def window(b, t):
        cidx = (b * nt + t) // steps_per_copy
        return cidx // (2 * nb), (cidx // 2) % nb, cidx % 2

    def cache_in(b, t):
        l, w, hf = window(b, t)
        return l, w, jnp.where(hf == 0, 1, half), 0, 0

    def cache_out(b, t):
        l, w, hf = window(b, t)
        return l, w, hf, 0, 0

    small = (conv_w3, a_log.reshape(1, heads), dt_bias.reshape(1, heads), norm_w.reshape(1, LANE))
    return pl.pallas_call(
        functools.partial(_delta_prompt_body, group=group, steps_per_copy=steps_per_copy),
        out_shape=(jax.ShapeDtypeStruct((heads, m, LANE), BF16),
                   jax.ShapeDtypeStruct((batch, heads, LANE, LANE), F32),
                   jax.ShapeDtypeStruct(cache.shape, cache.dtype)),
        grid=(batch, nt),
        in_specs=[spec(CB_QC), spec(CB_KC), spec(CB_VC), spec(CB_GC),
                  pl.BlockSpec((tile, 2 * heads), lambda b, t: (b * nt + t, 0))] + [full(a) for a in small]
        + [pl.BlockSpec(tuple(pl.Element(n) for n in (1, 1, half) + cache.shape[3:]), cache_in)],
        out_specs=(pl.BlockSpec((heads, tile, LANE), lambda b, t: (0, b * nt + t, 0)),
                   pl.BlockSpec((1, heads, LANE, LANE), lambda b, t: (b, 0, 0, 0)),
                   pl.BlockSpec((None, None, half) + cache.shape[3:], cache_out)),
        scratch_shapes=[pltpu.VMEM((3 * heads, 2 * CONV_PAD, LANE), F32),
                        pltpu.VMEM((heads, tile), F32)],
        compiler_params=_params("arbitrary", "arbitrary"),
        name="delta_prompt",
    )(z3, z3, z3, z3, ab, *small, cache)


def _attn_sample_body(z_ref, *refs, npat):
    kv_refs, o_ref = refs[:2 * npat], refs[2 * npat]
    b = pl.program_id(0)
    heads = A_HEADS
    n = A_BLOCK

    @pl.when(b == 0)
    def _():
        o_ref[...] = jnp.zeros_like(o_ref)

    row = pl.ds(b, 1)

    def head_rows(cb0):
        return jnp.concatenate([z_ref[cb0 + h, row, :] for h in range(heads)], axis=0)

    q = head_rows(CB_QA).astype(BF16)
    kn = head_rows(CB_KA).astype(BF16).astype(F32)
    vn = head_rows(CB_VA).astype(BF16).astype(F32)
    gate = head_rows(CB_GA)
    scale = LANE ** -0.5
    qpad = jnp.concatenate([q, jnp.zeros((MXU_ROWS - heads, LANE), BF16)], axis=0)
    s_new = jnp.sum(q.astype(F32) * kn, axis=1, keepdims=True) * scale
    col = lax.broadcasted_iota(jnp.int32, (heads, n * heads), 1)
    own = (col & (heads - 1)) == lax.broadcasted_iota(jnp.int32, (heads, n * heads), 0)
    parts = []
    for i in range(npat):
        kmat = kv_refs[2 * i][...].reshape(n * heads, LANE).astype(BF16)
        vmat = kv_refs[2 * i + 1][...].reshape(n * heads, LANE).astype(BF16)
        s = jnp.where(own, _dot_nt(qpad, kmat)[:heads] * scale, -jnp.inf)
        m = jnp.maximum(jnp.max(s, axis=1, keepdims=True), s_new)
        p = jnp.exp(s - m)
        pn = jnp.exp(s_new - m)
        l = jnp.sum(p, axis=1, keepdims=True) + pn
        ppad = jnp.concatenate([p, jnp.zeros((MXU_ROWS - heads, n * heads), F32)], axis=0).astype(BF16)
        o = _dot(ppad, vmat)[:heads] + pn.astype(BF16).astype(F32) * vn
        parts.append((o, m, l))
    m_all = functools.reduce(jnp.maximum, [m for _, m, _ in parts])
    num = jnp.zeros((heads, LANE), F32)
    den = jnp.zeros((heads, 1), F32)
    for o, m, l in parts:
        wgt = jnp.exp(m - m_all)
        num = num + o * wgt
        den = den + l * wgt
    out = num / den * _silu(gate)
    for h in range(heads):
        o_ref[h, row, :] = out[h:h + 1]


def _attn_sample(z3s, cache_k, cache_v, layer):
    depth, nb, nbuf, heads, hd = cache_k.shape
    rows = z3s.shape[1]
    n = A_BLOCK
    assert heads == A_HEADS and hd == LANE
    assert all(nbuf % window == 0 for window, _ in A_PATTERNS), "cached window shorter than a pattern"
    views, specs = [], []
    for window, d in A_PATTERNS:
        last = nbuf // d // n - 1
        spec = pl.BlockSpec((None, None, n, None, heads, hd), lambda b, last=last: (layer, b, last, 0, 0, 0))
        for cache in (cache_k, cache_v):
            views.append(cache.reshape(depth, nb, nbuf // d, d, heads, hd))
            specs.append(spec)
    return pl.pallas_call(
        functools.partial(_attn_sample_body, npat=len(A_PATTERNS)),
        out_shape=jax.ShapeDtypeStruct((heads, rows, LANE), F32),
        grid=(nb,),
        in_specs=[pl.BlockSpec(z3s.shape, lambda b: (0, 0, 0))] + specs,
        out_specs=pl.BlockSpec((heads, rows, LANE), lambda b: (0, 0, 0)),
        compiler_params=_params("arbitrary"),
        name="attn_sample",
    )(z3s, *views)


def _state_sample_body(z_ref, ab_ref, pool_ref, conv_ref, st_ref, pw_ref, psc_ref, cw_ref, alog_ref, dtb_ref,
                       nw_ref, mb_ref, mc_ref, so_ref):
    b = pl.program_id(0)
    heads = C_HEADS
    hd = LANE

    @pl.when(b == 0)
    def _():
        mb_ref[...] = jnp.zeros_like(mb_ref)
        mc_ref[...] = jnp.zeros_like(mc_ref)

    row = pl.ds(b, 1)

    gw = pw_ref.shape[1]
    cpg = gw // LANE
    for gi, width in enumerate(B_POOLS):
        ys = []
        for j in range(cpg):
            cb = gi * cpg + j
            un = z_ref[CB_UB + cb, row, :]
            prev = pool_ref[0, B_BUF - (width - 1):B_BUF, cb * LANE:(cb + 1) * LANE]
            mean = (jnp.sum(prev, axis=0, keepdims=True) + un) / float(width)
            ys.append(mean - un)
        y = jnp.broadcast_to(jnp.concatenate(ys, axis=1), (MXU_ROWS, gw)).astype(BF16)
        out = _dot(y, pw_ref[gi])[0:1] * psc_ref[:, gi * gw:(gi + 1) * gw]
        for j in range(cpg):
            cb = gi * cpg + j
            mb_ref[cb, row, :] = out[:, j * LANE:(j + 1) * LANE] * _silu(z_ref[CB_GB + cb, row, :])

    arow = ab_ref[row, :]
    eye = lax.broadcasted_iota(jnp.int32, (hd, hd), 0) == lax.broadcasted_iota(jnp.int32, (hd, hd), 1)

    def column(x):
        return jnp.sum(jnp.where(eye, jnp.broadcast_to(x, (hd, hd)), 0.0), axis=1, keepdims=True)

    def conv(j):
        st = conv_ref[0, j]
        w = cw_ref[j]
        y = st[0:1] * w[0:1]
        for i in range(1, C_CONV - 1):
            y = y + st[i:i + 1] * w[i:i + 1]
        y = y + z_ref[CB_QC + j, row, :] * w[C_CONV - 1:C_CONV]
        return _silu(y)

    def l2n(x):
        return x * lax.rsqrt(jnp.sum(x * x, axis=-1, keepdims=True) + RMS_EPS)

    for h in range(heads):
        q = l2n(conv(h)) * (hd ** -0.5)
        k = l2n(conv(heads + h))
        v = conv(2 * heads + h)
        g = -jnp.exp(alog_ref[:, h:h + 1]) * _softplus(arow[:, h:h + 1] + dtb_ref[:, h:h + 1])
        beta = _sigmoid(arow[:, heads + h:heads + h + 1])
        state = st_ref[0, h] * jnp.exp(g)
        kcol = column(k)
        delta = (v - jnp.sum(kcol * state, axis=0, keepdims=True)) * beta
        state = state + kcol * delta
        o = jnp.sum(column(q) * state, axis=0, keepdims=True)
        on = o * lax.rsqrt(jnp.mean(o * o, axis=-1, keepdims=True) + RMS_EPS) * nw_ref[...]
        mc_ref[h, row, :] = on * _silu(z_ref[CB_GC + h, row, :])
        so_ref[0, h] = state


def _state_sample(z3s, ab_s, state_pool, state_conv3, state_delta, layer, pool_w, pool_scale, conv_w3, a_log,
                  dt_bias, norm_w):
    nb = state_pool.shape[1]
    rows = z3s.shape[1]
    heads = C_HEADS
    bw = state_pool.shape[-1]

    def full(a):
        return pl.BlockSpec(a.shape, lambda b: (0,) * a.ndim)

    def per_b(a):
        return pl.BlockSpec((1,) + a.shape[1:], lambda b: (b,) + (0,) * (a.ndim - 1))

    def layer_b(a):
        return pl.BlockSpec((None, 1) + a.shape[2:], lambda b: (layer, b) + (0,) * (a.ndim - 2))

    small = (pool_w.astype(BF16), pool_scale.reshape(1, bw), conv_w3, a_log.reshape(1, heads),
             dt_bias.reshape(1, heads), norm_w.reshape(1, LANE))
    state_shape = jax.ShapeDtypeStruct(state_delta.shape[1:], F32)
    return pl.pallas_call(
        _state_sample_body,
        out_shape=(jax.ShapeDtypeStruct((bw // LANE, rows, LANE), F32),
                   jax.ShapeDtypeStruct((heads, rows, LANE), F32),
                   state_shape),
        grid=(nb,),
        in_specs=[full(z3s), full(ab_s), layer_b(state_pool), per_b(state_conv3), layer_b(state_delta)]
        + [full(a) for a in small],
        out_specs=(pl.BlockSpec((bw // LANE, rows, LANE), lambda b: (0, 0, 0)),
                   pl.BlockSpec((heads, rows, LANE), lambda b: (0, 0, 0)),
                   per_b(state_shape)),
        compiler_params=_params("arbitrary"),
        name="state_sample",
    )(z3s, ab_s, state_pool, state_conv3, state_delta, *small)


def _cache_insert_body(c_ref, n_ref, o_ref):
    del c_ref
    o_ref[...] = n_ref[...]


def _cache_insert(shifted, new_rows):
    depth, nb, nbuf, heads, hd = shifted.shape
    return pl.pallas_call(
        _cache_insert_body,
        out_shape=jax.ShapeDtypeStruct(shifted.shape, shifted.dtype),
        grid=(depth, nb),
        in_specs=[pl.BlockSpec(memory_space=pl.ANY),
                  pl.BlockSpec((None, None, 1, heads, hd), lambda l, b: (l, b, 0, 0, 0))],
        out_specs=pl.BlockSpec((None, None, 1, heads, hd), lambda l, b: (l, b, nbuf - 1, 0, 0)),
        input_output_aliases={0: 0},
        compiler_params=_params("arbitrary", "arbitrary"),
        name="cache_insert",
    )(shifted, new_rows)


def _cols(z4, cb0, ncb, r0, r1):
    blk = z4[cb0:cb0 + ncb, :, r0:r1]
    return jnp.transpose(blk, (1, 2, 0, 3)).reshape(blk.shape[1], r1 - r0, ncb * LANE)


def kernel(x_prompt, x_sample, cache_win_k, cache_win_v, state_pool, state_conv, state_delta, norm_w, w_in,
           conv_w, a_log, dt_bias, delta_norm_w, pool_w, pool_scale, w_out, final_norm_w):
    batch, seq, d_model = x_prompt.shape
    nb = x_sample.shape[0]
    depth = w_in.shape[0]
    heads = C_HEADS
    m = batch * seq
    rows_s = SAMPLE_ROWS
    nbuf = cache_win_k.shape[2]
    pool_cb = state_pool.shape[-1] // LANE

    hp = x_prompt.reshape(m, d_model)
    hs = jnp.zeros((rows_s, d_model), F32).at[:nb].set(x_sample.reshape(nb, d_model))

    assert nbuf == A_PATTERNS[-1][0], "the window cache is expected full: one row in, one row out"

    w_in_t = jnp.swapaxes(w_in, 1, 2)
    outs_p = [[] for _ in range(5)]
    outs_s = [[] for _ in range(5)]
    new_k, new_v = [], []
    assert depth == 2, "each of the two layers' delta-rule calls also moves one of the two window caches"
    shifted = [None, None]
    kv_nat = ()
    for l in range(depth):
        conv_w3 = jnp.transpose(conv_w[l].reshape(C_CONV, 3 * heads, LANE), (1, 0, 2))

        h = _rmsnorm(hp, norm_w[l], BF16, NORM_ROWS_BF16)
        h_s = _rmsnorm(hs, norm_w[l], BF16, rows_s)
        z3, ab, z3s, ab_s = _inproj(h, h_s, w_in_t, l, PROJ_ROWS, PROJ_COLS)
        mix_a, *kv_nat = _attn_prompt(z3, batch, seq, earlier=kv_nat)
        mix_b = _pool_prompt(z3, pool_w[l], pool_scale[l], batch, seq)
        mix_c, s_new, shifted[l] = _delta_prompt(z3, ab, conv_w3, a_log[l], dt_bias[l], delta_norm_w[l],
                                                 (cache_win_k, cache_win_v)[l], batch, seq)
        z4 = z3.reshape(CB_TOTAL, batch, seq, LANE)
        outs_p[2].append(_cols(z4, CB_UB, pool_cb, seq - B_BUF, seq))
        outs_p[3].append(_cols(z4, CB_QC, 3 * heads, seq - (C_CONV - 1), seq))
        outs_p[4].append(s_new)

        mix_as = _attn_sample(z3s, cache_win_k, cache_win_v, l)
        conv3 = jnp.transpose(state_conv[l].reshape(nb, C_CONV - 1, 3 * heads, LANE), (0, 2, 1, 3))
        mix_bs, mix_cs, st_new = _state_sample(z3s, ab_s, state_pool, conv3, state_delta, l, pool_w[l],
                                               pool_scale[l], conv_w3, a_log[l], dt_bias[l], delta_norm_w[l])
        hp, hs = _outproj((mix_a, mix_b, mix_c), (mix_as, mix_bs, mix_cs), w_out, l, hp, hs,
                           PROJ_ROWS, PROJ_COLS)

        z4s = z3s[:, :nb].reshape(CB_TOTAL, nb, 1, LANE)
        new_k.append(_cols(z4s, CB_KA, A_HEADS, 0, 1).reshape(nb, 1, A_HEADS, LANE))
        new_v.append(_cols(z4s, CB_VA, A_HEADS, 0, 1).reshape(nb, 1, A_HEADS, LANE))
        outs_s[2].append(jnp.concatenate([state_pool[l], _cols(z4s, CB_UB, pool_cb, 0, 1)], axis=1)[:, 1:])
        outs_s[3].append(jnp.concatenate([state_conv[l], _cols(z4s, CB_QC, 3 * heads, 0, 1)], axis=1)[:, 1:])
        outs_s[4].append(st_new)

    y_prompt = _rmsnorm(hp, final_norm_w, F32, NORM_ROWS_F32).reshape(batch, seq, d_model)
    y_sample = _rmsnorm(hs, final_norm_w, F32, rows_s)[:nb].reshape(nb, 1, d_model)
    stack = lambda xs: jnp.stack(xs, axis=0)
    win_k = _cache_insert(shifted[0], stack(new_k))
    win_v = _cache_insert(shifted[1], stack(new_v))
    keep_p = min(seq, A_PATTERNS[-1][0])
    win_p = tuple(a.reshape(depth, batch, seq, A_HEADS, LANE)[:, :, seq - keep_p:] for a in kv_nat)
    return ((y_prompt, y_sample) + win_p + tuple(stack(o) for o in outs_p[2:]) + (win_k, win_v)
            + tuple(stack(o) for o in outs_s[2:]))
```

```python
import functools

import jax
import jax.numpy as jnp
from jax import lax
from jax.experimental import pallas as pl
from jax.experimental.pallas import tpu as pltpu

F32 = jnp.float32
BF16 = jnp.bfloat16

LANE = 128
MXU_DEPTH = 256
MXU_ROWS = 16
VMEM_LIMIT = 56 * 2**20
RMS_EPS = 1e-6

PROJ_ROWS = 2048
PROJ_COLS = 512
ROW_CHUNK = 1024
NORM_ROWS_BF16 = 512
NORM_ROWS_F32 = 256
SAMPLE_ROWS = 16
DELTA_TILE = 128

A_HEADS = 8
A_PATTERNS = ((128, 1), (512, 4), (2048, 16))
A_BLOCK = 128
A_UNROLL = 4
B_POOLS = (2, 4, 8, 16)
B_BUF = 15
C_HEADS = 16
C_CONV = 4
CACHE_COPY_ROWS = 256
CONV_PAD = 8

CB_QA, CB_KA, CB_VA, CB_GA = 0, 8, 16, 24
CB_UB, CB_GB = 32, 40
CB_QC, CB_KC, CB_VC, CB_GC = 48, 64, 80, 96
CB_TOTAL = 112
N_MAIN = CB_TOTAL * LANE

HIGHEST = lax.Precision.HIGHEST
ROW_RESIDENT = pl.Buffered(1)


def _params(*sem):
    return pltpu.CompilerParams(dimension_semantics=sem, vmem_limit_bytes=VMEM_LIMIT)


def _sigmoid(x):
    return 1.0 / (1.0 + jnp.exp(-x))


def _silu(x):
    half = 0.5 * x
    return half + half * jnp.tanh(half)


def _softplus(x):
    return jnp.maximum(x, 0.0) + jnp.log(1.0 + jnp.exp(-jnp.abs(x)))


def _dot(a, b, precision=None):
    return jnp.dot(a, b, preferred_element_type=F32, precision=precision)


def _dot_nt(a, b, precision=None):
    return lax.dot_general(a, b, (((1,), (1,)), ((), ())), preferred_element_type=F32, precision=precision)


def _dot_tn(a, b, precision=None):
    return lax.dot_general(a, b, (((0,), (0,)), ((), ())), preferred_element_type=F32, precision=precision)


def _rms_body(x_ref, g_ref, o_ref):
    x = x_ref[...]
    ms = jnp.mean(x * x, axis=-1, keepdims=True)
    o_ref[...] = (x * lax.rsqrt(ms + RMS_EPS) * g_ref[...]).astype(o_ref.dtype)


def _rmsnorm(x, gain, out_dtype, tm):
    m, d = x.shape
    return pl.pallas_call(
        _rms_body,
        out_shape=jax.ShapeDtypeStruct((m, d), out_dtype),
        grid=(m // tm,),
        in_specs=[pl.BlockSpec((tm, d), lambda i: (i, 0)), pl.BlockSpec((1, d), lambda i: (0, 0))],
        out_specs=pl.BlockSpec((tm, d), lambda i: (i, 0)),
        compiler_params=_params("parallel"),
        name="rmsnorm",
    )(x, gain.reshape(1, d))


def _inproj_body(h_ref, hs_ref, w_ref, wab_ref, z_ref, ab_ref, zs_ref, abs_ref):
    i, j = pl.program_id(0), pl.program_id(1)
    tm = h_ref.shape[0]
    w = w_ref[...].astype(BF16)
    for r0 in range(0, tm, ROW_CHUNK):
        rows = pl.ds(r0, min(ROW_CHUNK, tm - r0))
        h = h_ref[rows, :]
        acc = _dot_nt(h, w)
        for c in range(acc.shape[1] // LANE):
            z_ref[c, rows, :] = acc[:, c * LANE:(c + 1) * LANE]

        @pl.when(j == 0)
        def _():
            ab_ref[rows, :] = _dot_nt(h, wab_ref[...].astype(BF16))

    @pl.when(i == 0)
    def _():
        hs = hs_ref[...]
        acc_s = _dot_nt(hs, w)
        for c in range(acc_s.shape[1] // LANE):
            zs_ref[c] = acc_s[:, c * LANE:(c + 1) * LANE]

        @pl.when(j == 0)
        def _():
            abs_ref[...] = _dot_nt(hs, wab_ref[...].astype(BF16))


def _inproj(h, hs, w_t, layer, tm, tn):
    m, d = h.shape
    ms = hs.shape[0]
    nab = w_t.shape[1] - N_MAIN
    nj = N_MAIN // tn
    assert N_MAIN % nab == 0
    return pl.pallas_call(
        _inproj_body,
        out_shape=(jax.ShapeDtypeStruct((CB_TOTAL, m, LANE), F32),
                   jax.ShapeDtypeStruct((m, nab), F32),
                   jax.ShapeDtypeStruct((CB_TOTAL, ms, LANE), F32),
                   jax.ShapeDtypeStruct((ms, nab), F32)),
        grid=(m // tm, nj),
        in_specs=[pl.BlockSpec((tm, d), lambda i, j: (i, 0), pipeline_mode=ROW_RESIDENT),
                  pl.BlockSpec((ms, d), lambda i, j: (0, 0)),
                  pl.BlockSpec((None, tn, d), lambda i, j: (layer, j, 0)),
                  pl.BlockSpec((None, nab, d), lambda i, j: (layer, N_MAIN // nab, 0))],
        out_specs=(pl.BlockSpec((tn // LANE, tm, LANE), lambda i, j: (j, i, 0)),
                   pl.BlockSpec((tm, nab), lambda i, j: (i, 0)),
                   pl.BlockSpec((tn // LANE, ms, LANE), lambda i, j: (jnp.where(i == 0, j, nj - 1), 0, 0)),
                   pl.BlockSpec((ms, nab), lambda i, j: (0, 0))),
        compiler_params=_params("arbitrary", "arbitrary"),
        name="inproj",
    )(h, hs, w_t, w_t)


def _outproj_body(ma_ref, mb_ref, mc_ref, sa_ref, sb_ref, sc_ref, w_ref, x_ref, xs_ref, o_ref, os_ref):
    w = w_ref[...].astype(BF16)
    pair = MXU_DEPTH // LANE

    def project(residual, mix_refs, rows):
        blocks = [(ref, c) for ref in mix_refs for c in range(ref.shape[0])]
        acc = residual
        for p in range(0, len(blocks), pair):
            lhs = jnp.concatenate([ref[c, rows, :].astype(BF16) for ref, c in blocks[p:p + pair]], axis=1)
            acc = acc + _dot(lhs, w[p * LANE:p * LANE + MXU_DEPTH, :])
        return acc

    tm = o_ref.shape[0]
    for r0 in range(0, tm, ROW_CHUNK):
        rows = pl.ds(r0, min(ROW_CHUNK, tm - r0))
        o_ref[rows, :] = project(x_ref[rows, :], (ma_ref, mb_ref, mc_ref), rows)

    @pl.when(pl.program_id(0) == 0)
    def _():
        os_ref[...] = project(xs_ref[...], (sa_ref, sb_ref, sc_ref), slice(None))


def _outproj(mix, mix_s, w_all, layer, x, xs, tm, tn):
    m, d = x.shape
    ms = xs.shape[0]
    k = w_all.shape[1]
    nj = d // tn

    def mix_spec(a):
        return pl.BlockSpec((a.shape[0], tm, LANE), lambda i, j: (0, i, 0), pipeline_mode=ROW_RESIDENT)

    def whole(a):
        return pl.BlockSpec(a.shape, lambda i, j: (0,) * a.ndim)

    def sample_cols(i, j):
        return 0, jnp.where(i == 0, j, nj - 1)

    return pl.pallas_call(
        _outproj_body,
        out_shape=(jax.ShapeDtypeStruct((m, d), F32), jax.ShapeDtypeStruct((ms, d), F32)),
        grid=(m // tm, nj),
        in_specs=[mix_spec(a) for a in mix] + [whole(a) for a in mix_s]
        + [pl.BlockSpec((None, k, tn), lambda i, j: (layer, 0, j)),
           pl.BlockSpec((tm, tn), lambda i, j: (i, j)),
           pl.BlockSpec((ms, tn), sample_cols)],
        out_specs=(pl.BlockSpec((tm, tn), lambda i, j: (i, j)), pl.BlockSpec((ms, tn), sample_cols)),
        compiler_params=_params("arbitrary", "arbitrary"),
        name="outproj",
    )(*mix, *mix_s, w_all, x, xs)


def _attn_prompt_body(q_ref, k_ref, v_ref, g_ref, *refs, seq):
    n = A_BLOCK
    scale = LANE ** -0.5
    earlier, (o_ref, kn_ref, vn_ref, acc_ref, m_ref, l_ref) = refs[:-6], refs[-6:]
    if earlier:
        kn_ref[0] = earlier[0][...]
        vn_ref[0] = earlier[1][...]
        kn_ref[1] = k_ref[0]
        vn_ref[1] = v_ref[0]
    else:
        kn_ref[...] = k_ref[0]
        vn_ref[...] = v_ref[0]
    qi = lax.broadcasted_iota(jnp.int32, (n, n), 0)
    kj = lax.broadcasted_iota(jnp.int32, (n, n), 1)
    cur_mask = kj <= qi
    prev_mask = kj >= qi

    def rows(t0, d):
        return pl.ds(t0, n) if d == 1 else pl.ds(t0, n, stride=d)

    blocks = []
    for pi, (window, d) in enumerate(sorted(A_PATTERNS, key=lambda p: -p[1])):
        for r in range(d):
            for blk in range(seq // window):
                t0 = blk * window + r
                blocks.append((pi, rows(t0, d), rows(t0 - window, d) if blk > 0 else None))
    waves = [blocks[i:i + A_UNROLL] for i in range(0, len(blocks), A_UNROLL)]

    def scores(block):
        _, cur, prev = block
        q = q_ref[0, cur, :].astype(BF16)
        sc = [_dot_nt(q, k_ref[0, cur, :].astype(BF16)), v_ref[0, cur, :].astype(BF16)]
        if prev is not None:
            sc += [_dot_nt(q, k_ref[0, prev, :].astype(BF16)), v_ref[0, prev, :].astype(BF16)]
        return sc

    def softmax(sc):
        s_c = jnp.where(cur_mask, sc[0] * scale, -jnp.inf)
        if len(sc) == 2:
            m = jnp.max(s_c, axis=1, keepdims=True)
            p_c = jnp.exp(s_c - m)
            return m, jnp.sum(p_c, axis=1, keepdims=True), p_c.astype(BF16)
        s_p = jnp.where(prev_mask, sc[2] * scale, -jnp.inf)
        m = jnp.max(jnp.maximum(s_c, s_p), axis=1, keepdims=True)
        p_c = jnp.exp(s_c - m)
        p_p = jnp.exp(s_p - m)
        return m, jnp.sum(p_c + p_p, axis=1, keepdims=True), p_c.astype(BF16), p_p.astype(BF16)

    def values(sc, pr):
        o = _dot(pr[2], sc[1])
        if len(sc) > 2:
            o = o + _dot(pr[3], sc[3])
        return o

    def accumulate(block, pr, o):
        pi, cur, _ = block
        mb = jnp.broadcast_to(pr[0], (n, LANE))
        lb = jnp.broadcast_to(pr[1], (n, LANE))
        if pi == 0:
            acc_ref[cur, :] = o
            m_ref[cur, :] = mb
            l_ref[cur, :] = lb
            return
        m_old = m_ref[cur, :]
        m_new = jnp.maximum(m_old, mb)
        w_old = jnp.exp(m_old - m_new)
        w_cur = jnp.exp(mb - m_new)
        acc = acc_ref[cur, :] * w_old + o * w_cur
        l = l_ref[cur, :] * w_old + lb * w_cur
        if pi < len(A_PATTERNS) - 1:
            acc_ref[cur, :] = acc
            l_ref[cur, :] = l
            m_ref[cur, :] = m_new
        else:
            o_ref[0, cur, :] = (acc / l * _silu(g_ref[0, cur, :])).astype(o_ref.dtype)

    sc_of, pr_of, o_of = {}, {}, {}
    for step in range(len(waves) + 3):
        if step < len(waves):
            sc_of[step] = [scores(blk) for blk in waves[step]]
        w = step - 1
        if 0 <= w < len(waves):
            pr_of[w] = [softmax(sc) for sc in sc_of[w]]
        w = step - 2
        if 0 <= w < len(waves):
            o_of[w] = [values(sc, pr) for sc, pr in zip(sc_of.pop(w), pr_of[w])]
        w = step - 3
        if 0 <= w < len(waves):
            for blk, pr, o in zip(waves[w], pr_of.pop(w), o_of.pop(w)):
                accumulate(blk, pr, o)


def _attn_prompt(z3, batch, seq, earlier=()):
    m = z3.shape[1]

    def spec(cb0):
        return pl.BlockSpec((1, seq, LANE), lambda b, h: (cb0 + h, b, 0))

    nat = pl.BlockSpec((seq, LANE), lambda b, h: (b, h))
    if earlier:
        nat_out = pl.BlockSpec((2, seq, LANE), lambda b, h: (0, b, h))
        nat_shape = jax.ShapeDtypeStruct((2, m, A_HEADS * LANE), F32)
    else:
        nat_out, nat_shape = nat, jax.ShapeDtypeStruct((m, A_HEADS * LANE), F32)
    return pl.pallas_call(
        functools.partial(_attn_prompt_body, seq=seq),
        out_shape=(jax.ShapeDtypeStruct((A_HEADS, m, LANE), BF16), nat_shape, nat_shape),
        grid=(batch, A_HEADS),
        in_specs=[spec(CB_QA), spec(CB_KA), spec(CB_VA), spec(CB_GA)] + [nat] * len(earlier),
        out_specs=(pl.BlockSpec((1, seq, LANE), lambda b, h: (h, b, 0)), nat_out, nat_out),
        scratch_shapes=[pltpu.VMEM((seq, LANE), F32)] * 3,
        compiler_params=_params("parallel", "parallel"),
        name="attn_prompt",
    )(z3, z3, z3, z3, *earlier)


def _pool_prompt_body(u_ref, g_ref, w_ref, sc_ref, o_ref, *, seq):
    grp = pl.program_id(1)
    row = lax.broadcasted_iota(jnp.int32, (seq, 1), 0)
    for gi, width in enumerate(B_POOLS):
        @pl.when(grp == gi)
        def _(width=width):
            u = jnp.concatenate([u_ref[0], u_ref[1]], axis=1)
            s = u
            step = 1
            while step < width:
                s = s + jnp.where(row >= step, pltpu.roll(s, step, axis=0), 0.0)
                step *= 2
            cnt = jnp.minimum(row + 1, width).astype(F32)
            y = (s / cnt - u).astype(BF16)
            out = _dot(y, w_ref[0]) * sc_ref[0]
            gate = jnp.concatenate([g_ref[0], g_ref[1]], axis=1)
            out = (out * _silu(gate)).astype(o_ref.dtype)
            o_ref[0] = out[:, :LANE]
            o_ref[1] = out[:, LANE:]


def _pool_prompt(z3, pool_w, pool_scale, batch, seq):
    m = z3.shape[1]
    ngrp = len(B_POOLS)
    gw = pool_w.shape[1]
    cpg = gw // LANE

    def spec(cb0):
        return pl.BlockSpec((cpg, seq, LANE), lambda b, g: (cb0 // cpg + g, b, 0))

    return pl.pallas_call(
        functools.partial(_pool_prompt_body, seq=seq),
        out_shape=jax.ShapeDtypeStruct((ngrp * cpg, m, LANE), BF16),
        grid=(batch, ngrp),
        in_specs=[spec(CB_UB), spec(CB_GB),
                  pl.BlockSpec((1, gw, gw), lambda b, g: (g, 0, 0)),
                  pl.BlockSpec((1, 1, gw), lambda b, g: (g, 0, 0))],
        out_specs=pl.BlockSpec((cpg, seq, LANE), lambda b, g: (g, b, 0)),
        compiler_params=_params("parallel", "parallel"),
        name="pool_prompt",
    )(z3, z3, pool_w.astype(BF16), pool_scale.reshape(ngrp, 1, gw))


def _unit_lower_inverses(mats):
    c = mats[0].shape[0]
    eye = (lax.broadcasted_iota(jnp.int32, (c, c), 0) == lax.broadcasted_iota(jnp.int32, (c, c), 1)).astype(F32)
    xs = [eye - a for a in mats]
    pbs = [(-a).astype(BF16) for a in mats]
    k = 2
    while k < c:
        pbs = [_dot(pb, pb).astype(BF16) for pb in pbs]
        xs = [x + _dot(x.astype(BF16), pb) for x, pb in zip(xs, pbs)]
        k *= 2
    return xs


def _shift_half_window(c_ref, o_ref, second_half):
    half = o_ref.shape[0]

    @pl.when(jnp.logical_not(second_half))
    def _():
        for i in range(0, half, CACHE_COPY_ROWS):
            o_ref[pl.ds(i, CACHE_COPY_ROWS)] = c_ref[pl.ds(i, CACHE_COPY_ROWS)]

    @pl.when(second_half)
    def _():
        for i in range(0, half, CACHE_COPY_ROWS):
            rows = min(CACHE_COPY_ROWS, half - 1 - i)
            o_ref[pl.ds(i, rows)] = c_ref[pl.ds(i + 1, rows)]
        o_ref[pl.ds(half - 1, 1)] = c_ref[pl.ds(half - 1, 1)]


def _delta_prompt_body(q_ref, k_ref, v_ref, gate_ref, ab_ref, cw_ref, alog_ref, dtb_ref, nw_ref, cache_ref,
                       o_ref, s_ref, shifted_ref, xbuf_ref, gct_ref, *, group, steps_per_copy):
    heads = C_HEADS
    c = q_ref.shape[1]
    tile = c
    hd = LANE

    step = pl.program_id(0) * pl.num_programs(1) + pl.program_id(1)

    @pl.when(step % steps_per_copy == 0)
    def _():
        _shift_half_window(cache_ref.at[0, 0], shifted_ref, (step // steps_per_copy) % 2 == 1)

    @pl.when(pl.program_id(1) == 0)
    def _():
        s_ref[...] = jnp.zeros_like(s_ref)
        xbuf_ref[:, 0:CONV_PAD, :] = jnp.zeros((3 * heads, CONV_PAD, hd), F32)

    srcs = (q_ref, k_ref, v_ref)
    for s, ref in enumerate(srcs):
        for j in range(heads):
            xbuf_ref[s * heads + j, CONV_PAD:2 * CONV_PAD, :] = ref[j, 0:CONV_PAD, :]

    g = -jnp.exp(alog_ref[...]) * _softplus(ab_ref[:, 0:heads] + dtb_ref[...])
    beta = _sigmoid(ab_ref[:, heads:2 * heads])

    ri = lax.broadcasted_iota(jnp.int32, (c, c), 0)
    ci = lax.broadcasted_iota(jnp.int32, (c, c), 1)
    causal = ri >= ci
    strict = ri > ci
    gcum = _dot(causal.astype(F32), g, HIGHEST)
    eye_h = (lax.broadcasted_iota(jnp.int32, (heads, heads), 0)
             == lax.broadcasted_iota(jnp.int32, (heads, heads), 1)).astype(F32)
    gct_ref[...] = _dot_nt(eye_h, gcum, HIGHEST)
    lane_h = lax.broadcasted_iota(jnp.int32, (1, heads), 1)

    def conv(s, h):
        j = s * heads + h
        w = cw_ref[j]
        first = CONV_PAD - (C_CONV - 1)
        y_edge = xbuf_ref[j, pl.ds(first, CONV_PAD), :] * w[0:1]
        y_rest = srcs[s][h, pl.ds(first, tile - CONV_PAD), :] * w[0:1]
        for i in range(1, C_CONV):
            y_edge = y_edge + xbuf_ref[j, pl.ds(first + i, CONV_PAD), :] * w[i:i + 1]
            y_rest = y_rest + srcs[s][h, pl.ds(first + i, tile - CONV_PAD), :] * w[i:i + 1]
        return _silu(jnp.concatenate([y_edge, y_rest], axis=0))

    def l2n(x):
        return x * lax.rsqrt(jnp.sum(x * x, axis=-1, keepdims=True) + RMS_EPS)

    def group_body(hg):
        hs = [hg * group + i for i in range(group)]
        states = [s_ref[0, h] for h in hs]
        pre = []
        for h in hs:
            onehot = lane_h == h
            q = l2n(conv(0, h)) * (hd ** -0.5)
            k = l2n(conv(1, h))
            v = conv(2, h)
            gcol = jnp.sum(jnp.where(onehot, gcum, 0.0), axis=1, keepdims=True)
            bcol = jnp.sum(jnp.where(onehot, beta, 0.0), axis=1, keepdims=True)
            grow = gct_ref[pl.ds(h, 1), :]
            dec = jnp.exp(jnp.where(causal, gcol - grow, -jnp.inf))
            kb = k * bcol
            kq = _dot_nt(jnp.concatenate([kb, q], axis=0).astype(BF16), k.astype(BF16))
            a_mat = jnp.where(strict, kq[:c] * dec, 0.0)
            qk = (kq[c:] * dec).astype(BF16)
            egc = jnp.exp(gcol)
            rhs = jnp.concatenate([v * bcol, kb * egc], axis=1).astype(BF16)
            glast = gcol[c - 1:c, :]
            kd = (k * jnp.exp(glast - gcol)).astype(BF16)
            pre.append((a_mat, qk, rhs, (q * egc).astype(BF16), kd, jnp.exp(glast)))
        invs = _unit_lower_inverses([p[0] for p in pre])
        sols = [_dot(t.astype(BF16), p[2]) for t, p in zip(invs, pre)]
        ws_qs = [_dot(jnp.concatenate([sol[:, hd:].astype(BF16), p[3]], axis=0), st.astype(BF16))
                 for sol, p, st in zip(sols, pre, states)]
        v_news = [(sol[:, :hd] - wq[:c]).astype(BF16) for sol, wq in zip(sols, ws_qs)]
        outs = [wq[c:] + _dot(p[1], vn) for wq, p, vn in zip(ws_qs, pre, v_news)]
        new_states = [st * p[5] + _dot_tn(p[4], vn) for st, p, vn in zip(states, pre, v_news)]
        for h, o in zip(hs, outs):
            on = o * lax.rsqrt(jnp.mean(o * o, axis=-1, keepdims=True) + RMS_EPS) * nw_ref[...]
            o_ref[h] = (on * _silu(gate_ref[h])).astype(o_ref.dtype)
        for h, state in zip(hs, new_states):
            s_ref[0, h] = state

    for hg in range(heads // group):
        group_body(hg)

    for s, ref in enumerate(srcs):
        for j in range(heads):
            xbuf_ref[s * heads + j, 0:CONV_PAD, :] = ref[j, tile - CONV_PAD:tile, :]


def _delta_prompt(z3, ab, conv_w3, a_log, dt_bias, norm_w, cache, batch, seq, tile=DELTA_TILE, group=C_HEADS):
    m = z3.shape[1]
    heads = C_HEADS
    nt = seq // tile
    depth, nb, nbuf = cache.shape[:3]
    half = nbuf // 2
    ncopy = depth * nb * 2
    assert (batch * nt) % ncopy == 0 and half % CACHE_COPY_ROWS == 0
    steps_per_copy = batch * nt // ncopy

    def spec(cb0):
        return pl.BlockSpec((heads, tile, LANE), lambda b, t: (cb0 // heads, b * nt + t, 0))

    def full(a):
        return pl.BlockSpec(a.shape, lambda b, t: (0,) * a.ndim)

    def window(b, t):
        cidx = (b * nt + t) // steps_per_copy
        return cidx // (2 * nb), (cidx // 2) % nb, cidx % 2

    def cache_in(b, t):
        l, w, hf = window(b, t)
        return l, w, jnp.where(hf == 0, 1, half), 0, 0

    def cache_out(b, t):
        l, w, hf = window(b, t)
        return l, w, hf, 0, 0

    small = (conv_w3, a_log.reshape(1, heads), dt_bias.reshape(1, heads), norm_w.reshape(1, LANE))
    return pl.pallas_call(
        functools.partial(_delta_prompt_body, group=group, steps_per_copy=steps_per_copy),
        out_shape=(jax.ShapeDtypeStruct((heads, m, LANE), BF16),
                   jax.ShapeDtypeStruct((batch, heads, LANE, LANE), F32),
                   jax.ShapeDtypeStruct(cache.shape, cache.dtype)),
        grid=(batch, nt),
        in_specs=[spec(CB_QC), spec(CB_KC), spec(CB_VC), spec(CB_GC),
                  pl.BlockSpec((tile, 2 * heads), lambda b, t: (b * nt + t, 0))] + [full(a) for a in small]
        + [pl.BlockSpec(tuple(pl.Element(n) for n in (1, 1, half) + cache.shape[3:]), cache_in)],
        out_specs=(pl.BlockSpec((heads, tile, LANE), lambda b, t: (0, b * nt + t, 0)),
                   pl.BlockSpec((1, heads, LANE, LANE), lambda b, t: (b, 0, 0, 0)),
                   pl.BlockSpec((None, None, half) + cache.shape[3:], cache_out)),
        scratch_shapes=[pltpu.VMEM((3 * heads, 2 * CONV_PAD, LANE), F32),
                        pltpu.VMEM((heads, tile), F32)],
        compiler_params=_params("arbitrary", "arbitrary"),
        name="delta_prompt",
    )(z3, z3, z3, z3, ab, *small, cache)


def _attn_sample_body(z_ref, *refs, npat):
    kv_refs, o_ref = refs[:2 * npat], refs[2 * npat]
    b = pl.program_id(0)
    heads = A_HEADS
    n = A_BLOCK

    @pl.when(b == 0)
    def _():
        o_ref[...] = jnp.zeros_like(o_ref)

    row = pl.ds(b, 1)

    def head_rows(cb0):
        return jnp.concatenate([z_ref[cb0 + h, row, :] for h in range(heads)], axis=0)

    q = head_rows(CB_QA).astype(BF16)
    kn = head_rows(CB_KA).astype(BF16).astype(F32)
    vn = head_rows(CB_VA).astype(BF16).astype(F32)
    gate = head_rows(CB_GA)
    scale = LANE ** -0.5
    qpad = jnp.concatenate([q, jnp.zeros((MXU_ROWS - heads, LANE), BF16)], axis=0)
    s_new = jnp.sum(q.astype(F32) * kn, axis=1, keepdims=True) * scale
    col = lax.broadcasted_iota(jnp.int32, (heads, n * heads), 1)
    own = (col & (heads - 1)) == lax.broadcasted_iota(jnp.int32, (heads, n * heads), 0)
    parts = []
    for i in range(npat):
        kmat = kv_refs[2 * i][...].reshape(n * heads, LANE).astype(BF16)
        vmat = kv_refs[2 * i + 1][...].reshape(n * heads, LANE).astype(BF16)
        s = jnp.where(own, _dot_nt(qpad, kmat)[:heads] * scale, -jnp.inf)
        m = jnp.maximum(jnp.max(s, axis=1, keepdims=True), s_new)
        p = jnp.exp(s - m)
        pn = jnp.exp(s_new - m)
        l = jnp.sum(p, axis=1, keepdims=True) + pn
        ppad = jnp.concatenate([p, jnp.zeros((MXU_ROWS - heads, n * heads), F32)], axis=0).astype(BF16)
        o = _dot(ppad, vmat)[:heads] + pn.astype(BF16).astype(F32) * vn
        parts.append((o, m, l))
    m_all = functools.reduce(jnp.maximum, [m for _, m, _ in parts])
    num = jnp.zeros((heads, LANE), F32)
    den = jnp.zeros((heads, 1), F32)
    for o, m, l in parts:
        wgt = jnp.exp(m - m_all)
        num = num + o * wgt
        den = den + l * wgt
    out = num / den * _silu(gate)
    for h in range(heads):
        o_ref[h, row, :] = out[h:h + 1]


def _attn_sample(z3s, cache_k, cache_v, layer):
    depth, nb, nbuf, heads, hd = cache_k.shape
    rows = z3s.shape[1]
    n = A_BLOCK
    assert heads == A_HEADS and hd == LANE
    assert all(nbuf % window == 0 for window, _ in A_PATTERNS), "cached window shorter than a pattern"
    views, specs = [], []
    for window, d in A_PATTERNS:
        last = nbuf // d // n - 1
        spec = pl.BlockSpec((None, None, n, None, heads, hd), lambda b, last=last: (layer, b, last, 0, 0, 0))
        for cache in (cache_k, cache_v):
            views.append(cache.reshape(depth, nb, nbuf // d, d, heads, hd))
            specs.append(spec)
    return pl.pallas_call(
        functools.partial(_attn_sample_body, npat=len(A_PATTERNS)),
        out_shape=jax.ShapeDtypeStruct((heads, rows, LANE), F32),
        grid=(nb,),
        in_specs=[pl.BlockSpec(z3s.shape, lambda b: (0, 0, 0))] + specs,
        out_specs=pl.BlockSpec((heads, rows, LANE), lambda b: (0, 0, 0)),
        compiler_params=_params("arbitrary"),
        name="attn_sample",
    )(z3s, *views)


def _state_sample_body(z_ref, ab_ref, pool_ref, conv_ref, st_ref, pw_ref, psc_ref, cw_ref, alog_ref, dtb_ref,
                       nw_ref, mb_ref, mc_ref, so_ref):
    b = pl.program_id(0)
    heads = C_HEADS
    hd = LANE

    @pl.when(b == 0)
    def _():
        mb_ref[...] = jnp.zeros_like(mb_ref)
        mc_ref[...] = jnp.zeros_like(mc_ref)

    row = pl.ds(b, 1)

    gw = pw_ref.shape[1]
    cpg = gw // LANE
    for gi, width in enumerate(B_POOLS):
        ys = []
        for j in range(cpg):
            cb = gi * cpg + j
            un = z_ref[CB_UB + cb, row, :]
            prev = pool_ref[0, B_BUF - (width - 1):B_BUF, cb * LANE:(cb + 1) * LANE]
            mean = (jnp.sum(prev, axis=0, keepdims=True) + un) / float(width)
            ys.append(mean - un)
        y = jnp.broadcast_to(jnp.concatenate(ys, axis=1), (MXU_ROWS, gw)).astype(BF16)
        out = _dot(y, pw_ref[gi])[0:1] * psc_ref[:, gi * gw:(gi + 1) * gw]
        for j in range(cpg):
            cb = gi * cpg + j
            mb_ref[cb, row, :] = out[:, j * LANE:(j + 1) * LANE] * _silu(z_ref[CB_GB + cb, row, :])

    arow = ab_ref[row, :]
    eye = lax.broadcasted_iota(jnp.int32, (hd, hd), 0) == lax.broadcasted_iota(jnp.int32, (hd, hd), 1)

    def column(x):
        return jnp.sum(jnp.where(eye, jnp.broadcast_to(x, (hd, hd)), 0.0), axis=1, keepdims=True)

    def conv(j):
        st = conv_ref[0, j]
        w = cw_ref[j]
        y = st[0:1] * w[0:1]
        for i in range(1, C_CONV - 1):
            y = y + st[i:i + 1] * w[i:i + 1]
        y = y + z_ref[CB_QC + j, row, :] * w[C_CONV - 1:C_CONV]
        return _silu(y)

    def l2n(x):
        return x * lax.rsqrt(jnp.sum(x * x, axis=-1, keepdims=True) + RMS_EPS)

    for h in range(heads):
        q = l2n(conv(h)) * (hd ** -0.5)
        k = l2n(conv(heads + h))
        v = conv(2 * heads + h)
        g = -jnp.exp(alog_ref[:, h:h + 1]) * _softplus(arow[:, h:h + 1] + dtb_ref[:, h:h + 1])
        beta = _sigmoid(arow[:, heads + h:heads + h + 1])
        state = st_ref[0, h] * jnp.exp(g)
        kcol = column(k)
        delta = (v - jnp.sum(kcol * state, axis=0, keepdims=True)) * beta
        state = state + kcol * delta
        o = jnp.sum(column(q) * state, axis=0, keepdims=True)
        on = o * lax.rsqrt(jnp.mean(o * o, axis=-1, keepdims=True) + RMS_EPS) * nw_ref[...]
        mc_ref[h, row, :] = on * _silu(z_ref[CB_GC + h, row, :])
        so_ref[0, h] = state


def _state_sample(z3s, ab_s, state_pool, state_conv3, state_delta, layer, pool_w, pool_scale, conv_w3, a_log,
                  dt_bias, norm_w):
    nb = state_pool.shape[1]
    rows = z3s.shape[1]
    heads = C_HEADS
    bw = state_pool.shape[-1]

    def full(a):
        return pl.BlockSpec(a.shape, lambda b: (0,) * a.ndim)

    def per_b(a):
        return pl.BlockSpec((1,) + a.shape[1:], lambda b: (b,) + (0,) * (a.ndim - 1))

    def layer_b(a):
        return pl.BlockSpec((None, 1) + a.shape[2:], lambda b: (layer, b) + (0,) * (a.ndim - 2))

    small = (pool_w.astype(BF16), pool_scale.reshape(1, bw), conv_w3, a_log.reshape(1, heads),
             dt_bias.reshape(1, heads), norm_w.reshape(1, LANE))
    state_shape = jax.ShapeDtypeStruct(state_delta.shape[1:], F32)
    return pl.pallas_call(
        _state_sample_body,
        out_shape=(jax.ShapeDtypeStruct((bw // LANE, rows, LANE), F32),
                   jax.ShapeDtypeStruct((heads, rows, LANE), F32),
                   state_shape),
        grid=(nb,),
        in_specs=[full(z3s), full(ab_s), layer_b(state_pool), per_b(state_conv3), layer_b(state_delta)]
        + [full(a) for a in small],
        out_specs=(pl.BlockSpec((bw // LANE, rows, LANE), lambda b: (0, 0, 0)),
                   pl.BlockSpec((heads, rows, LANE), lambda b: (0, 0, 0)),
                   per_b(state_shape)),
        compiler_params=_params("arbitrary"),
        name="state_sample",
    )(z3s, ab_s, state_pool, state_conv3, state_delta, *small)


def _cache_insert_body(c_ref, n_ref, o_ref):
    del c_ref
    o_ref[...] = n_ref[...]


def _cache_insert(shifted, new_rows):
    depth, nb, nbuf, heads, hd = shifted.shape
    return pl.pallas_call(
        _cache_insert_body,
        out_shape=jax.ShapeDtypeStruct(shifted.shape, shifted.dtype),
        grid=(depth, nb),
        in_specs=[pl.BlockSpec(memory_space=pl.ANY),
                  pl.BlockSpec((None, None, 1, heads, hd), lambda l, b: (l, b, 0, 0, 0))],
        out_specs=pl.BlockSpec((None, None, 1, heads, hd), lambda l, b: (l, b, nbuf - 1, 0, 0)),
        input_output_aliases={0: 0},
        compiler_params=_params("arbitrary", "arbitrary"),
        name="cache_insert",
    )(shifted, new_rows)


def _cols(z4, cb0, ncb, r0, r1):
    blk = z4[cb0:cb0 + ncb, :, r0:r1]
    return jnp.transpose(blk, (1, 2, 0, 3)).reshape(blk.shape[1], r1 - r0, ncb * LANE)


def kernel(x_prompt, x_sample, cache_win_k, cache_win_v, state_pool, state_conv, state_delta, norm_w, w_in,
           conv_w, a_log, dt_bias, delta_norm_w, pool_w, pool_scale, w_out, final_norm_w):
    batch, seq, d_model = x_prompt.shape
    nb = x_sample.shape[0]
    depth = w_in.shape[0]
    heads = C_HEADS
    m = batch * seq
    rows_s = SAMPLE_ROWS
    nbuf = cache_win_k.shape[2]
    pool_cb = state_pool.shape[-1] // LANE

    hp = x_prompt.reshape(m, d_model)
    hs = jnp.zeros((rows_s, d_model), F32).at[:nb].set(x_sample.reshape(nb, d_model))

    assert nbuf == A_PATTERNS[-1][0], "the window cache is expected full: one row in, one row out"

    w_in_t = jnp.swapaxes(w_in, 1, 2)
    outs_p = [[] for _ in range(5)]
    outs_s = [[] for _ in range(5)]
    new_k, new_v = [], []
    assert depth == 2, "each of the two layers' delta-rule calls also moves one of the two window caches"
    shifted = [None, None]
    kv_nat = ()
    for l in range(depth):
        conv_w3 = jnp.transpose(conv_w[l].reshape(C_CONV, 3 * heads, LANE), (1, 0, 2))

        h = _rmsnorm(hp, norm_w[l], BF16, NORM_ROWS_BF16)
        h_s = _rmsnorm(hs, norm_w[l], BF16, rows_s)
        z3, ab, z3s, ab_s = _inproj(h, h_s, w_in_t, l, PROJ_ROWS, PROJ_COLS)
        mix_a, *kv_nat = _attn_prompt(z3, batch, seq, earlier=kv_nat)
        mix_b = _pool_prompt(z3, pool_w[l], pool_scale[l], batch, seq)
        mix_c, s_new, shifted[l] = _delta_prompt(z3, ab, conv_w3, a_log[l], dt_bias[l], delta_norm_w[l],
                                                 (cache_win_k, cache_win_v)[l], batch, seq)
        z4 = z3.reshape(CB_TOTAL, batch, seq, LANE)
        outs_p[2].append(_cols(z4, CB_UB, pool_cb, seq - B_BUF, seq))
        outs_p[3].append(_cols(z4, CB_QC, 3 * heads, seq - (C_CONV - 1), seq))
        outs_p[4].append(s_new)

        mix_as = _attn_sample(z3s, cache_win_k, cache_win_v, l)
        conv3 = jnp.transpose(state_conv[l].reshape(nb, C_CONV - 1, 3 * heads, LANE), (0, 2, 1, 3))
        mix_bs, mix_cs, st_new = _state_sample(z3s, ab_s, state_pool, conv3, state_delta, l, pool_w[l],
                                               pool_scale[l], conv_w3, a_log[l], dt_bias[l], delta_norm_w[l])
        hp, hs = _outproj((mix_a, mix_b, mix_c), (mix_as, mix_bs, mix_cs), w_out, l, hp, hs,
                           PROJ_ROWS, PROJ_COLS)

        z4s = z3s[:, :nb].reshape(CB_TOTAL, nb, 1, LANE)
        new_k.append(_cols(z4s, CB_KA, A_HEADS, 0, 1).reshape(nb, 1, A_HEADS, LANE))
        new_v.append(_cols(z4s, CB_VA, A_HEADS, 0, 1).reshape(nb, 1, A_HEADS, LANE))
        outs_s[2].append(jnp.concatenate([state_pool[l], _cols(z4s, CB_UB, pool_cb, 0, 1)], axis=1)[:, 1:])
        outs_s[3].append(jnp.concatenate([state_conv[l], _cols(z4s, CB_QC, 3 * heads, 0, 1)], axis=1)[:, 1:])
        outs_s[4].append(st_new)

    y_prompt = _rmsnorm(hp, final_norm_w, F32, NORM_ROWS_F32).reshape(batch, seq, d_model)
    y_sample = _rmsnorm(hs, final_norm_w, F32, rows_s)[:nb].reshape(nb, 1, d_model)
    stack = lambda xs: jnp.stack(xs, axis=0)
    win_k = _cache_insert(shifted[0], stack(new_k))
    win_v = _cache_insert(shifted[1], stack(new_v))
    keep_p = min(seq, A_PATTERNS[-1][0])
    win_p = tuple(a.reshape(depth, batch, seq, A_HEADS, LANE)[:, :, seq - keep_p:] for a in kv_nat)
    return ((y_prompt, y_sample) + win_p + tuple(stack(o) for o in outs_p[2:]) + (win_k, win_v)
            + tuple(stack(o) for o in outs_s[2:]))
```
